```python
import math, functools
import jax, jax.numpy as jnp
from jax import lax
import numpy as np

D_MODEL = 2048
BATCH = 32
SEQ = 256
DEPTH = 4
DEC_BATCH = 4
DEC_SEQ = 4096
PAST_LEN = 512

GRID_W = 64
N_BRANCH = 4
BRANCH_W = 512
MLA_HEADS = 8
MLA_NOPE = 64
MLA_ROPE = 32
MLA_V = 64
MLA_Q_LORA = 512
MLA_KV_LORA = 256
NA_HEADS = 8
NA_HD = 64
NA_WIN_R = 8
NA_WIN_C = 16
POOL_WINDOWS = (2, 4, 8, 16)
N_POOL_GROUPS = 4
POOL_GROUP = 128
POOL_W = N_POOL_GROUPS * POOL_GROUP
DIFF_HEADS = 4
DIFF_HD = 64
D_FF = 4 * D_MODEL
Q_BLOCK = 128
ROPE_THETA = 10000.0
RMS_EPS = 1e-6
NEG_INF = -1e30
NA_W = NA_HEADS * NA_HD
DIFF_QK_W = DIFF_HEADS * 2 * DIFF_HD
DIFF_V_W = DIFF_HEADS * 2 * DIFF_HD
MLA_SCALE = (MLA_NOPE + MLA_ROPE) ** -0.5
IN_SIZES = (MLA_Q_LORA, MLA_KV_LORA, MLA_ROPE, NA_W, NA_W, NA_W, POOL_W, DIFF_QK_W, DIFF_QK_W, DIFF_V_W, N_BRANCH * D_MODEL)
IN_COLS = sum(IN_SIZES)
IN_SPLITS = tuple(sum(IN_SIZES[:i + 1]) for i in range(len(IN_SIZES) - 1))

kernel_name = 'hybrid_flow_trunk_step'


def rms_norm(x, g):
    xf = x.astype(jnp.float32)
    y = xf * lax.rsqrt(jnp.mean(jnp.square(xf), axis=-1, keepdims=True) + RMS_EPS)
    return (y * g.astype(jnp.float32)).astype(x.dtype)


def split_heads(x, n):
    return x.reshape(*x.shape[:-1], n, x.shape[-1] // n)


def axial_rope(n_tokens, rot_dim):
    t = jnp.arange(n_tokens)
    row = (t // GRID_W).astype(jnp.float32)
    col = (t % GRID_W).astype(jnp.float32)
    n_freq = rot_dim // 4
    inv = ROPE_THETA ** (-jnp.arange(n_freq, dtype=jnp.float32) / n_freq)
    ang = jnp.concatenate([row[:, None] * inv, col[:, None] * inv], axis=-1)
    return jnp.cos(ang), jnp.sin(ang)


def apply_rope(x, cos, sin):
    xf = x.astype(jnp.float32)
    x1, x2 = jnp.split(xf, 2, axis=-1)
    c = cos[:, None, :]
    s = sin[:, None, :]
    return jnp.concatenate([x1 * c - x2 * s, x2 * c + x1 * s], axis=-1).astype(x.dtype)


def _query_blocks(x):
    b, t = x.shape[:2]
    return jnp.moveaxis(x.reshape(b, t // Q_BLOCK, Q_BLOCK, *x.shape[2:]), 1, 0)


def _merge_blocks(y):
    y = jnp.moveaxis(y, 0, 1)
    return y.reshape(y.shape[0], -1, *y.shape[3:])


def _probs(q, k, scale):
    s = jnp.einsum('bqhd,bkhd->bhqk', q, k, preferred_element_type=jnp.float32) * scale
    return jax.nn.softmax(s, axis=-1)


def attend_blocked(q, k, v, scale):
    def one_block(qb):
        p = _probs(qb, k, scale).astype(v.dtype)
        return jnp.einsum('bhqk,bkhd->bqhd', p, v)
    return _merge_blocks(lax.map(one_block, _query_blocks(q)))


def mla_compress(q_c, kv_c, lp):
    q = rms_norm(q_c, lp['g_q_lora']) @ lp['w_uq']
    return split_heads(q, MLA_HEADS), rms_norm(kv_c, lp['g_kv_lora'])


def mla_expand(ckv, k_r, lp):
    k_nope, v = jnp.split(split_heads(ckv @ lp['w_ukv'], MLA_HEADS), [MLA_NOPE], axis=-1)
    k_rope = jnp.broadcast_to(k_r[:, :, None, :], k_nope.shape[:-1] + (MLA_ROPE,))
    return jnp.concatenate([k_nope, k_rope], axis=-1), v


def diff_attention(q1, q2, k1, k2, v, lam_p, norm_g, li):
    b, t = q1.shape[:2]
    lam_init = 0.8 - 0.6 * math.exp(-0.3 * li)
    lp32 = lam_p.astype(jnp.float32)
    lam = jnp.exp(jnp.sum(lp32[0] * lp32[1])) - jnp.exp(jnp.sum(lp32[2] * lp32[3])) + lam_init
    scale = DIFF_HD ** -0.5
    def one_block(qs):
        a, c2 = qs
        p = _probs(a, k1, scale) - lam * _probs(c2, k2, scale)
        return jnp.einsum('bhqk,bkhd->bqhd', p.astype(v.dtype), v)
    o = _merge_blocks(lax.map(one_block, (_query_blocks(q1), _query_blocks(q2))))
    o = rms_norm(o, norm_g) * (1.0 - lam_init)
    return o.reshape(b, t, -1)


def neighbourhood_attention(q, k, v, k_ctx, v_ctx, rpb):
    b, t, h, dh = q.shape
    rows = t // GRID_W
    wr = min(NA_WIN_R, rows)
    ncb = GRID_W // NA_WIN_C
    gw = 2 * NA_WIN_C
    r = jnp.arange(rows)
    row_idx = jnp.clip(r - wr // 2, 0, rows - wr)[:, None] + jnp.arange(wr)
    j = jnp.arange(ncb)
    col_idx = jnp.clip(j * NA_WIN_C - NA_WIN_C // 2, 0, GRID_W - gw)[:, None] + jnp.arange(gw)
    qcol = jnp.arange(GRID_W).reshape(ncb, NA_WIN_C)
    col_start = jnp.clip(qcol - NA_WIN_C // 2, 0, GRID_W - NA_WIN_C)
    kc = col_idx[:, None, :]
    valid = (kc >= col_start[..., None]) & (kc < col_start[..., None] + NA_WIN_C)
    dr = row_idx - r[:, None] + NA_WIN_R - 1
    dc = jnp.clip(kc - qcol[..., None] + NA_WIN_C - 1, 0, 2 * NA_WIN_C - 2)
    bias = rpb[:, dr[:, None, None, :, None], dc[None, :, :, None, :]].astype(jnp.float32)
    bias = jnp.where(valid[None, None, :, :, None, :], bias, NEG_INF)
    bias = bias.reshape(h, rows, ncb, NA_WIN_C, wr * gw).transpose(1, 2, 0, 3, 4)
    gidx = (row_idx[:, None, :, None], col_idx[None, :, None, :])
    kg = k.reshape(b, rows, GRID_W, h, dh)[:, gidx[0], gidx[1]].reshape(b, rows, ncb, wr * gw, h, dh)
    vg = v.reshape(b, rows, GRID_W, h, dh)[:, gidx[0], gidx[1]].reshape(b, rows, ncb, wr * gw, h, dh)
    qg = q.reshape(b, rows, ncb, NA_WIN_C, h, dh)
    scale = dh ** -0.5
    s_win = jnp.einsum('brjqhd,brjkhd->brjhqk', qg, kg, preferred_element_type=jnp.float32) * scale + bias[None]
    s_ctx = jnp.einsum('brjqhd,blhd->brjhql', qg, k_ctx, preferred_element_type=jnp.float32) * scale
    p = jax.nn.softmax(jnp.concatenate([s_win, s_ctx], axis=-1), axis=-1).astype(v.dtype)
    nw = wr * gw
    o = (jnp.einsum('brjhqk,brjkhd->brjqhd', p[..., :nw], vg)
         + jnp.einsum('brjhql,blhd->brjqhd', p[..., nw:], v_ctx))
    return o.reshape(b, t, h * dh)


def multiscale_pool(u, pool_w, pool_scale):
    b, t, _ = u.shape
    uf = u.astype(jnp.float32)
    csum = jnp.concatenate([jnp.zeros((b, 1, POOL_W), jnp.float32), jnp.cumsum(uf, axis=1)], axis=1)
    pos = jnp.arange(t)
    groups = []
    for gi, w in enumerate(POOL_WINDOWS):
        sl = slice(gi * POOL_GROUP, (gi + 1) * POOL_GROUP)
        lo = jnp.clip(pos - w // 2, 0, t)
        hi = jnp.clip(pos + w // 2, 0, t)
        mean = (csum[:, hi, sl] - csum[:, lo, sl]) / (hi - lo).astype(jnp.float32)[None, :, None]
        groups.append(mean - uf[:, :, sl])
    pooled = jnp.stack(groups, axis=2).astype(u.dtype)
    mixed = jnp.einsum('btgc,gcd->btgd', pooled, pool_w).reshape(b, t, POOL_W)
    return mixed * pool_scale


def merge_branches(outs, gates, w_br, w_o):
    g = jax.nn.sigmoid(gates.astype(jnp.float32)).astype(gates.dtype)
    gs = jnp.split(g, N_BRANCH, axis=-1)
    merged = gs[0] * (outs[0] @ w_br[0])
    for bi in range(1, N_BRANCH):
        merged = merged + gs[bi] * (outs[bi] @ w_br[bi])
    return merged @ w_o


def context_mixers(h, lp, li):
    b, l, _ = h.shape
    (q_c, kv_c, k_r, na_q, na_k, na_v, pool_in, dq, dk, dv, gates) = jnp.split(h @ lp['w_in'], IN_SPLITS, axis=-1)
    q, ckv = mla_compress(q_c, kv_c, lp)
    k_a, v_a = mla_expand(ckv, k_r, lp)
    o_a = attend_blocked(q, k_a, v_a, MLA_SCALE).reshape(b, l, -1)
    na_k = split_heads(na_k, NA_HEADS)
    na_v = split_heads(na_v, NA_HEADS)
    o_b = attend_blocked(split_heads(na_q, NA_HEADS), na_k, na_v, NA_HD ** -0.5).reshape(b, l, -1)
    o_c = multiscale_pool(pool_in, lp['pool_w'], lp['pool_scale'])
    dq = split_heads(dq, DIFF_HEADS)
    dk = split_heads(dk, DIFF_HEADS)
    dv = split_heads(dv, DIFF_HEADS)
    o_d = diff_attention(dq[..., :DIFF_HD], dq[..., DIFF_HD:], dk[..., :DIFF_HD], dk[..., DIFF_HD:], dv,
                         lp['diff_lambda'], lp['diff_norm_g'], li)
    y = merge_branches((o_a, o_b, o_c, o_d), gates, lp['w_br'], lp['w_o'])
    return y, (ckv, k_r, na_k, na_v, dk, dv)


def latent_mixers(h, lp, li, ctx, rope_mla, rope_diff):
    ckv_c, kr_c, nak_c, nav_c, dk_c, dv_c = ctx
    b, t, _ = h.shape
    (q_c, kv_c, k_r, na_q, na_k, na_v, pool_in, dq, dk, dv, gates) = jnp.split(h @ lp['w_in'], IN_SPLITS, axis=-1)
    q, ckv = mla_compress(q_c, kv_c, lp)
    q = jnp.concatenate([q[..., :MLA_NOPE], apply_rope(q[..., MLA_NOPE:], *rope_mla)], axis=-1)
    k_r = apply_rope(k_r[:, :, None, :], *rope_mla)[:, :, 0, :]
    k_lat, v_lat = mla_expand(ckv, k_r, lp)
    k_ctx, v_ctx = mla_expand(ckv_c, kr_c, lp)
    o_a = attend_blocked(q, jnp.concatenate([k_lat, k_ctx], axis=1), jnp.concatenate([v_lat, v_ctx], axis=1),
                         MLA_SCALE).reshape(b, t, -1)
    o_b = neighbourhood_attention(split_heads(na_q, NA_HEADS), split_heads(na_k, NA_HEADS),
                                  split_heads(na_v, NA_HEADS), nak_c, nav_c, lp['na_rpb'])
    o_c = multiscale_pool(pool_in, lp['pool_w'], lp['pool_scale'])
    dq = split_heads(dq, DIFF_HEADS)
    dk = split_heads(dk, DIFF_HEADS)
    q1 = apply_rope(dq[..., :DIFF_HD], *rope_diff)
    q2 = apply_rope(dq[..., DIFF_HD:], *rope_diff)
    k1 = jnp.concatenate([apply_rope(dk[..., :DIFF_HD], *rope_diff), dk_c[..., :DIFF_HD]], axis=1)
    k2 = jnp.concatenate([apply_rope(dk[..., DIFF_HD:], *rope_diff), dk_c[..., DIFF_HD:]], axis=1)
    vd = jnp.concatenate([split_heads(dv, DIFF_HEADS), dv_c], axis=1)
    o_d = diff_attention(q1, q2, k1, k2, vd, lp['diff_lambda'], lp['diff_norm_g'], li)
    y = merge_branches((o_a, o_b, o_c, o_d), gates, lp['w_br'], lp['w_o'])
    return y, None


def trunk_layer(x, cond, lp, mixer):
    mod = jax.nn.silu(cond) @ lp['w_mod'] + lp['b_mod']
    sh1, sc1, g1, sh2, sc2, g2 = jnp.split(mod[:, None, :], 6, axis=-1)
    gn = lp['g_norm']
    h = rms_norm(x, gn[0]) * (1.0 + sc1) + sh1
    y, extra = mixer(h)
    x = x + g1 * rms_norm(y, gn[1])
    h = rms_norm(x, gn[2]) * (1.0 + sc2) + sh2
    y = jnp.square(jax.nn.relu(h @ lp['w_up'])) @ lp['w_down']
    x = x + g2 * rms_norm(y, gn[3])
    return x, extra


def setup_inputs(seed: int = 0) -> dict:
    key = jax.random.key(seed)
    ks = jax.random.split(key, 32)
    def nrm(i, shape, scale=1.0):
        return jax.random.normal(ks[i], shape, jnp.float32) * scale
    return {
        'x_prompt': nrm(0, (BATCH, SEQ, D_MODEL)),
        'x_sample': nrm(1, (DEC_BATCH, DEC_SEQ, D_MODEL)),
        'cache_mla_ckv': nrm(2, (DEC_BATCH, DEPTH, PAST_LEN, MLA_KV_LORA)),
        'cache_mla_krope': nrm(3, (DEC_BATCH, DEPTH, PAST_LEN, MLA_ROPE)),
        'cache_na_k': nrm(4, (DEC_BATCH, DEPTH, PAST_LEN, NA_HEADS, NA_HD)),
        'cache_na_v': nrm(5, (DEC_BATCH, DEPTH, PAST_LEN, NA_HEADS, NA_HD)),
        'cache_diff_k': nrm(6, (DEC_BATCH, DEPTH, PAST_LEN, DIFF_HEADS, 2 * DIFF_HD)),
        'cache_diff_v': nrm(7, (DEC_BATCH, DEPTH, PAST_LEN, DIFF_HEADS, 2 * DIFF_HD)),
        'c': nrm(8, (DEC_BATCH, D_MODEL)),
        'c_ctx': nrm(9, (D_MODEL,)),
        'w_mod': nrm(10, (DEPTH, D_MODEL, 6 * D_MODEL), 0.5 * D_MODEL ** -0.5),
        'b_mod': nrm(11, (DEPTH, 6 * D_MODEL), 0.01),
        'g_norm': 1.0 + nrm(12, (DEPTH, 4, D_MODEL), 0.02),
        'w_in': nrm(13, (DEPTH, D_MODEL, IN_COLS), D_MODEL ** -0.5),
        'g_q_lora': 1.0 + nrm(14, (DEPTH, MLA_Q_LORA), 0.02),
        'g_kv_lora': 1.0 + nrm(15, (DEPTH, MLA_KV_LORA), 0.02),
        'w_uq': nrm(16, (DEPTH, MLA_Q_LORA, MLA_HEADS * (MLA_NOPE + MLA_ROPE)), MLA_Q_LORA ** -0.5),
        'w_ukv': nrm(17, (DEPTH, MLA_KV_LORA, MLA_HEADS * (MLA_NOPE + MLA_V)), MLA_KV_LORA ** -0.5),
        'na_rpb': nrm(18, (DEPTH, NA_HEADS, 2 * NA_WIN_R - 1, 2 * NA_WIN_C - 1), 0.1),
        'pool_w': nrm(19, (DEPTH, N_POOL_GROUPS, POOL_GROUP, POOL_GROUP), POOL_GROUP ** -0.5),
        'pool_scale': 1.0 + nrm(20, (DEPTH, POOL_W), 0.1),
        'diff_lambda': nrm(21, (DEPTH, 4, DIFF_HD), 0.1),
        'diff_norm_g': 1.0 + nrm(22, (DEPTH, 2 * DIFF_HD), 0.02),
        'w_br': nrm(23, (DEPTH, N_BRANCH, BRANCH_W, D_MODEL), BRANCH_W ** -0.5),
        'w_o': nrm(24, (DEPTH, D_MODEL, D_MODEL), D_MODEL ** -0.5),
        'w_up': nrm(25, (DEPTH, D_MODEL, D_FF), D_MODEL ** -0.5),
        'w_down': nrm(26, (DEPTH, D_FF, D_MODEL), D_FF ** -0.5),
    }


def reference(x_prompt, x_sample, cache_mla_ckv, cache_mla_krope, cache_na_k, cache_na_v, cache_diff_k,
              cache_diff_v, c, c_ctx, w_mod, b_mod, g_norm, w_in, g_q_lora, g_kv_lora, w_uq, w_ukv, na_rpb,
              pool_w, pool_scale, diff_lambda, diff_norm_g, w_br, w_o, w_up, w_down):
    params = dict(w_mod=w_mod, b_mod=b_mod, g_norm=g_norm, w_in=w_in, g_q_lora=g_q_lora, g_kv_lora=g_kv_lora,
                  w_uq=w_uq, w_ukv=w_ukv, na_rpb=na_rpb, pool_w=pool_w, pool_scale=pool_scale,
                  diff_lambda=diff_lambda, diff_norm_g=diff_norm_g, w_br=w_br, w_o=w_o, w_up=w_up, w_down=w_down)
    xp = x_prompt
    cond_ctx = c_ctx[None, :]
    ctx_states = []
    for i in range(DEPTH):
        lp = {name: arr[i] for name, arr in params.items()}
        xp, ctx_i = trunk_layer(xp, cond_ctx, lp, functools.partial(context_mixers, lp=lp, li=i))
        ctx_states.append(ctx_i)
    state_mla_ckv = jnp.stack([s[0] for s in ctx_states], axis=1)
    state_mla_krope = jnp.stack([s[1] for s in ctx_states], axis=1)
    state_na_k = jnp.stack([s[2] for s in ctx_states], axis=1)
    state_na_v = jnp.stack([s[3] for s in ctx_states], axis=1)
    state_diff_k = jnp.stack([s[4] for s in ctx_states], axis=1)
    state_diff_v = jnp.stack([s[5] for s in ctx_states], axis=1)
    n_lat = x_sample.shape[1]
    rope_mla = axial_rope(n_lat, MLA_ROPE)
    rope_diff = axial_rope(n_lat, DIFF_HD)
    xs = x_sample
    for i in range(DEPTH):
        lp = {name: arr[i] for name, arr in params.items()}
        ctx_i = (cache_mla_ckv[:, i], cache_mla_krope[:, i], cache_na_k[:, i], cache_na_v[:, i],
                 cache_diff_k[:, i], cache_diff_v[:, i])
        xs, _ = trunk_layer(xs, c, lp, functools.partial(latent_mixers, lp=lp, li=i, ctx=ctx_i,
                                                         rope_mla=rope_mla, rope_diff=rope_diff))
    return (xp, xs, state_mla_ckv, state_mla_krope, state_na_k, state_na_v, state_diff_k, state_diff_v)
```

```python
import functools
import math

import numpy as np
import jax
import jax.numpy as jnp
from jax import lax
from jax.experimental import pallas as pl
from jax.experimental.pallas import tpu as pltpu

F32 = jnp.float32
BF16 = jnp.bfloat16

GRID_W = 64
MLA_HEADS = 8
MLA_NOPE = 64
MLA_ROPE = 32
MLA_V = 64
MLA_Q_LORA = 512
MLA_KV_LORA = 256
NA_HEADS = 8
NA_HD = 64
NA_WIN_R = 8
NA_WIN_C = 16
POOL_WINDOWS = (2, 4, 8, 16)
POOL_GROUP = 128
DIFF_HEADS = 4
DIFF_HD = 64
ROPE_THETA = 10000.0
RMS_EPS = 1e-6
NEG_INF = -1e30
MLA_SCALE = (MLA_NOPE + MLA_ROPE) ** -0.5
HEAD_SCALE_64 = 64 ** -0.5
LANES = 128
MIB = 1024 * 1024

C_GATES = 0
C_QC = 8192
C_DQ = 8704
C_DK = 9216
C_DV = 9728
C_NQ = 10240
C_NK = 10752
C_NV = 11264
C_POOL = 11776
C_KVC = 12288
C_KR = 12544
Z_COLS = 12672

NA_QROWS = 8
NA_KROWS = NA_QROWS + NA_WIN_R - 1


def _cp(sem, vmem_mib):
    return pltpu.CompilerParams(dimension_semantics=sem, vmem_limit_bytes=vmem_mib * MIB)


def _rms(x, g):
    return x * lax.rsqrt(jnp.mean(x * x, axis=-1, keepdims=True) + RMS_EPS) * g


def _scores(q, k):
    return lax.dot_general(q, k, (((1,), (1,)), ((), ())), preferred_element_type=F32)


def _softmax_parts(ss):
    m = functools.reduce(jnp.maximum, [jnp.max(s, axis=-1, keepdims=True) for s in ss])
    ps = [jnp.exp(s - m) for s in ss]
    l = functools.reduce(jnp.add, [jnp.sum(p, axis=-1, keepdims=True) for p in ps])
    return ps, l


def _pv(ps, vs):
    acc = None
    for p, v in zip(ps, vs):
        t = jnp.dot(p.astype(BF16), v, preferred_element_type=F32)
        acc = t if acc is None else acc + t
    return acc


def _lane_lo():
    return lax.broadcasted_iota(jnp.int32, (1, LANES), 1) < 64


def _mod_kernel(c_ref, w_ref, b_ref, o_ref):
    c = c_ref[...]
    s = c * jax.nn.sigmoid(c)
    o_ref[0] = jnp.dot(s.astype(BF16), w_ref[0].astype(BF16), preferred_element_type=F32) + b_ref[0]


def _modulation(cond, w_mod, b_mod):
    depth, d, n = w_mod.shape
    r = cond.shape[0]
    tn = 1024
    return pl.pallas_call(
        _mod_kernel,
        out_shape=jax.ShapeDtypeStruct((depth, r, n), F32),
        grid=(depth, n // tn),
        in_specs=[pl.BlockSpec((r, d), lambda l, j: (0, 0)),
                  pl.BlockSpec((1, d, tn), lambda l, j: (l, 0, j)),
                  pl.BlockSpec((1, 1, tn), lambda l, j: (l, 0, j))],
        out_specs=pl.BlockSpec((1, r, tn), lambda l, j: (l, 0, j)),
        compiler_params=_cp(("parallel", "parallel"), 40),
    )(cond, w_mod, b_mod.reshape(depth, 1, n))


def _norm_mod_kernel(x_ref, mod_ref, g_ref, h_ref, *, d):
    mod = mod_ref[0]
    sh = mod[:, 0:d]
    sc = mod[:, d:2 * d]
    h_ref[...] = (_rms(x_ref[...], g_ref[...]) * (1.0 + sc) + sh).astype(BF16)


def _norm_mod(x, mod_l, g, row_of):
    t, d = x.shape
    tm = 512
    return pl.pallas_call(
        functools.partial(_norm_mod_kernel, d=d),
        out_shape=jax.ShapeDtypeStruct((t, d), BF16),
        grid=(t // tm,),
        in_specs=[pl.BlockSpec((tm, d), lambda i: (i, 0)),
                  pl.BlockSpec((1, 1, 6 * d), lambda i: (row_of(i, tm), 0, 0)),
                  pl.BlockSpec((1, d), lambda i: (0, 0))],
        out_specs=pl.BlockSpec((tm, d), lambda i: (i, 0)),
        compiler_params=_cp(("parallel",), 40),
    )(x, mod_l, g)


def _mm_kernel(a_ref, b_ref, o_ref):
    o_ref[...] = jnp.dot(a_ref[...], b_ref[...], preferred_element_type=F32).astype(o_ref.dtype)


def _matmul(a, b, tm, tn, out_dtype):
    m, k = a.shape
    n = b.shape[1]
    return pl.pallas_call(
        _mm_kernel,
        out_shape=jax.ShapeDtypeStruct((m, n), out_dtype),
        grid=(m // tm, n // tn),
        in_specs=[pl.BlockSpec((tm, k), lambda i, j: (i, 0)),
                  pl.BlockSpec((k, tn), lambda i, j: (0, j))],
        out_specs=pl.BlockSpec((tm, tn), lambda i, j: (i, j)),
        compiler_params=_cp(("parallel", "arbitrary"), 48),
    )(a, b)


def _rope_block(x, cos, sin, first_half, half):
    partner = jnp.where(first_half, -pltpu.roll(x, LANES - half, 1), pltpu.roll(x, half, 1))
    return x * cos + partner * sin


def _rope_tables(n_lat, ident_rows):
    t = jnp.arange(n_lat)
    row = (t // GRID_W).astype(F32)
    col = (t % GRID_W).astype(F32)

    def angles(rot_dim):
        n_freq = rot_dim // 4
        inv = ROPE_THETA ** (-jnp.arange(n_freq, dtype=F32) / n_freq)
        return jnp.concatenate([row[:, None] * inv, col[:, None] * inv], axis=-1)

    a_m = angles(MLA_ROPE)
    zeros64 = jnp.zeros((n_lat, 64), F32)
    zeros32 = jnp.zeros((n_lat, 32), F32)
    cos_m = jnp.concatenate([zeros64 + 1.0, jnp.cos(a_m), jnp.cos(a_m), zeros32 + 1.0], axis=-1)
    sin_m = jnp.concatenate([zeros64, jnp.sin(a_m), jnp.sin(a_m), zeros32], axis=-1)
    a_d = angles(DIFF_HD)
    cos_d = jnp.tile(jnp.cos(a_d), (1, 4))
    sin_d = jnp.tile(jnp.sin(a_d), (1, 4))
    one = jnp.ones((ident_rows, LANES), F32)
    zero = jnp.zeros((ident_rows, LANES), F32)
    cat = lambda a, b: jnp.concatenate([a, b], axis=0)
    return cat(one, cos_m), cat(zero, sin_m), cat(one, cos_d), cat(zero, sin_d)


def _mla_prep_kernel(qc_ref, kvc_ref, kr_ref, cos_ref, sin_ref, gq_ref, gkv_ref, wuq_ref, wkv_ref,
                     q_ref, ckv_ref, k_ref, v_ref):
    cos = cos_ref[...]
    sin = sin_ref[...]
    lane = lax.broadcasted_iota(jnp.int32, (1, LANES), 1)
    first = lane < MLA_NOPE + MLA_ROPE // 2
    rope = lambda x: _rope_block(x, cos, sin, first, MLA_ROPE // 2)
    qn = _rms(qc_ref[...], gq_ref[...]).astype(BF16)
    q = jnp.dot(qn, wuq_ref[...], preferred_element_type=F32)
    for h in range(MLA_HEADS):
        sl = slice(h * LANES, (h + 1) * LANES)
        q_ref[:, sl] = (rope(q[:, sl]) * MLA_SCALE).astype(BF16)
    ckv = _rms(kvc_ref[...], gkv_ref[...])
    ckv_ref[...] = ckv
    kv = jnp.dot(ckv.astype(BF16), wkv_ref[...], preferred_element_type=F32)
    krr = rope(kr_ref[...])
    for h in range(MLA_HEADS):
        sl = slice(h * LANES, (h + 1) * LANES)
        k_ref[:, sl] = (kv[:, sl] + krr).astype(BF16)
    v_ref[...] = kv[:, MLA_HEADS * LANES:].astype(BF16)


def _mla_prep(z, cos_m, sin_m, gq, gkv, wuq, wkv, rope_row_of):
    t = z.shape[0]
    tm = 512
    kw = MLA_HEADS * LANES
    vw = MLA_HEADS * MLA_V
    return pl.pallas_call(
        _mla_prep_kernel,
        out_shape=(jax.ShapeDtypeStruct((t, kw), BF16), jax.ShapeDtypeStruct((t, MLA_KV_LORA), F32),
                   jax.ShapeDtypeStruct((t, kw), BF16), jax.ShapeDtypeStruct((t, vw), BF16)),
        grid=(t // tm,),
        in_specs=[pl.BlockSpec((tm, MLA_Q_LORA), lambda i: (i, C_QC // MLA_Q_LORA)),
                  pl.BlockSpec((tm, MLA_KV_LORA), lambda i: (i, C_KVC // MLA_KV_LORA)),
                  pl.BlockSpec((tm, LANES), lambda i: (i, C_KR // LANES)),
                  pl.BlockSpec((tm, LANES), lambda i: (rope_row_of(i, tm), 0)),
                  pl.BlockSpec((tm, LANES), lambda i: (rope_row_of(i, tm), 0)),
                  pl.BlockSpec((1, MLA_Q_LORA), lambda i: (0, 0)),
                  pl.BlockSpec((1, MLA_KV_LORA), lambda i: (0, 0)),
                  pl.BlockSpec((MLA_Q_LORA, kw), lambda i: (0, 0)),
                  pl.BlockSpec((MLA_KV_LORA, kw + vw), lambda i: (0, 0))],
        out_specs=(pl.BlockSpec((tm, kw), lambda i: (i, 0)),
                   pl.BlockSpec((tm, MLA_KV_LORA), lambda i: (i, 0)),
                   pl.BlockSpec((tm, kw), lambda i: (i, 0)),
                   pl.BlockSpec((tm, vw), lambda i: (i, 0))),
        compiler_params=_cp(("parallel",), 40),
    )(z, z, z, cos_m, sin_m, gq, gkv, wuq, wkv)


def _mla_cache_kernel(ckv_ref, kr_ref, wkv_ref, k_ref, v_ref):
    kv = jnp.dot(ckv_ref[...].astype(BF16), wkv_ref[...], preferred_element_type=F32)
    kr = kr_ref[...]
    for h in range(MLA_HEADS):
        sl = slice(h * LANES, (h + 1) * LANES)
        k_ref[:, sl] = (kv[:, sl] + kr).astype(BF16)
    v_ref[...] = kv[:, MLA_HEADS * LANES:].astype(BF16)


def _mla_cache(cache_ckv, cache_kr_pad, wkv, layer):
    db, _, past, _ = cache_ckv.shape
    kw = MLA_HEADS * LANES
    vw = MLA_HEADS * MLA_V
    return pl.pallas_call(
        _mla_cache_kernel,
        out_shape=(jax.ShapeDtypeStruct((db * past, kw), BF16), jax.ShapeDtypeStruct((db * past, vw), BF16)),
        grid=(db,),
        in_specs=[pl.BlockSpec((None, None, past, MLA_KV_LORA), lambda b: (b, layer, 0, 0)),
                  pl.BlockSpec((None, None, past, LANES), lambda b: (b, layer, 0, 0)),
                  pl.BlockSpec((MLA_KV_LORA, kw + vw), lambda b: (0, 0))],
        out_specs=(pl.BlockSpec((past, kw), lambda b: (b, 0)), pl.BlockSpec((past, vw), lambda b: (b, 0))),
        compiler_params=_cp(("parallel",), 40),
    )(cache_ckv, cache_kr_pad, wkv)


def _mla_attn_kernel(*refs, n_src):
    q_ref = refs[0]
    k_refs = refs[1:1 + n_src]
    v_refs = refs[1 + n_src:1 + 2 * n_src]
    o_ref = refs[1 + 2 * n_src]
    vs = [v_ref[...] for v_ref in v_refs]
    outs = []
    for hh in range(2):
        sl = slice(hh * LANES, (hh + 1) * LANES)
        q = q_ref[:, sl]
        ps, l = _softmax_parts([_scores(q, k_ref[:, sl]) for k_ref in k_refs])
        outs.append(_pv(ps, vs) / l)
    o_ref[...] = jnp.where(_lane_lo(), outs[0], outs[1]).astype(o_ref.dtype)


def _mla_attn_ctx(q, k, v, out, n_seq, seq):
    pairs = MLA_HEADS // 2
    return pl.pallas_call(
        functools.partial(_mla_attn_kernel, n_src=1),
        out_shape=out,
        grid=(n_seq, pairs),
        in_specs=[pl.BlockSpec((seq, 2 * LANES), lambda s, p: (s, p)),
                  pl.BlockSpec((seq, 2 * LANES), lambda s, p: (s, p)),
                  pl.BlockSpec((seq, LANES), lambda s, p: (s, p))],
        out_specs=pl.BlockSpec((seq, LANES), lambda s, p: (s, p)),
        compiler_params=_cp(("parallel", "parallel"), 40),
    )(q, k, v)


def _mla_attn_lat(q, k, v, kc, vc, prev, t_ctx, db, ds, past, tq):
    pairs = MLA_HEADS // 2
    nq = ds // tq
    q0 = t_ctx // tq
    s0 = t_ctx // ds
    return pl.pallas_call(
        functools.partial(_mla_attn_kernel_aliased, n_src=2),
        out_shape=jax.ShapeDtypeStruct(prev.shape, prev.dtype),
        grid=(db, pairs, nq),
        in_specs=[pl.BlockSpec((tq, 2 * LANES), lambda b, p, i: (q0 + b * nq + i, p)),
                  pl.BlockSpec((ds, 2 * LANES), lambda b, p, i: (s0 + b, p)),
                  pl.BlockSpec((past, 2 * LANES), lambda b, p, i: (b, p)),
                  pl.BlockSpec((ds, LANES), lambda b, p, i: (s0 + b, p)),
                  pl.BlockSpec((past, LANES), lambda b, p, i: (b, p)),
                  pl.BlockSpec(memory_space=pl.ANY)],
        out_specs=pl.BlockSpec((tq, LANES), lambda b, p, i: (q0 + b * nq + i, p)),
        input_output_aliases={5: 0},
        compiler_params=_cp(("parallel", "parallel", "arbitrary"), 56),
    )(q, k, kc, v, vc, prev)


def _mla_attn_kernel_aliased(*refs, n_src):
    _mla_attn_kernel(*refs[:1 + 2 * n_src], refs[-1], n_src=n_src)


def _attn64_ctx_kernel(q_ref, k_ref, v_ref, o_ref):
    lo = _lane_lo()
    q = q_ref[...] * HEAD_SCALE_64
    k = k_ref[...].astype(BF16)
    v = v_ref[...].astype(BF16)
    outs = []
    for hh in range(2):
        qm = jnp.where(lo if hh == 0 else jnp.logical_not(lo), q, 0.0).astype(BF16)
        ps, l = _softmax_parts([_scores(qm, k)])
        outs.append(_pv(ps, [v]) / l)
    o_ref[...] = jnp.where(lo, outs[0], outs[1]).astype(o_ref.dtype)


def _na_attn_ctx(z, out, n_seq, seq):
    pairs = NA_HEADS // 2
    return pl.pallas_call(
        _attn64_ctx_kernel,
        out_shape=out,
        grid=(n_seq, pairs),
        in_specs=[pl.BlockSpec((seq, LANES), lambda s, p: (s, C_NQ // LANES + p)),
                  pl.BlockSpec((seq, LANES), lambda s, p: (s, C_NK // LANES + p)),
                  pl.BlockSpec((seq, LANES), lambda s, p: (s, C_NV // LANES + p))],
        out_specs=pl.BlockSpec((seq, LANES), lambda s, p: (s, p)),
        compiler_params=_cp(("parallel", "parallel"), 40),
    )(z, z, z)


def _na_plan(rows):
    krows = min(NA_KROWS, rows)
    wr = min(NA_WIN_R, rows)
    nblk = rows // NA_QROWS
    kbase = np.zeros((nblk,), np.int32)
    drmaps = np.zeros((nblk, NA_QROWS, krows), np.int32)
    invalid = 2 * NA_WIN_R - 1
    for blk in range(nblk):
        r0 = blk * NA_QROWS
        kb = int(np.clip(r0 - wr // 2, 0, rows - krows))
        kbase[blk] = kb
        for rr in range(NA_QROWS):
            r = r0 + rr
            w0 = int(np.clip(r - wr // 2, 0, rows - wr))
            for kk in range(krows):
                krow = kb + kk
                drmaps[blk, rr, kk] = (krow - r + NA_WIN_R - 1) if (w0 <= krow < w0 + wr) else invalid
    pats, pat_of = np.unique(drmaps, axis=0, return_inverse=True)
    return kbase, pats, np.asarray(pat_of, np.int32).reshape(nblk), krows


def _na_bias_table(rpb, pats):
    h = rpb.shape[0]
    c = np.arange(GRID_W)[:, None]
    kc = np.arange(GRID_W)[None, :]
    cs = np.clip(c - NA_WIN_C // 2, 0, GRID_W - NA_WIN_C)
    valid = (kc >= cs) & (kc < cs + NA_WIN_C)
    idx = np.clip(kc - c + NA_WIN_C - 1, 0, 2 * NA_WIN_C - 2)
    e = jnp.where(valid[None, None], rpb[:, :, idx].astype(F32), NEG_INF)
    e = jnp.concatenate([e, jnp.full((h, 1, GRID_W, GRID_W), NEG_INF, F32)], axis=1)
    b = e[:, pats]
    p, qr, kr = pats.shape
    return b.transpose(1, 0, 2, 4, 3, 5).reshape(p, h, qr * GRID_W, kr * GRID_W)


def _na_lat_kernel(pat_ref, kb_ref, q_ref, k_ref, v_ref, kc_ref, vc_ref, bias_ref, prev_ref, o_ref, *, nk):
    del pat_ref, prev_ref
    blk = pl.program_id(2)
    kstart = pl.multiple_of(kb_ref[blk] * GRID_W, GRID_W)
    lo = _lane_lo()
    q = q_ref[...] * HEAD_SCALE_64
    kw = k_ref[pl.ds(kstart, nk), :].astype(BF16)
    vw = v_ref[pl.ds(kstart, nk), :].astype(BF16)
    kc = kc_ref[...].astype(BF16)
    vc = vc_ref[...].astype(BF16)
    outs = []
    for hh in range(2):
        qm = jnp.where(lo if hh == 0 else jnp.logical_not(lo), q, 0.0).astype(BF16)
        ps, l = _softmax_parts([_scores(qm, kw) + bias_ref[hh], _scores(qm, kc)])
        outs.append(_pv(ps, [vw, vc]) / l)
    o_ref[...] = jnp.where(lo, outs[0], outs[1]).astype(o_ref.dtype)


def _na_attn_lat(z, cache_k, cache_v, bias, plan, prev, layer, t_ctx, db, ds, past):
    kbase, _, pat_of, krows = plan
    pairs = NA_HEADS // 2
    tq = NA_QROWS * GRID_W
    nk = krows * GRID_W
    nblk = ds // tq
    q0 = t_ctx // tq
    s0 = t_ctx // ds
    grid_spec = pltpu.PrefetchScalarGridSpec(
        num_scalar_prefetch=2,
        grid=(db, pairs, nblk),
        in_specs=[pl.BlockSpec((tq, LANES), lambda b, p, i, pat, kb: (q0 + b * nblk + i, C_NQ // LANES + p)),
                  pl.BlockSpec((ds, LANES), lambda b, p, i, pat, kb: (s0 + b, C_NK // LANES + p)),
                  pl.BlockSpec((ds, LANES), lambda b, p, i, pat, kb: (s0 + b, C_NV // LANES + p)),
                  pl.BlockSpec((None, None, past, LANES), lambda b, p, i, pat, kb: (b, layer, 0, p)),
                  pl.BlockSpec((None, None, past, LANES), lambda b, p, i, pat, kb: (b, layer, 0, p)),
                  pl.BlockSpec((None, 2, tq, nk), lambda b, p, i, pat, kb: (pat[i], p, 0, 0)),
                  pl.BlockSpec(memory_space=pl.ANY)],
        out_specs=pl.BlockSpec((tq, LANES), lambda b, p, i, pat, kb: (q0 + b * nblk + i, p)),
    )
    return pl.pallas_call(
        functools.partial(_na_lat_kernel, nk=nk),
        out_shape=jax.ShapeDtypeStruct(prev.shape, prev.dtype),
        grid_spec=grid_spec,
        input_output_aliases={8: 0},
        compiler_params=_cp(("parallel", "parallel", "arbitrary"), 56),
    )(jnp.asarray(pat_of), jnp.asarray(kbase), z, z, z, cache_k, cache_v, bias, prev)


def _pool_kernel(*refs, t, aliased):
    u_ref, w_ref, sc_ref = refs[:3]
    o_ref, pad_ref = refs[-2:]
    halo = 8
    pos = lax.broadcasted_iota(jnp.int32, (t, POOL_GROUP), 0)
    zeros = jnp.zeros((halo, POOL_GROUP), F32)
    for gi, w in enumerate(POOL_WINDOWS):
        sl = slice(gi * POOL_GROUP, (gi + 1) * POOL_GROUP)
        u = u_ref[:, sl]
        pad_ref[0:halo, :] = zeros
        pad_ref[halo + t:2 * halo + t, :] = zeros
        pad_ref[halo:halo + t, :] = u
        tot = None
        for d in range(-(w // 2), w // 2):
            part = pad_ref[halo + d:halo + d + t, :]
            tot = part if tot is None else tot + part
        cnt = (jnp.minimum(pos + w // 2, t) - jnp.maximum(pos - w // 2, 0)).astype(F32)
        pooled = (tot / cnt - u).astype(BF16)
        mixed = jnp.dot(pooled, w_ref[gi], preferred_element_type=F32)
        o_ref[:, sl] = (mixed * sc_ref[:, sl]).astype(o_ref.dtype)


def _pool(z, pool_w, pool_scale, out, prev, n_seq, seq, row0):
    width = len(POOL_WINDOWS) * POOL_GROUP
    aliased = prev is not None
    in_specs = [pl.BlockSpec((seq, width), lambda s: (row0 + s, C_POOL // width)),
                pl.BlockSpec((len(POOL_WINDOWS), POOL_GROUP, POOL_GROUP), lambda s: (0, 0, 0)),
                pl.BlockSpec((1, width), lambda s: (0, 0))]
    args = [z, pool_w, pool_scale]
    if aliased:
        in_specs.append(pl.BlockSpec(memory_space=pl.ANY))
        args.append(prev)
        out = jax.ShapeDtypeStruct(prev.shape, prev.dtype)
    return pl.pallas_call(
        functools.partial(_pool_kernel, t=seq, aliased=aliased),
        out_shape=out,
        grid=(n_seq,),
        in_specs=in_specs,
        out_specs=pl.BlockSpec((seq, width), lambda s: (row0 + s, 0)),
        scratch_shapes=[pltpu.VMEM((seq + 16, POOL_GROUP), F32)],
        input_output_aliases={3: 0} if aliased else {},
        compiler_params=_cp(("parallel",), 56),
    )(*args)


def _diff_prep_kernel(q_ref, k_ref, v_ref, cos_ref, sin_ref, qo_ref, ko_ref, vo_ref):
    cos = cos_ref[...]
    sin = sin_ref[...]
    lane = lax.broadcasted_iota(jnp.int32, (1, LANES), 1)
    first = (lane % DIFF_HD) < DIFF_HD // 2
    for h in range(DIFF_HEADS):
        sl = slice(h * LANES, (h + 1) * LANES)
        qo_ref[:, sl] = (_rope_block(q_ref[:, sl], cos, sin, first, DIFF_HD // 2) * HEAD_SCALE_64).astype(BF16)
        ko_ref[:, sl] = _rope_block(k_ref[:, sl], cos, sin, first, DIFF_HD // 2).astype(BF16)
    vo_ref[...] = v_ref[...].astype(BF16)


def _diff_prep(z, cos_d, sin_d, rope_row_of):
    t = z.shape[0]
    tm = 512
    w = DIFF_HEADS * LANES
    sds = jax.ShapeDtypeStruct((t, w), BF16)
    return pl.pallas_call(
        _diff_prep_kernel,
        out_shape=(sds, sds, sds),
        grid=(t // tm,),
        in_specs=[pl.BlockSpec((tm, w), lambda i: (i, C_DQ // w)),
                  pl.BlockSpec((tm, w), lambda i: (i, C_DK // w)),
                  pl.BlockSpec((tm, w), lambda i: (i, C_DV // w)),
                  pl.BlockSpec((tm, LANES), lambda i: (rope_row_of(i, tm), 0)),
                  pl.BlockSpec((tm, LANES), lambda i: (rope_row_of(i, tm), 0))],
        out_specs=(pl.BlockSpec((tm, w), lambda i: (i, 0)),) * 3,
        compiler_params=_cp(("parallel",), 40),
    )(z, z, z, cos_d, sin_d)


def _diff_attn_kernel(*refs, n_src, aliased):
    linit_ref, lam_ref, g_ref, q_ref = refs[:4]
    k_refs = refs[4:4 + n_src]
    v_refs = refs[4 + n_src:4 + 2 * n_src]
    o_ref = refs[-1]
    lam_init = linit_ref[0]
    lp = lam_ref[...]
    lam = (jnp.exp(jnp.sum(lp[0:1] * lp[1:2], axis=-1, keepdims=True))
           - jnp.exp(jnp.sum(lp[2:3] * lp[3:4], axis=-1, keepdims=True)) + lam_init)
    lo = _lane_lo()
    q = q_ref[...]
    ks = [k_ref[...].astype(BF16) for k_ref in k_refs]
    vs = [v_ref[...].astype(BF16) for v_ref in v_refs]
    zero = jnp.zeros_like(q)
    p1, l1 = _softmax_parts([_scores(jnp.where(lo, q, zero), k) for k in ks])
    p2, l2 = _softmax_parts([_scores(jnp.where(lo, zero, q), k) for k in ks])
    r1 = 1.0 / l1
    r2 = lam / l2
    o = _pv([a * r1 - b * r2 for a, b in zip(p1, p2)], vs)
    o_ref[...] = (_rms(o, g_ref[...]) * (1.0 - lam_init)).astype(o_ref.dtype)


def _diff_attn_ctx(linit, lam_p, g, q, k, v, out, layer, n_seq, seq):
    smem = pl.BlockSpec(memory_space=pltpu.SMEM)
    return pl.pallas_call(
        functools.partial(_diff_attn_kernel, n_src=1, aliased=False),
        out_shape=out,
        grid=(n_seq, DIFF_HEADS),
        in_specs=[smem,
                  pl.BlockSpec((None, 4, DIFF_HD), lambda s, h: (layer, 0, 0)),
                  pl.BlockSpec((1, LANES), lambda s, h: (0, 0)),
                  pl.BlockSpec((seq, LANES), lambda s, h: (s, h)),
                  pl.BlockSpec((seq, LANES), lambda s, h: (s, h)),
                  pl.BlockSpec((seq, LANES), lambda s, h: (s, h))],
        out_specs=pl.BlockSpec((seq, LANES), lambda s, h: (s, h)),
        compiler_params=_cp(("parallel", "parallel"), 40),
    )(linit, lam_p, g, q, k, v)


def _diff_attn_lat(linit, lam_p, g, q, k, v, cache_k, cache_v, prev, layer, t_ctx, db, ds, past, tq):
    smem = pl.BlockSpec(memory_space=pltpu.SMEM)
    nq = ds // tq
    q0 = t_ctx // tq
    s0 = t_ctx // ds
    return pl.pallas_call(
        functools.partial(_diff_attn_kernel_aliased, n_src=2),
        out_shape=jax.ShapeDtypeStruct(prev.shape, prev.dtype),
        grid=(db, DIFF_HEADS, nq),
        in_specs=[smem,
                  pl.BlockSpec((None, 4, DIFF_HD), lambda b, h, i: (layer, 0, 0)),
                  pl.BlockSpec((1, LANES), lambda b, h, i: (0, 0)),
                  pl.BlockSpec((tq, LANES), lambda b, h, i: (q0 + b * nq + i, h)),
                  pl.BlockSpec((ds, LANES), lambda b, h, i: (s0 + b, h)),
                  pl.BlockSpec((None, None, past, LANES), lambda b, h, i: (b, layer, 0, h)),
                  pl.BlockSpec((ds, LANES), lambda b, h, i: (s0 + b, h)),
                  pl.BlockSpec((None, None, past, LANES), lambda b, h, i: (b, layer, 0, h)),
                  pl.BlockSpec(memory_space=pl.ANY)],
        out_specs=pl.BlockSpec((tq, LANES), lambda b, h, i: (q0 + b * nq + i, h)),
        input_output_aliases={8: 0},
        compiler_params=_cp(("parallel", "parallel", "arbitrary"), 56),
    )(linit, lam_p, g, q, k, cache_k, v, cache_v, prev)


def _diff_attn_kernel_aliased(*refs, n_src):
    _diff_attn_kernel(*refs[:4 + 2 * n_src], refs[-1], n_src=n_src, aliased=True)


def _branch_kernel(oa_ref, ob_ref, oc_ref, od_ref, g0_ref, g1_ref, g2_ref, g3_ref, w_ref, out_ref):
    acc = None
    for bi, (o_ref, g_ref) in enumerate(((oa_ref, g0_ref), (ob_ref, g1_ref), (oc_ref, g2_ref), (od_ref, g3_ref))):
        term = jax.nn.sigmoid(g_ref[...]) * jnp.dot(o_ref[...], w_ref[bi], preferred_element_type=F32)
        acc = term if acc is None else acc + term
    out_ref[...] = acc.astype(out_ref.dtype)


def _branch_merge(outs, z, w_br, d):
    t, bw = outs[0].shape
    tm, tn = 512, 1024
    nb = d // tn
    o_spec = pl.BlockSpec((tm, bw), lambda i, j: (i, 0))
    g_specs = [pl.BlockSpec((tm, tn), functools.partial(lambda i, j, bi: (i, C_GATES // tn + bi * nb + j), bi=bi))
               for bi in range(4)]
    return pl.pallas_call(
        _branch_kernel,
        out_shape=jax.ShapeDtypeStruct((t, d), BF16),
        grid=(t // tm, nb),
        in_specs=[o_spec] * 4 + g_specs + [pl.BlockSpec((4, bw, tn), lambda i, j: (0, 0, j))],
        out_specs=pl.BlockSpec((tm, tn), lambda i, j: (i, j)),
        compiler_params=_cp(("parallel", "arbitrary"), 56),
    )(*outs, z, z, z, z, w_br)


def _wo_kernel(m_ref, wo_ref, x_ref, mod_ref, gn_ref, x1_ref, h2_ref, *, d):
    y = jnp.dot(m_ref[...], wo_ref[...], preferred_element_type=F32)
    mod = mod_ref[0]
    g1 = mod[:, 2 * d:3 * d]
    sh2 = mod[:, 3 * d:4 * d]
    sc2 = mod[:, 4 * d:5 * d]
    x1 = x_ref[...] + g1 * _rms(y, gn_ref[1:2, :])
    x1_ref[...] = x1
    h2_ref[...] = (_rms(x1, gn_ref[2:3, :]) * (1.0 + sc2) + sh2).astype(BF16)


def _wo_residual(merged, w_o, x, mod_l, gn, row_of):
    t, d = x.shape
    tm = 256
    return pl.pallas_call(
        functools.partial(_wo_kernel, d=d),
        out_shape=(jax.ShapeDtypeStruct((t, d), F32), jax.ShapeDtypeStruct((t, d), BF16)),
        grid=(t // tm,),
        in_specs=[pl.BlockSpec((tm, d), lambda i: (i, 0)),
                  pl.BlockSpec((d, d), lambda i: (0, 0)),
                  pl.BlockSpec((tm, d), lambda i: (i, 0)),
                  pl.BlockSpec((1, 1, 6 * d), lambda i: (row_of(i, tm), 0, 0)),
                  pl.BlockSpec((4, d), lambda i: (0, 0))],
        out_specs=(pl.BlockSpec((tm, d), lambda i: (i, 0)), pl.BlockSpec((tm, d), lambda i: (i, 0))),
        compiler_params=_cp(("parallel",), 56),
    )(merged, w_o, x, mod_l, gn)


def _ffn_kernel(h_ref, wu_ref, wd_ref, x_ref, mod_ref, gn_ref, o_ref, acc_ref, *, d):
    f = pl.program_id(1)
    u = jnp.dot(h_ref[...], wu_ref[...], preferred_element_type=F32)
    u = jnp.square(jnp.maximum(u, 0.0)).astype(BF16)
    part = jnp.dot(u, wd_ref[...], preferred_element_type=F32)

    @pl.when(f == 0)
    def _():
        acc_ref[...] = part

    @pl.when(f > 0)
    def _():
        acc_ref[...] += part

    @pl.when(f == pl.num_programs(1) - 1)
    def _():
        g2 = mod_ref[0][:, 5 * d:6 * d]
        o_ref[...] = x_ref[...] + g2 * _rms(acc_ref[...], gn_ref[3:4, :])


def _ffn(h2, w_up, w_down, x1, mod_l, gn, row_of):
    t, d = x1.shape
    dff = w_up.shape[1]
    tm, tf = 512, 512
    return pl.pallas_call(
        functools.partial(_ffn_kernel, d=d),
        out_shape=jax.ShapeDtypeStruct((t, d), F32),
        grid=(t // tm, dff // tf),
        in_specs=[pl.BlockSpec((tm, d), lambda i, f: (i, 0)),
                  pl.BlockSpec((d, tf), lambda i, f: (0, f)),
                  pl.BlockSpec((tf, d), lambda i, f: (f, 0)),
                  pl.BlockSpec((tm, d), lambda i, f: (i, 0)),
                  pl.BlockSpec((1, 1, 6 * d), lambda i, f: (row_of(i, tm), 0, 0)),
                  pl.BlockSpec((4, d), lambda i, f: (0, 0))],
        out_specs=pl.BlockSpec((tm, d), lambda i, f: (i, 0)),
        scratch_shapes=[pltpu.VMEM((tm, d), F32)],
        compiler_params=_cp(("parallel", "arbitrary"), 56),
    )(h2, w_up, w_down, x1, mod_l, gn)


def _prep_weights(w_in, w_uq, w_ukv):
    depth, d, _ = w_in.shape
    sizes = (MLA_Q_LORA, MLA_KV_LORA, MLA_ROPE, 512, 512, 512, 512, 512, 512, 512, 4 * d)
    offs = np.cumsum((0,) + sizes)
    part = lambda i: w_in[:, :, offs[i]:offs[i + 1]]
    q_c, kv_c, k_r, na_q, na_k, na_v, pool, dq, dk, dv, gates = (part(i) for i in range(11))
    kr_blk = jnp.pad(k_r, ((0, 0), (0, 0), (MLA_NOPE, LANES - MLA_NOPE - MLA_ROPE)))
    w_all = jnp.concatenate([gates, q_c, dq, dk, dv, na_q, na_k, na_v, pool, kv_c, kr_blk], axis=-1).astype(BF16)
    hd = MLA_NOPE + MLA_ROPE
    wuq = jnp.pad(w_uq.reshape(depth, MLA_Q_LORA, MLA_HEADS, hd), ((0, 0), (0, 0), (0, 0), (0, LANES - hd)))
    wuq = wuq.reshape(depth, MLA_Q_LORA, MLA_HEADS * LANES).astype(BF16)
    wkv4 = w_ukv.reshape(depth, MLA_KV_LORA, MLA_HEADS, MLA_NOPE + MLA_V)
    wk = jnp.pad(wkv4[..., :MLA_NOPE], ((0, 0), (0, 0), (0, 0), (0, LANES - MLA_NOPE)))
    wk = wk.reshape(depth, MLA_KV_LORA, MLA_HEADS * LANES)
    wv = wkv4[..., MLA_NOPE:].reshape(depth, MLA_KV_LORA, MLA_HEADS * MLA_V)
    wkv = jnp.concatenate([wk, wv], axis=-1).astype(BF16)
    return w_all, wuq, wkv


def kernel(x_prompt, x_sample, cache_mla_ckv, cache_mla_krope, cache_na_k, cache_na_v, cache_diff_k, cache_diff_v, c, c_ctx, w_mod, b_mod, g_norm, w_in, g_q_lora, g_kv_lora, w_uq, w_ukv, na_rpb, pool_w, pool_scale, diff_lambda, diff_norm_g, w_br, w_o, w_up, w_down):
    nb, seq, d = x_prompt.shape
    db, ds, _ = x_sample.shape
    depth = w_in.shape[0]
    past = cache_mla_ckv.shape[2]
    t_ctx = nb * seq
    t_lat = db * ds
    t = t_ctx + t_lat
    tq_lat = 256
    assert t_ctx % ds == 0 and ds % 512 == 0 and t_ctx % 1024 == 0 and t % 1024 == 0 and seq % 8 == 0

    def row_of(i, tm):
        n_ctx = t_ctx // tm
        return jnp.where(i < n_ctx, 0, 1 + (i - n_ctx) // (ds // tm))

    def rope_row_of(i, tm):
        n_ctx = t_ctx // tm
        return jnp.where(i < n_ctx, 0, 1 + (i - n_ctx) % (ds // tm))

    w_all, wuq, wkv = _prep_weights(w_in, w_uq, w_ukv)
    w_br_b = w_br.astype(BF16)
    w_o_b = w_o.astype(BF16)
    w_up_b = w_up.astype(BF16)
    w_down_b = w_down.astype(BF16)
    pool_w_b = pool_w.astype(BF16)
    cos_m, sin_m, cos_d, sin_d = _rope_tables(ds, 512)
    rows = ds // GRID_W
    na_plan = _na_plan(rows)
    cache_kr_pad = jnp.pad(cache_mla_krope, ((0, 0), (0, 0), (0, 0), (MLA_NOPE, LANES - MLA_NOPE - MLA_ROPE)))
    cache_nk = cache_na_k.reshape(db, depth, past, NA_HEADS * NA_HD)
    cache_nv = cache_na_v.reshape(db, depth, past, NA_HEADS * NA_HD)
    cache_dk = cache_diff_k.reshape(db, depth, past, DIFF_HEADS * 2 * DIFF_HD)
    cache_dv = cache_diff_v.reshape(db, depth, past, DIFF_HEADS * 2 * DIFF_HD)

    n_rows = 1 + db
    r_pad = -(-n_rows // 8) * 8
    cond = jnp.concatenate([c_ctx[None, :], c, jnp.zeros((r_pad - n_rows, d), F32)], axis=0)
    mod = _modulation(cond, w_mod, b_mod)

    x = jnp.concatenate([x_prompt.reshape(t_ctx, d), x_sample.reshape(t_lat, d)], axis=0)
    branch_sds = jax.ShapeDtypeStruct((t, 512), BF16)
    states = []
    for li in range(depth):
        mod_l = mod[li].reshape(r_pad, 1, 6 * d)
        gn = g_norm[li]
        h = _norm_mod(x, mod_l, gn[0:1], row_of)
        z = _matmul(h, w_all[li], 1024, 1152, F32)

        q_a, ckv, k_a, v_a = _mla_prep(z, cos_m, sin_m, g_q_lora[li][None, :], g_kv_lora[li][None, :],
                                       wuq[li], wkv[li], rope_row_of)
        kc_a, vc_a = _mla_cache(cache_mla_ckv, cache_kr_pad, wkv[li], li)
        o_a = _mla_attn_ctx(q_a, k_a, v_a, branch_sds, nb, seq)
        o_a = _mla_attn_lat(q_a, k_a, v_a, kc_a, vc_a, o_a, t_ctx, db, ds, past, tq_lat)

        bias = _na_bias_table(na_rpb[li], na_plan[1])
        o_b = _na_attn_ctx(z, branch_sds, nb, seq)
        o_b = _na_attn_lat(z, cache_nk, cache_nv, bias, na_plan, o_b, li, t_ctx, db, ds, past)

        o_c = _pool(z, pool_w_b[li], pool_scale[li][None, :], branch_sds, None, nb, seq, 0)
        o_c = _pool(z, pool_w_b[li], pool_scale[li][None, :], None, o_c, db, ds, t_ctx // ds)

        q_d, k_d, v_d = _diff_prep(z, cos_d, sin_d, rope_row_of)
        linit = jnp.full((1,), 0.8 - 0.6 * math.exp(-0.3 * li), F32)
        g_d = diff_norm_g[li][None, :]
        o_d = _diff_attn_ctx(linit, diff_lambda, g_d, q_d, k_d, v_d, branch_sds, li, nb, seq)
        o_d = _diff_attn_lat(linit, diff_lambda, g_d, q_d, k_d, v_d, cache_dk, cache_dv, o_d,
                             li, t_ctx, db, ds, past, tq_lat)

        merged = _branch_merge((o_a, o_b, o_c, o_d), z, w_br_b[li], d)
        x1, h2 = _wo_residual(merged, w_o_b[li], x, mod_l, gn, row_of)
        x = _ffn(h2, w_up_b[li], w_down_b[li], x1, mod_l, gn, row_of)

        zc = z[:t_ctx]
        states.append((ckv[:t_ctx].reshape(nb, seq, MLA_KV_LORA),
                       zc[:, C_KR + MLA_NOPE:C_KR + MLA_NOPE + MLA_ROPE].reshape(nb, seq, MLA_ROPE),
                       zc[:, C_NK:C_NK + 512].reshape(nb, seq, NA_HEADS, NA_HD),
                       zc[:, C_NV:C_NV + 512].reshape(nb, seq, NA_HEADS, NA_HD),
                       zc[:, C_DK:C_DK + 512].reshape(nb, seq, DIFF_HEADS, 2 * DIFF_HD),
                       zc[:, C_DV:C_DV + 512].reshape(nb, seq, DIFF_HEADS, 2 * DIFF_HD)))

    y_prompt = x[:t_ctx].reshape(nb, seq, d)
    y_sample = x[t_ctx:].reshape(db, ds, d)
    st = [jnp.stack([s[k] for s in states], axis=1) for k in range(6)]
    return (y_prompt, y_sample, *st)
```

```python
import functools
import math

import numpy as np
import jax
import jax.numpy as jnp
from jax import lax
from jax.experimental import pallas as pl
from jax.experimental.pallas import tpu as pltpu

F32 = jnp.float32
BF16 = jnp.bfloat16

GRID_W = 64
MLA_HEADS = 8
MLA_NOPE = 64
MLA_ROPE = 32
MLA_V = 64
MLA_Q_LORA = 512
MLA_KV_LORA = 256
NA_HEADS = 8
NA_HD = 64
NA_WIN_R = 8
NA_WIN_C = 16
POOL_WINDOWS = (2, 4, 8, 16)
POOL_GROUP = 128
DIFF_HEADS = 4
DIFF_HD = 64
ROPE_THETA = 10000.0
RMS_EPS = 1e-6
NEG_INF = -1e30
LOG2E = math.log2(math.e)
MLA_QSCALE = (MLA_NOPE + MLA_ROPE) ** -0.5 * LOG2E
HEAD64_QSCALE = 64 ** -0.5 * LOG2E
LANES = 128
MIB = 1024 * 1024

C_GATES = 0
C_QC = 8192
C_DQ = 8704
C_DK = 9216
C_DV = 9728
C_NQ = 10240
C_NK = 10752
C_NV = 11264
C_POOL = 11776
C_KVC = 12288
C_KR = 12544
Z_COLS = 12672

NA_QROWS = 8
NA_KROWS = 16
ATT_TQ = 256
ATT_CK = 512
CTX_GROUP = 4


def _cp(sem, vmem_mib):
    return pltpu.CompilerParams(dimension_semantics=sem, vmem_limit_bytes=vmem_mib * MIB)


def _rms(x, g):
    return x * lax.rsqrt(jnp.mean(x * x, axis=-1, keepdims=True) + RMS_EPS) * g


def _scores(q, k):
    return lax.dot_general(q, k, (((1,), (1,)), ((), ())), preferred_element_type=F32)


def _softmax2_parts(ss):
    m = functools.reduce(jnp.maximum, [jnp.max(s, axis=-1, keepdims=True) for s in ss])
    ps = [jnp.exp2(s - m) for s in ss]
    l = functools.reduce(jnp.add, [jnp.sum(p, axis=-1, keepdims=True) for p in ps])
    return ps, l


def _pv(ps, vs):
    acc = None
    for p, v in zip(ps, vs):
        t = jnp.dot(p.astype(BF16), v, preferred_element_type=F32)
        acc = t if acc is None else acc + t
    return acc


def _lane_lo():
    return lax.broadcasted_iota(jnp.int32, (1, LANES), 1) < 64


def _tile_fold(op, run, x):
    for c in range(x.shape[1] // LANES):
        run = op(run, x[:, c * LANES:(c + 1) * LANES])
    return run


def _chunks(sizes, ck):
    out, off = [], 0
    for si, n in enumerate(sizes):
        step = min(ck, n)
        for st in range(0, n, step):
            out.append((si, st, step, off))
            off += step
    return out


def _mod_kernel(c_ref, w_ref, b_ref, o_ref):
    c = c_ref[...]
    s = c * jax.nn.sigmoid(c)
    o_ref[0] = jnp.dot(s.astype(BF16), w_ref[0].astype(BF16), preferred_element_type=F32) + b_ref[0]


def _modulation(cond, w_mod, b_mod):
    depth, d, n = w_mod.shape
    r = cond.shape[0]
    tn = 1024
    return pl.pallas_call(
        _mod_kernel,
        out_shape=jax.ShapeDtypeStruct((depth, r, n), F32),
        grid=(depth, n // tn),
        in_specs=[pl.BlockSpec((r, d), lambda l, j: (0, 0)),
                  pl.BlockSpec((1, d, tn), lambda l, j: (l, 0, j)),
                  pl.BlockSpec((1, 1, tn), lambda l, j: (l, 0, j))],
        out_specs=pl.BlockSpec((1, r, tn), lambda l, j: (l, 0, j)),
        compiler_params=_cp(("parallel", "parallel"), 40),
        name="modulation",
    )(cond, w_mod, b_mod.reshape(depth, 1, n))


def _norm_mod_kernel(x_ref, mod_ref, g_ref, h_ref, *, d):
    mod = mod_ref[0]
    sh = mod[:, 0:d]
    sc = mod[:, d:2 * d]
    h_ref[...] = (_rms(x_ref[...], g_ref[...]) * (1.0 + sc) + sh).astype(BF16)


def _norm_mod(x, mod_l, g, row_of):
    t, d = x.shape
    tm = 512
    return pl.pallas_call(
        functools.partial(_norm_mod_kernel, d=d),
        out_shape=jax.ShapeDtypeStruct((t, d), BF16),
        grid=(t // tm,),
        in_specs=[pl.BlockSpec((tm, d), lambda i: (i, 0)),
                  pl.BlockSpec((1, 1, 6 * d), lambda i: (row_of(i, tm), 0, 0)),
                  pl.BlockSpec((1, d), lambda i: (0, 0))],
        out_specs=pl.BlockSpec((tm, d), lambda i: (i, 0)),
        compiler_params=_cp(("parallel",), 40),
        name="norm_mod",
    )(x, mod_l, g)


def _mm_kernel(a_ref, b_ref, o_ref):
    o_ref[...] = jnp.dot(a_ref[...], b_ref[...], preferred_element_type=F32).astype(o_ref.dtype)


def _matmul(a, b, tm, tn, out_dtype, name):
    m, k = a.shape
    n = b.shape[1]
    return pl.pallas_call(
        _mm_kernel,
        out_shape=jax.ShapeDtypeStruct((m, n), out_dtype),
        grid=(m // tm, n // tn),
        in_specs=[pl.BlockSpec((tm, k), lambda i, j: (i, 0)),
                  pl.BlockSpec((k, tn), lambda i, j: (0, j))],
        out_specs=pl.BlockSpec((tm, tn), lambda i, j: (i, j)),
        compiler_params=_cp(("parallel", "arbitrary"), 48),
        name=name,
    )(a, b)


def _rope_block(x, cos, sin, first_half, half):
    partner = jnp.where(first_half, -pltpu.roll(x, LANES - half, 1), pltpu.roll(x, half, 1))
    return x * cos + partner * sin


def _rope_tables(n_lat, ident_rows):
    t = jnp.arange(n_lat)
    row = (t // GRID_W).astype(F32)
    col = (t % GRID_W).astype(F32)

    def angles(rot_dim):
        n_freq = rot_dim // 4
        inv = ROPE_THETA ** (-jnp.arange(n_freq, dtype=F32) / n_freq)
        return jnp.concatenate([row[:, None] * inv, col[:, None] * inv], axis=-1)

    a_m = angles(MLA_ROPE)
    zeros64 = jnp.zeros((n_lat, 64), F32)
    zeros32 = jnp.zeros((n_lat, 32), F32)
    cos_m = jnp.concatenate([zeros64 + 1.0, jnp.cos(a_m), jnp.cos(a_m), zeros32 + 1.0], axis=-1)
    sin_m = jnp.concatenate([zeros64, jnp.sin(a_m), jnp.sin(a_m), zeros32], axis=-1)
    a_d = angles(DIFF_HD)
    cos_d = jnp.tile(jnp.cos(a_d), (1, 4))
    sin_d = jnp.tile(jnp.sin(a_d), (1, 4))
    one = jnp.ones((ident_rows, LANES), F32)
    zero = jnp.zeros((ident_rows, LANES), F32)
    cat = lambda a, b: jnp.concatenate([a, b], axis=0)
    return cat(one, cos_m), cat(zero, sin_m), cat(one, cos_d), cat(zero, sin_d)


def _mla_prep_kernel(qc_ref, kvc_ref, kr_ref, cos_ref, sin_ref, gq_ref, gkv_ref, wuq_ref, wkv_ref,
                     q_ref, ckv_ref, k_ref, v_ref):
    cos = cos_ref[...]
    sin = sin_ref[...]
    lane = lax.broadcasted_iota(jnp.int32, (1, LANES), 1)
    first = lane < MLA_NOPE + MLA_ROPE // 2
    rope = lambda x: _rope_block(x, cos, sin, first, MLA_ROPE // 2)
    qn = _rms(qc_ref[...], gq_ref[...]).astype(BF16)
    q = jnp.dot(qn, wuq_ref[...], preferred_element_type=F32)
    for h in range(MLA_HEADS):
        sl = slice(h * LANES, (h + 1) * LANES)
        q_ref[:, sl] = (rope(q[:, sl]) * MLA_QSCALE).astype(BF16)
    ckv = _rms(kvc_ref[...], gkv_ref[...])
    ckv_ref[...] = ckv
    kv = jnp.dot(ckv.astype(BF16), wkv_ref[...], preferred_element_type=F32)
    krr = rope(kr_ref[...])
    for h in range(MLA_HEADS):
        sl = slice(h * LANES, (h + 1) * LANES)
        k_ref[:, sl] = (kv[:, sl] + krr).astype(BF16)
    v_ref[...] = kv[:, MLA_HEADS * LANES:].astype(BF16)


def _mla_prep(z, cos_m, sin_m, gq, gkv, wuq, wkv, rope_row_of):
    t = z.shape[0]
    tm = 512
    kw = MLA_HEADS * LANES
    vw = MLA_HEADS * MLA_V
    return pl.pallas_call(
        _mla_prep_kernel,
        out_shape=(jax.ShapeDtypeStruct((t, kw), BF16), jax.ShapeDtypeStruct((t, MLA_KV_LORA), F32),
                   jax.ShapeDtypeStruct((t, kw), BF16), jax.ShapeDtypeStruct((t, vw), BF16)),
        grid=(t // tm,),
        in_specs=[pl.BlockSpec((tm, MLA_Q_LORA), lambda i: (i, C_QC // MLA_Q_LORA)),
                  pl.BlockSpec((tm, MLA_KV_LORA), lambda i: (i, C_KVC // MLA_KV_LORA)),
                  pl.BlockSpec((tm, LANES), lambda i: (i, C_KR // LANES)),
                  pl.BlockSpec((tm, LANES), lambda i: (rope_row_of(i, tm), 0)),
                  pl.BlockSpec((tm, LANES), lambda i: (rope_row_of(i, tm), 0)),
                  pl.BlockSpec((1, MLA_Q_LORA), lambda i: (0, 0)),
                  pl.BlockSpec((1, MLA_KV_LORA), lambda i: (0, 0)),
                  pl.BlockSpec((MLA_Q_LORA, kw), lambda i: (0, 0)),
                  pl.BlockSpec((MLA_KV_LORA, kw + vw), lambda i: (0, 0))],
        out_specs=(pl.BlockSpec((tm, kw), lambda i: (i, 0)),
                   pl.BlockSpec((tm, MLA_KV_LORA), lambda i: (i, 0)),
                   pl.BlockSpec((tm, kw), lambda i: (i, 0)),
                   pl.BlockSpec((tm, vw), lambda i: (i, 0))),
        compiler_params=_cp(("parallel",), 40),
        name="mla_prep",
    )(z, z, z, cos_m, sin_m, gq, gkv, wuq, wkv)


def _mla_cache_kernel(ckv_ref, kr_ref, wkv_ref, k_ref, v_ref):
    kv = jnp.dot(ckv_ref[...].astype(BF16), wkv_ref[...], preferred_element_type=F32)
    kr = kr_ref[...]
    for h in range(MLA_HEADS):
        sl = slice(h * LANES, (h + 1) * LANES)
        k_ref[:, sl] = (kv[:, sl] + kr).astype(BF16)
    v_ref[...] = kv[:, MLA_HEADS * LANES:].astype(BF16)


def _mla_cache(cache_ckv, cache_kr_pad, wkv, layer):
    db, _, past, _ = cache_ckv.shape
    kw = MLA_HEADS * LANES
    vw = MLA_HEADS * MLA_V
    return pl.pallas_call(
        _mla_cache_kernel,
        out_shape=(jax.ShapeDtypeStruct((db * past, kw), BF16), jax.ShapeDtypeStruct((db * past, vw), BF16)),
        grid=(db,),
        in_specs=[pl.BlockSpec((None, None, past, MLA_KV_LORA), lambda b: (b, layer, 0, 0)),
                  pl.BlockSpec((None, None, past, LANES), lambda b: (b, layer, 0, 0)),
                  pl.BlockSpec((MLA_KV_LORA, kw + vw), lambda b: (0, 0))],
        out_specs=(pl.BlockSpec((past, kw), lambda b: (b, 0)), pl.BlockSpec((past, vw), lambda b: (b, 0))),
        compiler_params=_cp(("parallel",), 40),
        name="mla_cache",
    )(cache_ckv, cache_kr_pad, wkv)


def _mla_ctx_kernel(q_ref, k_ref, v_ref, o_ref, *, seq):
    lo = _lane_lo()
    for g in range(CTX_GROUP):
        rows = slice(g * seq, (g + 1) * seq)
        v = v_ref[rows, :]
        outs = []
        for hh in range(2):
            sl = slice(hh * LANES, (hh + 1) * LANES)
            ps, l = _softmax2_parts([_scores(q_ref[rows, sl], k_ref[rows, sl])])
            outs.append(_pv(ps, [v]) / l)
        o_ref[rows, :] = jnp.where(lo, outs[0], outs[1]).astype(o_ref.dtype)


def _mla_attn_ctx(q, k, v, out, n_seq, seq):
    pairs = MLA_HEADS // 2
    rows = CTX_GROUP * seq
    return pl.pallas_call(
        functools.partial(_mla_ctx_kernel, seq=seq),
        out_shape=out,
        grid=(n_seq // CTX_GROUP, pairs),
        in_specs=[pl.BlockSpec((rows, 2 * LANES), lambda s, p: (s, p)),
                  pl.BlockSpec((rows, 2 * LANES), lambda s, p: (s, p)),
                  pl.BlockSpec((rows, LANES), lambda s, p: (s, p))],
        out_specs=pl.BlockSpec((rows, LANES), lambda s, p: (s, p)),
        compiler_params=_cp(("parallel", "parallel"), 40),
        name="mla_attn_ctx",
    )(q, k, v)


def _mla_lat_kernel(q_ref, kl_ref, kc_ref, vl_ref, vc_ref, prev_ref, o_ref, s_ref, *, n_sub):
    del prev_ref
    tq = ATT_TQ
    k_refs = (kl_ref, kc_ref)
    v_refs = (vl_ref, vc_ref)
    chunks = _chunks((kl_ref.shape[0], kc_ref.shape[0]), ATT_CK)
    units = [(qs, hh) for qs in range(n_sub) for hh in range(2)]

    def stage_a(unit, slot, chunk, mrun):
        qs, hh = unit
        si, st, n, off = chunk
        sl = slice(hh * LANES, (hh + 1) * LANES)
        s = _scores(q_ref[qs * tq:(qs + 1) * tq, sl], k_refs[si][st:st + n, sl])
        s_ref[slot, :, off:off + n] = s
        return _tile_fold(jnp.maximum, mrun, s)

    def stage_b(slot, chunk, m, lrun, acc):
        si, st, n, off = chunk
        p = jnp.exp2(s_ref[slot, :, off:off + n] - m)
        lrun = _tile_fold(jnp.add, lrun, p)
        acc = acc + jnp.dot(p.astype(BF16), v_refs[si][st:st + n, :], preferred_element_type=F32)
        return lrun, acc

    neg = jnp.full((tq, LANES), -jnp.inf, F32)
    zero = jnp.zeros((tq, LANES), F32)
    mrun = neg
    for chunk in chunks:
        mrun = stage_a(units[0], 0, chunk, mrun)
    outs = {}
    for ui, unit in enumerate(units):
        slot = ui % 2
        m = jnp.max(mrun, axis=-1, keepdims=True)
        lrun, acc, mrun = zero, zero, neg
        for chunk in chunks:
            if ui + 1 < len(units):
                mrun = stage_a(units[ui + 1], 1 - slot, chunk, mrun)
            lrun, acc = stage_b(slot, chunk, m, lrun, acc)
        outs[unit] = acc / jnp.sum(lrun, axis=-1, keepdims=True)
    lo = _lane_lo()
    for qs in range(n_sub):
        o_ref[qs * tq:(qs + 1) * tq, :] = jnp.where(lo, outs[(qs, 0)], outs[(qs, 1)]).astype(o_ref.dtype)


def _mla_attn_lat(q, k, v, kc, vc, prev, t_ctx, db, ds, past):
    pairs = MLA_HEADS // 2
    n_sub = 2
    tqb = n_sub * ATT_TQ
    nq = ds // tqb
    q0 = t_ctx // tqb
    s0 = t_ctx // ds
    return pl.pallas_call(
        functools.partial(_mla_lat_kernel, n_sub=n_sub),
        out_shape=jax.ShapeDtypeStruct(prev.shape, prev.dtype),
        grid=(db, pairs, nq),
        in_specs=[pl.BlockSpec((tqb, 2 * LANES), lambda b, p, i: (q0 + b * nq + i, p)),
                  pl.BlockSpec((ds, 2 * LANES), lambda b, p, i: (s0 + b, p)),
                  pl.BlockSpec((past, 2 * LANES), lambda b, p, i: (b, p)),
                  pl.BlockSpec((ds, LANES), lambda b, p, i: (s0 + b, p)),
                  pl.BlockSpec((past, LANES), lambda b, p, i: (b, p)),
                  pl.BlockSpec(memory_space=pl.ANY)],
        out_specs=pl.BlockSpec((tqb, LANES), lambda b, p, i: (q0 + b * nq + i, p)),
        scratch_shapes=[pltpu.VMEM((2, ATT_TQ, ds + past), F32)],
        input_output_aliases={5: 0},
        compiler_params=_cp(("parallel", "parallel", "arbitrary"), 56),
        name="mla_attn_lat",
    )(q, k, kc, v, vc, prev)


def _attn64_ctx_kernel(q_ref, k_ref, v_ref, o_ref, *, seq):
    lo = _lane_lo()
    for g in range(CTX_GROUP):
        rows = slice(g * seq, (g + 1) * seq)
        q = q_ref[rows, :] * HEAD64_QSCALE
        k = k_ref[rows, :].astype(BF16)
        v = v_ref[rows, :].astype(BF16)
        outs = []
        for hh in range(2):
            qm = jnp.where(lo if hh == 0 else jnp.logical_not(lo), q, 0.0).astype(BF16)
            ps, l = _softmax2_parts([_scores(qm, k)])
            outs.append(_pv(ps, [v]) / l)
        o_ref[rows, :] = jnp.where(lo, outs[0], outs[1]).astype(o_ref.dtype)


def _na_attn_ctx(z, out, n_seq, seq):
    pairs = NA_HEADS // 2
    rows = CTX_GROUP * seq
    return pl.pallas_call(
        functools.partial(_attn64_ctx_kernel, seq=seq),
        out_shape=out,
        grid=(n_seq // CTX_GROUP, pairs),
        in_specs=[pl.BlockSpec((rows, LANES), lambda s, p: (s, C_NQ // LANES + p)),
                  pl.BlockSpec((rows, LANES), lambda s, p: (s, C_NK // LANES + p)),
                  pl.BlockSpec((rows, LANES), lambda s, p: (s, C_NV // LANES + p))],
        out_specs=pl.BlockSpec((rows, LANES), lambda s, p: (s, p)),
        compiler_params=_cp(("parallel", "parallel"), 40),
        name="na_attn_ctx",
    )(z, z, z)


def _na_plan(rows):
    krows = min(NA_KROWS, rows)
    wr = min(NA_WIN_R, rows)
    nblk = rows // NA_QROWS
    kbase = np.zeros((nblk,), np.int32)
    drmaps = np.zeros((nblk, NA_QROWS, krows), np.int32)
    invalid = 2 * NA_WIN_R - 1
    for blk in range(nblk):
        r0 = blk * NA_QROWS
        kb = int(np.clip(r0 - wr // 2, 0, rows - krows))
        kbase[blk] = kb
        for rr in range(NA_QROWS):
            r = r0 + rr
            w0 = int(np.clip(r - wr // 2, 0, rows - wr))
            assert kb <= w0 and w0 + wr <= kb + krows
            for kk in range(krows):
                krow = kb + kk
                drmaps[blk, rr, kk] = (krow - r + NA_WIN_R - 1) if (w0 <= krow < w0 + wr) else invalid
    pats, pat_of = np.unique(drmaps, axis=0, return_inverse=True)
    return kbase, pats, np.asarray(pat_of, np.int32).reshape(nblk), krows


def _na_bias_table(rpb, pats):
    h = rpb.shape[0]
    c = np.arange(GRID_W)[:, None]
    kc = np.arange(GRID_W)[None, :]
    cs = np.clip(c - NA_WIN_C // 2, 0, GRID_W - NA_WIN_C)
    valid = (kc >= cs) & (kc < cs + NA_WIN_C)
    idx = np.clip(kc - c + NA_WIN_C - 1, 0, 2 * NA_WIN_C - 2)
    e = jnp.where(valid[None, None], rpb[:, :, idx].astype(F32) * LOG2E, NEG_INF)
    e = jnp.concatenate([e, jnp.full((h, 1, GRID_W, GRID_W), NEG_INF, F32)], axis=1)
    b = e[:, pats]
    p, qr, kr = pats.shape
    return b.transpose(1, 0, 2, 4, 3, 5).reshape(p, h, qr * GRID_W, kr * GRID_W)


def _na_lat_kernel(pat_ref, kb_ref, q_ref, k_ref, v_ref, kc_ref, vc_ref, bias_ref, prev_ref, o_ref, s_ref, *, nk):
    del pat_ref, prev_ref
    blk = pl.program_id(2)
    kstart = pl.multiple_of(kb_ref[blk] * GRID_W, GRID_W)
    lo = _lane_lo()
    tq = q_ref.shape[0]
    chunks = _chunks((nk, kc_ref.shape[0]), ATT_CK)
    q = q_ref[...] * HEAD64_QSCALE
    qms = [jnp.where(lo, q, 0.0).astype(BF16), jnp.where(lo, 0.0, q).astype(BF16)]

    def kv_chunk(refs, chunk):
        si, st, n, _ = chunk
        if si == 0:
            return refs[0][pl.ds(kstart + st, n), :].astype(BF16)
        return refs[1][st:st + n, :].astype(BF16)

    def stage_a(hh, chunk, mrun):
        si, st, n, off = chunk
        s = _scores(qms[hh], kv_chunk((k_ref, kc_ref), chunk))
        if si == 0:
            s = s + bias_ref[hh, :, st:st + n]
        s_ref[hh, :, off:off + n] = s
        return _tile_fold(jnp.maximum, mrun, s)

    def stage_b(hh, chunk, m, lrun, acc):
        _, _, n, off = chunk
        p = jnp.exp2(s_ref[hh, :, off:off + n] - m)
        lrun = _tile_fold(jnp.add, lrun, p)
        acc = acc + jnp.dot(p.astype(BF16), kv_chunk((v_ref, vc_ref), chunk), preferred_element_type=F32)
        return lrun, acc

    neg = jnp.full((tq, LANES), -jnp.inf, F32)
    zero = jnp.zeros((tq, LANES), F32)
    mrun = neg
    for chunk in chunks:
        mrun = stage_a(0, chunk, mrun)
    outs = []
    for hh in range(2):
        m = jnp.max(mrun, axis=-1, keepdims=True)
        lrun, acc, mrun = zero, zero, neg
        for chunk in chunks:
            if hh == 0:
                mrun = stage_a(1, chunk, mrun)
            lrun, acc = stage_b(hh, chunk, m, lrun, acc)
        outs.append(acc / jnp.sum(lrun, axis=-1, keepdims=True))
    o_ref[...] = jnp.where(lo, outs[0], outs[1]).astype(o_ref.dtype)


def _na_attn_lat(z, cache_k, cache_v, bias, plan, prev, layer, t_ctx, db, ds, past):
    kbase, _, pat_of, krows = plan
    pairs = NA_HEADS // 2
    tq = NA_QROWS * GRID_W
    nk = krows * GRID_W
    nblk = ds // tq
    q0 = t_ctx // tq
    s0 = t_ctx // ds
    grid_spec = pltpu.PrefetchScalarGridSpec(
        num_scalar_prefetch=2,
        grid=(db, pairs, nblk),
        in_specs=[pl.BlockSpec((tq, LANES), lambda b, p, i, pat, kb: (q0 + b * nblk + i, C_NQ // LANES + p)),
                  pl.BlockSpec((ds, LANES), lambda b, p, i, pat, kb: (s0 + b, C_NK // LANES + p)),
                  pl.BlockSpec((ds, LANES), lambda b, p, i, pat, kb: (s0 + b, C_NV // LANES + p)),
                  pl.BlockSpec((None, None, past, LANES), lambda b, p, i, pat, kb: (b, layer, 0, p)),
                  pl.BlockSpec((None, None, past, LANES), lambda b, p, i, pat, kb: (b, layer, 0, p)),
                  pl.BlockSpec((None, 2, tq, nk), lambda b, p, i, pat, kb: (pat[i], p, 0, 0)),
                  pl.BlockSpec(memory_space=pl.ANY)],
        out_specs=pl.BlockSpec((tq, LANES), lambda b, p, i, pat, kb: (q0 + b * nblk + i, p)),
        scratch_shapes=[pltpu.VMEM((2, tq, nk + past), F32)],
    )
    return pl.pallas_call(
        functools.partial(_na_lat_kernel, nk=nk),
        out_shape=jax.ShapeDtypeStruct(prev.shape, prev.dtype),
        grid_spec=grid_spec,
        input_output_aliases={8: 0},
        compiler_params=_cp(("parallel", "parallel", "arbitrary"), 56),
        name="na_attn_lat",
    )(jnp.asarray(pat_of), jnp.asarray(kbase), z, z, z, cache_k, cache_v, bias, prev)


def _pool_kernel(*refs, t):
    u_ref, w_ref, sc_ref = refs[:3]
    o_ref, pad_ref = refs[-2:]
    halo = 8
    pos = lax.broadcasted_iota(jnp.int32, (t, POOL_GROUP), 0)
    zeros = jnp.zeros((halo, POOL_GROUP), F32)
    for gi, w in enumerate(POOL_WINDOWS):
        sl = slice(gi * POOL_GROUP, (gi + 1) * POOL_GROUP)
        u = u_ref[:, sl]
        pad_ref[0:halo, :] = zeros
        pad_ref[halo + t:2 * halo + t, :] = zeros
        pad_ref[halo:halo + t, :] = u
        tot = None
        for d in range(-(w // 2), w // 2):
            part = pad_ref[halo + d:halo + d + t, :]
            tot = part if tot is None else tot + part
        cnt = (jnp.minimum(pos + w // 2, t) - jnp.maximum(pos - w // 2, 0)).astype(F32)
        pooled = (tot / cnt - u).astype(BF16)
        mixed = jnp.dot(pooled, w_ref[gi], preferred_element_type=F32)
        o_ref[:, sl] = (mixed * sc_ref[:, sl]).astype(o_ref.dtype)


def _pool(z, pool_w, pool_scale, out, prev, n_seq, seq, row0, name):
    width = len(POOL_WINDOWS) * POOL_GROUP
    aliased = prev is not None
    in_specs = [pl.BlockSpec((seq, width), lambda s: (row0 + s, C_POOL // width)),
                pl.BlockSpec((len(POOL_WINDOWS), POOL_GROUP, POOL_GROUP), lambda s: (0, 0, 0)),
                pl.BlockSpec((1, width), lambda s: (0, 0))]
    args = [z, pool_w, pool_scale]
    if aliased:
        in_specs.append(pl.BlockSpec(memory_space=pl.ANY))
        args.append(prev)
        out = jax.ShapeDtypeStruct(prev.shape, prev.dtype)
    return pl.pallas_call(
        functools.partial(_pool_kernel, t=seq),
        out_shape=out,
        grid=(n_seq,),
        in_specs=in_specs,
        out_specs=pl.BlockSpec((seq, width), lambda s: (row0 + s, 0)),
        scratch_shapes=[pltpu.VMEM((seq + 16, POOL_GROUP), F32)],
        input_output_aliases={3: 0} if aliased else {},
        compiler_params=_cp(("parallel",), 56),
        name=name,
    )(*args)


def _diff_prep_kernel(q_ref, k_ref, v_ref, cos_ref, sin_ref, qo_ref, ko_ref, vo_ref):
    cos = cos_ref[...]
    sin = sin_ref[...]
    lane = lax.broadcasted_iota(jnp.int32, (1, LANES), 1)
    first = (lane % DIFF_HD) < DIFF_HD // 2
    for h in range(DIFF_HEADS):
        sl = slice(h * LANES, (h + 1) * LANES)
        qo_ref[:, sl] = (_rope_block(q_ref[:, sl], cos, sin, first, DIFF_HD // 2) * HEAD64_QSCALE).astype(BF16)
        ko_ref[:, sl] = _rope_block(k_ref[:, sl], cos, sin, first, DIFF_HD // 2).astype(BF16)
    vo_ref[...] = v_ref[...].astype(BF16)


def _diff_prep(z, cos_d, sin_d, rope_row_of):
    t = z.shape[0]
    tm = 512
    w = DIFF_HEADS * LANES
    sds = jax.ShapeDtypeStruct((t, w), BF16)
    return pl.pallas_call(
        _diff_prep_kernel,
        out_shape=(sds, sds, sds),
        grid=(t // tm,),
        in_specs=[pl.BlockSpec((tm, w), lambda i: (i, C_DQ // w)),
                  pl.BlockSpec((tm, w), lambda i: (i, C_DK // w)),
                  pl.BlockSpec((tm, w), lambda i: (i, C_DV // w)),
                  pl.BlockSpec((tm, LANES), lambda i: (rope_row_of(i, tm), 0)),
                  pl.BlockSpec((tm, LANES), lambda i: (rope_row_of(i, tm), 0))],
        out_specs=(pl.BlockSpec((tm, w), lambda i: (i, 0)),) * 3,
        compiler_params=_cp(("parallel",), 40),
        name="diff_prep",
    )(z, z, z, cos_d, sin_d)


def _diff_lambda(linit_ref, lam_ref):
    lam_init = linit_ref[0]
    lp = lam_ref[...]
    lam = (jnp.exp(jnp.sum(lp[0:1] * lp[1:2], axis=-1, keepdims=True))
           - jnp.exp(jnp.sum(lp[2:3] * lp[3:4], axis=-1, keepdims=True)) + lam_init)
    return lam, lam_init


def _diff_ctx_kernel(linit_ref, lam_ref, g_ref, q_ref, k_ref, v_ref, o_ref, *, seq):
    lam, lam_init = _diff_lambda(linit_ref, lam_ref)
    lo = _lane_lo()
    for g in range(CTX_GROUP):
        rows = slice(g * seq, (g + 1) * seq)
        q = q_ref[rows, :]
        k = k_ref[rows, :]
        zero = jnp.zeros_like(q)
        p1, l1 = _softmax2_parts([_scores(jnp.where(lo, q, zero), k)])
        p2, l2 = _softmax2_parts([_scores(jnp.where(lo, zero, q), k)])
        o = _pv([p1[0] * (1.0 / l1) - p2[0] * (lam / l2)], [v_ref[rows, :]])
        o_ref[rows, :] = (_rms(o, g_ref[...]) * (1.0 - lam_init)).astype(o_ref.dtype)


def _diff_attn_ctx(linit, lam_p, g, q, k, v, out, layer, n_seq, seq):
    smem = pl.BlockSpec(memory_space=pltpu.SMEM)
    rows = CTX_GROUP * seq
    return pl.pallas_call(
        functools.partial(_diff_ctx_kernel, seq=seq),
        out_shape=out,
        grid=(n_seq // CTX_GROUP, DIFF_HEADS),
        in_specs=[smem,
                  pl.BlockSpec((None, 4, DIFF_HD), lambda s, h: (layer, 0, 0)),
                  pl.BlockSpec((1, LANES), lambda s, h: (0, 0)),
                  pl.BlockSpec((rows, LANES), lambda s, h: (s, h)),
                  pl.BlockSpec((rows, LANES), lambda s, h: (s, h)),
                  pl.BlockSpec((rows, LANES), lambda s, h: (s, h))],
        out_specs=pl.BlockSpec((rows, LANES), lambda s, h: (s, h)),
        compiler_params=_cp(("parallel", "parallel"), 40),
        name="diff_attn_ctx",
    )(linit, lam_p, g, q, k, v)


def _diff_lat_kernel(linit_ref, lam_ref, g_ref, q_ref, kl_ref, kc_ref, vl_ref, vc_ref, prev_ref, o_ref, s_ref,
                     *, n_sub):
    del prev_ref
    tq = ATT_TQ
    lam, lam_init = _diff_lambda(linit_ref, lam_ref)
    lo = _lane_lo()
    k_refs = (kl_ref, kc_ref)
    v_refs = (vl_ref, vc_ref)
    chunks = _chunks((kl_ref.shape[0], kc_ref.shape[0]), ATT_CK)

    def q_maps(qs):
        q = q_ref[qs * tq:(qs + 1) * tq, :]
        zero = jnp.zeros_like(q)
        return jnp.where(lo, q, zero), jnp.where(lo, zero, q)

    def stage_a(qms, slot, chunk, mruns):
        si, st, n, off = chunk
        k = k_refs[si][st:st + n, :].astype(BF16)
        out = []
        for mi in range(2):
            s = _scores(qms[mi], k)
            s_ref[slot, mi, :, off:off + n] = s
            out.append(_tile_fold(jnp.maximum, mruns[mi], s))
        return out

    def stage_b1(slot, chunk, ms, lruns):
        _, _, n, off = chunk
        out = []
        for mi in range(2):
            e = jnp.exp2(s_ref[slot, mi, :, off:off + n] - ms[mi])
            s_ref[slot, mi, :, off:off + n] = e
            out.append(_tile_fold(jnp.add, lruns[mi], e))
        return out

    def stage_b2(slot, chunk, r1, r2, acc):
        si, st, n, off = chunk
        p = s_ref[slot, 0, :, off:off + n] * r1 - s_ref[slot, 1, :, off:off + n] * r2
        return acc + jnp.dot(p.astype(BF16), v_refs[si][st:st + n, :].astype(BF16), preferred_element_type=F32)

    neg = jnp.full((tq, LANES), -jnp.inf, F32)
    zero = jnp.zeros((tq, LANES), F32)
    mruns = [neg, neg]
    qms = q_maps(0)
    for chunk in chunks:
        mruns = stage_a(qms, 0, chunk, mruns)
    for qs in range(n_sub):
        slot = qs % 2
        ms = [jnp.max(mr, axis=-1, keepdims=True) for mr in mruns]
        lruns, mruns = [zero, zero], [neg, neg]
        if qs + 1 < n_sub:
            qms = q_maps(qs + 1)
        for chunk in chunks:
            if qs + 1 < n_sub:
                mruns = stage_a(qms, 1 - slot, chunk, mruns)
            lruns = stage_b1(slot, chunk, ms, lruns)
        r1 = 1.0 / jnp.sum(lruns[0], axis=-1, keepdims=True)
        r2 = lam / jnp.sum(lruns[1], axis=-1, keepdims=True)
        acc = zero
        for chunk in chunks:
            acc = stage_b2(slot, chunk, r1, r2, acc)
        o_ref[qs * tq:(qs + 1) * tq, :] = (_rms(acc, g_ref[...]) * (1.0 - lam_init)).astype(o_ref.dtype)


def _diff_attn_lat(linit, lam_p, g, q, k, v, cache_k, cache_v, prev, layer, t_ctx, db, ds, past):
    smem = pl.BlockSpec(memory_space=pltpu.SMEM)
    n_sub = 4
    tqb = n_sub * ATT_TQ
    nq = ds // tqb
    q0 = t_ctx // tqb
    s0 = t_ctx // ds
    return pl.pallas_call(
        functools.partial(_diff_lat_kernel, n_sub=n_sub),
        out_shape=jax.ShapeDtypeStruct(prev.shape, prev.dtype),
        grid=(db, DIFF_HEADS, nq),
        in_specs=[smem,
                  pl.BlockSpec((None, 4, DIFF_HD), lambda b, h, i: (layer, 0, 0)),
                  pl.BlockSpec((1, LANES), lambda b, h, i: (0, 0)),
                  pl.BlockSpec((tqb, LANES), lambda b, h, i: (q0 + b * nq + i, h)),
                  pl.BlockSpec((ds, LANES), lambda b, h, i: (s0 + b, h)),
                  pl.BlockSpec((None, None, past, LANES), lambda b, h, i: (b, layer, 0, h)),
                  pl.BlockSpec((ds, LANES), lambda b, h, i: (s0 + b, h)),
                  pl.BlockSpec((None, None, past, LANES), lambda b, h, i: (b, layer, 0, h)),
                  pl.BlockSpec(memory_space=pl.ANY)],
        out_specs=pl.BlockSpec((tqb, LANES), lambda b, h, i: (q0 + b * nq + i, h)),
        scratch_shapes=[pltpu.VMEM((2, 2, ATT_TQ, ds + past), F32)],
        input_output_aliases={8: 0},
        compiler_params=_cp(("parallel", "parallel", "arbitrary"), 56),
        name="diff_attn_lat",
    )(linit, lam_p, g, q, k, cache_k, v, cache_v, prev)


def _branch_kernel(oa_ref, ob_ref, oc_ref, od_ref, g0_ref, g1_ref, g2_ref, g3_ref, w_ref, out_ref):
    acc = None
    for bi, (o_ref, g_ref) in enumerate(((oa_ref, g0_ref), (ob_ref, g1_ref), (oc_ref, g2_ref), (od_ref, g3_ref))):
        term = jax.nn.sigmoid(g_ref[...]) * jnp.dot(o_ref[...], w_ref[bi], preferred_element_type=F32)
        acc = term if acc is None else acc + term
    out_ref[...] = acc.astype(out_ref.dtype)


def _branch_merge(outs, z, w_br, d):
    t, bw = outs[0].shape
    tm, tn = 512, 1024
    nb = d // tn
    o_spec = pl.BlockSpec((tm, bw), lambda i, j: (i, 0))
    g_specs = [pl.BlockSpec((tm, tn), functools.partial(lambda i, j, bi: (i, C_GATES // tn + bi * nb + j), bi=bi))
               for bi in range(4)]
    return pl.pallas_call(
        _branch_kernel,
        out_shape=jax.ShapeDtypeStruct((t, d), BF16),
        grid=(t // tm, nb),
        in_specs=[o_spec] * 4 + g_specs + [pl.BlockSpec((4, bw, tn), lambda i, j: (0, 0, j))],
        out_specs=pl.BlockSpec((tm, tn), lambda i, j: (i, j)),
        compiler_params=_cp(("parallel", "arbitrary"), 56),
        name="branch_merge",
    )(*outs, z, z, z, z, w_br)


def _wo_kernel(m_ref, wo_ref, x_ref, mod_ref, gn_ref, x1_ref, h2_ref, *, d):
    y = jnp.dot(m_ref[...], wo_ref[...], preferred_element_type=F32)
    mod = mod_ref[0]
    g1 = mod[:, 2 * d:3 * d]
    sh2 = mod[:, 3 * d:4 * d]
    sc2 = mod[:, 4 * d:5 * d]
    x1 = x_ref[...] + g1 * _rms(y, gn_ref[1:2, :])
    x1_ref[...] = x1
    h2_ref[...] = (_rms(x1, gn_ref[2:3, :]) * (1.0 + sc2) + sh2).astype(BF16)


def _wo_residual(merged, w_o, x, mod_l, gn, row_of):
    t, d = x.shape
    tm = 256
    return pl.pallas_call(
        functools.partial(_wo_kernel, d=d),
        out_shape=(jax.ShapeDtypeStruct((t, d), F32), jax.ShapeDtypeStruct((t, d), BF16)),
        grid=(t // tm,),
        in_specs=[pl.BlockSpec((tm, d), lambda i: (i, 0)),
                  pl.BlockSpec((d, d), lambda i: (0, 0)),
                  pl.BlockSpec((tm, d), lambda i: (i, 0)),
                  pl.BlockSpec((1, 1, 6 * d), lambda i: (row_of(i, tm), 0, 0)),
                  pl.BlockSpec((4, d), lambda i: (0, 0))],
        out_specs=(pl.BlockSpec((tm, d), lambda i: (i, 0)), pl.BlockSpec((tm, d), lambda i: (i, 0))),
        compiler_params=_cp(("parallel",), 56),
        name="wo_residual",
    )(merged, w_o, x, mod_l, gn)


def _ffn_kernel(*refs, d, emit_next):
    h_ref, wu_ref, wd_ref, x_ref, mod_ref, gn_ref = refs[:6]
    o_ref = refs[8] if emit_next else refs[6]
    f = pl.program_id(1)

    @pl.when(f == 0)
    def _():
        o_ref[...] = jnp.zeros_like(o_ref)

    u = jnp.dot(h_ref[...], wu_ref[...], preferred_element_type=F32)
    u = jnp.square(jnp.maximum(u, 0.0)).astype(BF16)
    o_ref[...] += jnp.dot(u, wd_ref[...], preferred_element_type=F32)

    @pl.when(f == pl.num_programs(1) - 1)
    def _():
        g2 = mod_ref[0][:, 5 * d:6 * d]
        x_new = x_ref[...] + g2 * _rms(o_ref[...], gn_ref[3:4, :])
        o_ref[...] = x_new
        if emit_next:
            modn_ref, gnn_ref, hn_ref = refs[6], refs[7], refs[9]
            modn = modn_ref[0]
            hn_ref[...] = (_rms(x_new, gnn_ref[0:1, :]) * (1.0 + modn[:, d:2 * d]) + modn[:, 0:d]).astype(BF16)


def _ffn(h2, w_up, w_down, x1, mod_l, gn, row_of, mod_next=None, gn_next=None):
    t, d = x1.shape
    dff = w_up.shape[1]
    tm, tf = 512, 512
    emit_next = mod_next is not None
    mod_spec = pl.BlockSpec((1, 1, 6 * d), lambda i, f: (row_of(i, tm), 0, 0))
    gn_spec = pl.BlockSpec((4, d), lambda i, f: (0, 0))
    row_spec = pl.BlockSpec((tm, d), lambda i, f: (i, 0))
    in_specs = [row_spec,
                pl.BlockSpec((d, tf), lambda i, f: (0, f)),
                pl.BlockSpec((tf, d), lambda i, f: (f, 0)),
                row_spec, mod_spec, gn_spec]
    args = [h2, w_up, w_down, x1, mod_l, gn]
    out_shape = jax.ShapeDtypeStruct((t, d), F32)
    out_specs = row_spec
    if emit_next:
        in_specs += [mod_spec, gn_spec]
        args += [mod_next, gn_next]
        out_shape = (out_shape, jax.ShapeDtypeStruct((t, d), BF16))
        out_specs = (row_spec, row_spec)
    return pl.pallas_call(
        functools.partial(_ffn_kernel, d=d, emit_next=emit_next),
        out_shape=out_shape,
        grid=(t // tm, dff // tf),
        in_specs=in_specs,
        out_specs=out_specs,
        compiler_params=_cp(("parallel", "arbitrary"), 56),
        name="ffn",
    )(*args)


def _prep_weights(w_in, w_uq, w_ukv):
    depth, d, _ = w_in.shape
    sizes = (MLA_Q_LORA, MLA_KV_LORA, MLA_ROPE, 512, 512, 512, 512, 512, 512, 512, 4 * d)
    offs = np.cumsum((0,) + sizes)
    part = lambda i: w_in[:, :, offs[i]:offs[i + 1]]
    q_c, kv_c, k_r, na_q, na_k, na_v, pool, dq, dk, dv, gates = (part(i) for i in range(11))
    kr_blk = jnp.pad(k_r, ((0, 0), (0, 0), (MLA_NOPE, LANES - MLA_NOPE - MLA_ROPE)))
    w_all = jnp.concatenate([gates, q_c, dq, dk, dv, na_q, na_k, na_v, pool, kv_c, kr_blk], axis=-1).astype(BF16)
    hd = MLA_NOPE + MLA_ROPE
    wuq = jnp.pad(w_uq.reshape(depth, MLA_Q_LORA, MLA_HEADS, hd), ((0, 0), (0, 0), (0, 0), (0, LANES - hd)))
    wuq = wuq.reshape(depth, MLA_Q_LORA, MLA_HEADS * LANES).astype(BF16)
    wkv4 = w_ukv.reshape(depth, MLA_KV_LORA, MLA_HEADS, MLA_NOPE + MLA_V)
    wk = jnp.pad(wkv4[..., :MLA_NOPE], ((0, 0), (0, 0), (0, 0), (0, LANES - MLA_NOPE)))
    wk = wk.reshape(depth, MLA_KV_LORA, MLA_HEADS * LANES)
    wv = wkv4[..., MLA_NOPE:].reshape(depth, MLA_KV_LORA, MLA_HEADS * MLA_V)
    wkv = jnp.concatenate([wk, wv], axis=-1).astype(BF16)
    return w_all, wuq, wkv


def kernel(x_prompt, x_sample, cache_mla_ckv, cache_mla_krope, cache_na_k, cache_na_v, cache_diff_k, cache_diff_v, c, c_ctx, w_mod, b_mod, g_norm, w_in, g_q_lora, g_kv_lora, w_uq, w_ukv, na_rpb, pool_w, pool_scale, diff_lambda, diff_norm_g, w_br, w_o, w_up, w_down):
    nb, seq, d = x_prompt.shape
    db, ds, _ = x_sample.shape
    depth = w_in.shape[0]
    past = cache_mla_ckv.shape[2]
    t_ctx = nb * seq
    t_lat = db * ds
    t = t_ctx + t_lat
    assert t_ctx % ds == 0 and ds % 1024 == 0 and t_ctx % 1024 == 0 and seq % 8 == 0 and nb % CTX_GROUP == 0

    def row_of(i, tm):
        n_ctx = t_ctx // tm
        return jnp.where(i < n_ctx, 0, 1 + (i - n_ctx) // (ds // tm))

    def rope_row_of(i, tm):
        n_ctx = t_ctx // tm
        return jnp.where(i < n_ctx, 0, 1 + (i - n_ctx) % (ds // tm))

    w_all, wuq, wkv = _prep_weights(w_in, w_uq, w_ukv)
    w_br_b = w_br.astype(BF16)
    w_o_b = w_o.astype(BF16)
    w_up_b = w_up.astype(BF16)
    w_down_b = w_down.astype(BF16)
    pool_w_b = pool_w.astype(BF16)
    cos_m, sin_m, cos_d, sin_d = _rope_tables(ds, 512)
    rows = ds // GRID_W
    na_plan = _na_plan(rows)
    cache_kr_pad = jnp.pad(cache_mla_krope, ((0, 0), (0, 0), (0, 0), (MLA_NOPE, LANES - MLA_NOPE - MLA_ROPE)))
    cache_nk = cache_na_k.reshape(db, depth, past, NA_HEADS * NA_HD)
    cache_nv = cache_na_v.reshape(db, depth, past, NA_HEADS * NA_HD)
    cache_dk = cache_diff_k.reshape(db, depth, past, DIFF_HEADS * 2 * DIFF_HD)
    cache_dv = cache_diff_v.reshape(db, depth, past, DIFF_HEADS * 2 * DIFF_HD)

    n_rows = 1 + db
    r_pad = -(-n_rows // 8) * 8
    cond = jnp.concatenate([c_ctx[None, :], c, jnp.zeros((r_pad - n_rows, d), F32)], axis=0)
    mod = _modulation(cond, w_mod, b_mod)
    mods = [mod[li].reshape(r_pad, 1, 6 * d) for li in range(depth)]

    x = jnp.concatenate([x_prompt.reshape(t_ctx, d), x_sample.reshape(t_lat, d)], axis=0)
    branch_sds = jax.ShapeDtypeStruct((t, 512), BF16)
    h = _norm_mod(x, mods[0], g_norm[0][0:1], row_of)
    states = []
    for li in range(depth):
        mod_l = mods[li]
        gn = g_norm[li]
        z = _matmul(h, w_all[li], 1024, 1152, F32, "in_proj")

        q_a, ckv, k_a, v_a = _mla_prep(z, cos_m, sin_m, g_q_lora[li][None, :], g_kv_lora[li][None, :],
                                       wuq[li], wkv[li], rope_row_of)
        kc_a, vc_a = _mla_cache(cache_mla_ckv, cache_kr_pad, wkv[li], li)
        o_a = _mla_attn_ctx(q_a, k_a, v_a, branch_sds, nb, seq)
        o_a = _mla_attn_lat(q_a, k_a, v_a, kc_a, vc_a, o_a, t_ctx, db, ds, past)

        bias = _na_bias_table(na_rpb[li], na_plan[1])
        o_b = _na_attn_ctx(z, branch_sds, nb, seq)
        o_b = _na_attn_lat(z, cache_nk, cache_nv, bias, na_plan, o_b, li, t_ctx, db, ds, past)

        o_c = _pool(z, pool_w_b[li], pool_scale[li][None, :], branch_sds, None, nb, seq, 0, "pool_ctx")
        o_c = _pool(z, pool_w_b[li], pool_scale[li][None, :], None, o_c, db, ds, t_ctx // ds, "pool_lat")

        q_d, k_d, v_d = _diff_prep(z, cos_d, sin_d, rope_row_of)
        linit = jnp.full((1,), 0.8 - 0.6 * math.exp(-0.3 * li), F32)
        g_d = diff_norm_g[li][None, :]
        o_d = _diff_attn_ctx(linit, diff_lambda, g_d, q_d, k_d, v_d, branch_sds, li, nb, seq)
        o_d = _diff_attn_lat(linit, diff_lambda, g_d, q_d, k_d, v_d, cache_dk, cache_dv, o_d,
                             li, t_ctx, db, ds, past)

        merged = _branch_merge((o_a, o_b, o_c, o_d), z, w_br_b[li], d)
        x1, h2 = _wo_residual(merged, w_o_b[li], x, mod_l, gn, row_of)
        if li + 1 < depth:
            x, h = _ffn(h2, w_up_b[li], w_down_b[li], x1, mod_l, gn, row_of, mods[li + 1], g_norm[li + 1])
        else:
            x = _ffn(h2, w_up_b[li], w_down_b[li], x1, mod_l, gn, row_of)

        zc = z[:t_ctx]
        states.append((ckv[:t_ctx].reshape(nb, seq, MLA_KV_LORA),
                       zc[:, C_KR + MLA_NOPE:C_KR + MLA_NOPE + MLA_ROPE].reshape(nb, seq, MLA_ROPE),
                       zc[:, C_NK:C_NK + 512].reshape(nb, seq, NA_HEADS, NA_HD),
                       zc[:, C_NV:C_NV + 512].reshape(nb, seq, NA_HEADS, NA_HD),
                       zc[:, C_DK:C_DK + 512].reshape(nb, seq, DIFF_HEADS, 2 * DIFF_HD),
                       zc[:, C_DV:C_DV + 512].reshape(nb, seq, DIFF_HEADS, 2 * DIFF_HD)))

    y_prompt = x[:t_ctx].reshape(nb, seq, d)
    y_sample = x[t_ctx:].reshape(db, ds, d)
    st = [jnp.stack([s[k] for s in states], axis=1) for k in range(6)]
    return (y_prompt, y_sample, *st)
```

```python
import functools
import math

import numpy as np
import jax
import jax.numpy as jnp
from jax import lax
from jax.experimental import pallas as pl
from jax.experimental.pallas import tpu as pltpu

F32 = jnp.float32
BF16 = jnp.bfloat16

GRID_W = 64
MLA_HEADS = 8
MLA_NOPE = 64
MLA_ROPE = 32
MLA_V = 64
MLA_Q_LORA = 512
MLA_KV_LORA = 256
NA_HEADS = 8
NA_HD = 64
NA_WIN_R = 8
NA_WIN_C = 16
POOL_WINDOWS = (2, 4, 8, 16)
POOL_GROUP = 128
DIFF_HEADS = 4
DIFF_HD = 64
ROPE_THETA = 10000.0
RMS_EPS = 1e-6
NEG_INF = -1e30
LOG2E = math.log2(math.e)
MLA_QSCALE = (MLA_NOPE + MLA_ROPE) ** -0.5 * LOG2E
HEAD64_QSCALE = 64 ** -0.5 * LOG2E
LANES = 128
MIB = 1024 * 1024

C_QC = 0
C_DQ = 512
C_DK = 1024
C_DV = 1536
C_NQ = 2048
C_NK = 2560
C_NV = 3072
C_POOL = 3584
C_KVC = 4096
C_KR = 4352
Z_COLS = 4480

MERGE_TN = 256
NA_STEP_HEADS = 4
NA_QROWS = 8
NA_KROWS = 16
ATT_TQ = 256
ATT_CK = 512
CTX_GROUP = 4


def _cp(sem, vmem_mib):
    return pltpu.CompilerParams(dimension_semantics=sem, vmem_limit_bytes=vmem_mib * MIB)


def _rms(x, g):
    return x * lax.rsqrt(jnp.mean(x * x, axis=-1, keepdims=True) + RMS_EPS) * g


def _scores(q, k):
    return lax.dot_general(q, k, (((1,), (1,)), ((), ())), preferred_element_type=F32)


def _softmax2_parts(ss):
    m = functools.reduce(jnp.maximum, [jnp.max(s, axis=-1, keepdims=True) for s in ss])
    ps = [jnp.exp2(s - m) for s in ss]
    l = functools.reduce(jnp.add, [jnp.sum(p, axis=-1, keepdims=True) for p in ps])
    return ps, l


def _pv(ps, vs):
    acc = None
    for p, v in zip(ps, vs):
        t = jnp.dot(p.astype(BF16), v, preferred_element_type=F32)
        acc = t if acc is None else acc + t
    return acc


def _lane_lo():
    return lax.broadcasted_iota(jnp.int32, (1, LANES), 1) < 64


def _tile_fold(op, run, x):
    for c in range(x.shape[1] // LANES):
        run = op(run, x[:, c * LANES:(c + 1) * LANES])
    return run


def _chunks(sizes, ck):
    out, off = [], 0
    for si, n in enumerate(sizes):
        step = min(ck, n)
        for st in range(0, n, step):
            out.append((si, st, step, off))
            off += step
    return out


def _mod_kernel(c_ref, w_ref, b_ref, o_ref):
    c = c_ref[...]
    s = c * jax.nn.sigmoid(c)
    o_ref[0] = jnp.dot(s.astype(BF16), w_ref[0].astype(BF16), preferred_element_type=F32) + b_ref[0]


def _modulation(cond, w_mod, b_mod):
    depth, d, n = w_mod.shape
    r = cond.shape[0]
    tn = 1024
    return pl.pallas_call(
        _mod_kernel,
        out_shape=jax.ShapeDtypeStruct((depth, r, n), F32),
        grid=(depth, n // tn),
        in_specs=[pl.BlockSpec((r, d), lambda l, j: (0, 0)),
                  pl.BlockSpec((1, d, tn), lambda l, j: (l, 0, j)),
                  pl.BlockSpec((1, 1, tn), lambda l, j: (l, 0, j))],
        out_specs=pl.BlockSpec((1, r, tn), lambda l, j: (l, 0, j)),
        compiler_params=_cp(("parallel", "parallel"), 40),
        name="modulation",
    )(cond, w_mod, b_mod.reshape(depth, 1, n))


def _norm_mod_kernel(xp_ref, xs_ref, mod_ref, g_ref, x_ref, h_ref, *, d, n_ctx):
    mod = mod_ref[0]
    sh = mod[:, 0:d]
    sc = mod[:, d:2 * d]

    def emit(src_ref):
        x = src_ref[...]
        x_ref[...] = x
        h_ref[...] = (_rms(x, g_ref[...]) * (1.0 + sc) + sh).astype(BF16)

    @pl.when(pl.program_id(0) < n_ctx)
    def _():
        emit(xp_ref)

    @pl.when(pl.program_id(0) >= n_ctx)
    def _():
        emit(xs_ref)


def _norm_mod(xp, xs, mod_l, g, row_of):
    (t_ctx, d), t_lat = xp.shape, xs.shape[0]
    tm = 512
    n_ctx = t_ctx // tm
    t = t_ctx + t_lat
    return pl.pallas_call(
        functools.partial(_norm_mod_kernel, d=d, n_ctx=n_ctx),
        out_shape=(jax.ShapeDtypeStruct((t, d), F32), jax.ShapeDtypeStruct((t, d), BF16)),
        grid=(t // tm,),
        in_specs=[pl.BlockSpec((tm, d), lambda i: (jnp.minimum(i, n_ctx - 1), 0)),
                  pl.BlockSpec((tm, d), lambda i: (jnp.maximum(i - n_ctx, 0), 0)),
                  pl.BlockSpec((1, 1, 6 * d), lambda i: (row_of(i, tm), 0, 0)),
                  pl.BlockSpec((1, d), lambda i: (0, 0))],
        out_specs=(pl.BlockSpec((tm, d), lambda i: (i, 0)), pl.BlockSpec((tm, d), lambda i: (i, 0))),
        compiler_params=_cp(("parallel",), 48),
        name="norm_mod",
    )(xp, xs, mod_l, g)


def _mm_kernel(a_ref, b_ref, o_ref):
    o_ref[...] = jnp.dot(a_ref[...], b_ref[...], preferred_element_type=F32).astype(o_ref.dtype)


def _matmul(a, b, tm, tn, out_dtype, name):
    m, k = a.shape
    n = b.shape[1]
    return pl.pallas_call(
        _mm_kernel,
        out_shape=jax.ShapeDtypeStruct((m, n), out_dtype),
        grid=(m // tm, n // tn),
        in_specs=[pl.BlockSpec((tm, k), lambda i, j: (i, 0)),
                  pl.BlockSpec((k, tn), lambda i, j: (0, j))],
        out_specs=pl.BlockSpec((tm, tn), lambda i, j: (i, j)),
        compiler_params=_cp(("parallel", "arbitrary"), 48),
        name=name,
    )(a, b)


def _rope_block(x, cos, sin, first_half, half):
    partner = jnp.where(first_half, -pltpu.roll(x, LANES - half, 1), pltpu.roll(x, half, 1))
    return x * cos + partner * sin


def _rope_tables(n_lat, ident_rows):
    t = jnp.arange(n_lat)
    row = (t // GRID_W).astype(F32)
    col = (t % GRID_W).astype(F32)

    def angles(rot_dim):
        n_freq = rot_dim // 4
        inv = ROPE_THETA ** (-jnp.arange(n_freq, dtype=F32) / n_freq)
        return jnp.concatenate([row[:, None] * inv, col[:, None] * inv], axis=-1)

    a_m = angles(MLA_ROPE)
    zeros64 = jnp.zeros((n_lat, 64), F32)
    zeros32 = jnp.zeros((n_lat, 32), F32)
    cos_m = jnp.concatenate([zeros64 + 1.0, jnp.cos(a_m), jnp.cos(a_m), zeros32 + 1.0], axis=-1)
    sin_m = jnp.concatenate([zeros64, jnp.sin(a_m), jnp.sin(a_m), zeros32], axis=-1)
    a_d = angles(DIFF_HD)
    cos_d = jnp.tile(jnp.cos(a_d), (1, 4))
    sin_d = jnp.tile(jnp.sin(a_d), (1, 4))
    one = jnp.ones((ident_rows, LANES), F32)
    zero = jnp.zeros((ident_rows, LANES), F32)
    cat = lambda a, b: jnp.concatenate([a, b], axis=0)
    return cat(one, cos_m), cat(zero, sin_m), cat(one, cos_d), cat(zero, sin_d)


def _mla_prep_kernel(qc_ref, kvc_ref, kr_ref, cos_ref, sin_ref, gq_ref, gkv_ref, wuq_ref, wkv_ref,
                     q_ref, ckv_ref, k_ref, v_ref):
    cos = cos_ref[...]
    sin = sin_ref[...]
    lane = lax.broadcasted_iota(jnp.int32, (1, LANES), 1)
    first = lane < MLA_NOPE + MLA_ROPE // 2
    rope = lambda x: _rope_block(x, cos, sin, first, MLA_ROPE // 2)
    qn = _rms(qc_ref[...], gq_ref[...]).astype(BF16)
    q = jnp.dot(qn, wuq_ref[...], preferred_element_type=F32)
    for h in range(MLA_HEADS):
        sl = slice(h * LANES, (h + 1) * LANES)
        q_ref[:, sl] = (rope(q[:, sl]) * MLA_QSCALE).astype(BF16)
    ckv = _rms(kvc_ref[...], gkv_ref[...])
    ckv_ref[...] = ckv
    kv = jnp.dot(ckv.astype(BF16), wkv_ref[...], preferred_element_type=F32)
    krr = rope(kr_ref[...])
    for h in range(MLA_HEADS):
        sl = slice(h * LANES, (h + 1) * LANES)
        k_ref[:, sl] = (kv[:, sl] + krr).astype(BF16)
    v_ref[...] = kv[:, MLA_HEADS * LANES:].astype(BF16)


def _mla_prep(z, cos_m, sin_m, gq, gkv, wuq, wkv, rope_row_of):
    t = z.shape[0]
    tm = 512
    kw = MLA_HEADS * LANES
    vw = MLA_HEADS * MLA_V
    return pl.pallas_call(
        _mla_prep_kernel,
        out_shape=(jax.ShapeDtypeStruct((t, kw), BF16), jax.ShapeDtypeStruct((t, MLA_KV_LORA), F32),
                   jax.ShapeDtypeStruct((t, kw), BF16), jax.ShapeDtypeStruct((t, vw), BF16)),
        grid=(t // tm,),
        in_specs=[pl.BlockSpec((tm, MLA_Q_LORA), lambda i: (i, C_QC // MLA_Q_LORA)),
                  pl.BlockSpec((tm, MLA_KV_LORA), lambda i: (i, C_KVC // MLA_KV_LORA)),
                  pl.BlockSpec((tm, LANES), lambda i: (i, C_KR // LANES)),
                  pl.BlockSpec((tm, LANES), lambda i: (rope_row_of(i, tm), 0)),
                  pl.BlockSpec((tm, LANES), lambda i: (rope_row_of(i, tm), 0)),
                  pl.BlockSpec((1, MLA_Q_LORA), lambda i: (0, 0)),
                  pl.BlockSpec((1, MLA_KV_LORA), lambda i: (0, 0)),
                  pl.BlockSpec((MLA_Q_LORA, kw), lambda i: (0, 0)),
                  pl.BlockSpec((MLA_KV_LORA, kw + vw), lambda i: (0, 0))],
        out_specs=(pl.BlockSpec((tm, kw), lambda i: (i, 0)),
                   pl.BlockSpec((tm, MLA_KV_LORA), lambda i: (i, 0)),
                   pl.BlockSpec((tm, kw), lambda i: (i, 0)),
                   pl.BlockSpec((tm, vw), lambda i: (i, 0))),
        compiler_params=_cp(("parallel",), 40),
        name="mla_prep",
    )(z, z, z, cos_m, sin_m, gq, gkv, wuq, wkv)


def _mla_cache_kernel(ckv_ref, kr_ref, wkv_ref, k_ref, v_ref):
    kv = jnp.dot(ckv_ref[...].astype(BF16), wkv_ref[...], preferred_element_type=F32)
    kr = kr_ref[...]
    for h in range(MLA_HEADS):
        sl = slice(h * LANES, (h + 1) * LANES)
        k_ref[:, sl] = (kv[:, sl] + kr).astype(BF16)
    v_ref[...] = kv[:, MLA_HEADS * LANES:].astype(BF16)


def _mla_cache(cache_ckv, cache_kr_pad, wkv, layer):
    db, _, past, _ = cache_ckv.shape
    kw = MLA_HEADS * LANES
    vw = MLA_HEADS * MLA_V
    return pl.pallas_call(
        _mla_cache_kernel,
        out_shape=(jax.ShapeDtypeStruct((db * past, kw), BF16), jax.ShapeDtypeStruct((db * past, vw), BF16)),
        grid=(db,),
        in_specs=[pl.BlockSpec((None, None, past, MLA_KV_LORA), lambda b: (b, layer, 0, 0)),
                  pl.BlockSpec((None, None, past, LANES), lambda b: (b, layer, 0, 0)),
                  pl.BlockSpec((MLA_KV_LORA, kw + vw), lambda b: (0, 0))],
        out_specs=(pl.BlockSpec((past, kw), lambda b: (b, 0)), pl.BlockSpec((past, vw), lambda b: (b, 0))),
        compiler_params=_cp(("parallel",), 40),
        name="mla_cache",
    )(cache_ckv, cache_kr_pad, wkv)


def _mla_ctx_kernel(q_ref, k_ref, v_ref, o_ref, *, seq):
    lo = _lane_lo()
    for g in range(CTX_GROUP):
        rows = slice(g * seq, (g + 1) * seq)
        v = v_ref[rows, :]
        outs = []
        for hh in range(2):
            sl = slice(hh * LANES, (hh + 1) * LANES)
            ps, l = _softmax2_parts([_scores(q_ref[rows, sl], k_ref[rows, sl])])
            outs.append(_pv(ps, [v]) / l)
        o_ref[rows, :] = jnp.where(lo, outs[0], outs[1]).astype(o_ref.dtype)


def _mla_attn_ctx(q, k, v, out, n_seq, seq):
    pairs = MLA_HEADS // 2
    rows = CTX_GROUP * seq
    return pl.pallas_call(
        functools.partial(_mla_ctx_kernel, seq=seq),
        out_shape=out,
        grid=(n_seq // CTX_GROUP, pairs),
        in_specs=[pl.BlockSpec((rows, 2 * LANES), lambda s, p: (s, p)),
                  pl.BlockSpec((rows, 2 * LANES), lambda s, p: (s, p)),
                  pl.BlockSpec((rows, LANES), lambda s, p: (s, p))],
        out_specs=pl.BlockSpec((rows, LANES), lambda s, p: (s, p)),
        compiler_params=_cp(("parallel", "parallel"), 40),
        name="mla_attn_ctx",
    )(q, k, v)


def _mla_lat_kernel(q_ref, kl_ref, kc_ref, vl_ref, vc_ref, prev_ref, o_ref, s_ref, *, n_sub):
    del prev_ref
    tq = ATT_TQ
    k_refs = (kl_ref, kc_ref)
    v_refs = (vl_ref, vc_ref)
    chunks = _chunks((kl_ref.shape[0], kc_ref.shape[0]), ATT_CK)
    units = [(qs, hh) for qs in range(n_sub) for hh in range(2)]

    def stage_a(unit, slot, chunk, mrun):
        qs, hh = unit
        si, st, n, off = chunk
        sl = slice(hh * LANES, (hh + 1) * LANES)
        s = _scores(q_ref[qs * tq:(qs + 1) * tq, sl], k_refs[si][st:st + n, sl])
        s_ref[slot, :, off:off + n] = s
        return _tile_fold(jnp.maximum, mrun, s)

    def stage_b(slot, chunk, m, lrun, acc):
        si, st, n, off = chunk
        p = jnp.exp2(s_ref[slot, :, off:off + n] - m)
        lrun = _tile_fold(jnp.add, lrun, p)
        acc = acc + jnp.dot(p.astype(BF16), v_refs[si][st:st + n, :], preferred_element_type=F32)
        return lrun, acc

    neg = jnp.full((tq, LANES), -jnp.inf, F32)
    zero = jnp.zeros((tq, LANES), F32)
    mrun = neg
    for chunk in chunks:
        mrun = stage_a(units[0], 0, chunk, mrun)
    outs = {}
    for ui, unit in enumerate(units):
        slot = ui % 2
        m = jnp.max(mrun, axis=-1, keepdims=True)
        lrun, acc, mrun = zero, zero, neg
        for chunk in chunks:
            if ui + 1 < len(units):
                mrun = stage_a(units[ui + 1], 1 - slot, chunk, mrun)
            lrun, acc = stage_b(slot, chunk, m, lrun, acc)
        outs[unit] = acc / jnp.sum(lrun, axis=-1, keepdims=True)
    lo = _lane_lo()
    for qs in range(n_sub):
        o_ref[qs * tq:(qs + 1) * tq, :] = jnp.where(lo, outs[(qs, 0)], outs[(qs, 1)]).astype(o_ref.dtype)


def _mla_attn_lat(q, k, v, kc, vc, prev, t_ctx, db, ds, past):
    pairs = MLA_HEADS // 2
    n_sub = 2
    tqb = n_sub * ATT_TQ
    nq = ds // tqb
    q0 = t_ctx // tqb
    s0 = t_ctx // ds
    return pl.pallas_call(
        functools.partial(_mla_lat_kernel, n_sub=n_sub),
        out_shape=jax.ShapeDtypeStruct(prev.shape, prev.dtype),
        grid=(db, pairs, nq),
        in_specs=[pl.BlockSpec((tqb, 2 * LANES), lambda b, p, i: (q0 + b * nq + i, p)),
                  pl.BlockSpec((ds, 2 * LANES), lambda b, p, i: (s0 + b, p)),
                  pl.BlockSpec((past, 2 * LANES), lambda b, p, i: (b, p)),
                  pl.BlockSpec((ds, LANES), lambda b, p, i: (s0 + b, p)),
                  pl.BlockSpec((past, LANES), lambda b, p, i: (b, p)),
                  pl.BlockSpec(memory_space=pl.ANY)],
        out_specs=pl.BlockSpec((tqb, LANES), lambda b, p, i: (q0 + b * nq + i, p)),
        scratch_shapes=[pltpu.VMEM((2, ATT_TQ, ds + past), F32)],
        input_output_aliases={5: 0},
        compiler_params=_cp(("parallel", "parallel", "arbitrary"), 56),
        name="mla_attn_lat",
    )(q, k, kc, v, vc, prev)


def _attn64_ctx_kernel(q_ref, k_ref, v_ref, o_ref, *, seq):
    lo = _lane_lo()
    for g in range(CTX_GROUP):
        rows = slice(g * seq, (g + 1) * seq)
        q = q_ref[rows, :] * HEAD64_QSCALE
        k = k_ref[rows, :].astype(BF16)
        v = v_ref[rows, :].astype(BF16)
        outs = []
        for hh in range(2):
            qm = jnp.where(lo if hh == 0 else jnp.logical_not(lo), q, 0.0).astype(BF16)
            ps, l = _softmax2_parts([_scores(qm, k)])
            outs.append(_pv(ps, [v]) / l)
        o_ref[rows, :] = jnp.where(lo, outs[0], outs[1]).astype(o_ref.dtype)


def _na_attn_ctx(z, out, n_seq, seq):
    pairs = NA_HEADS // 2
    rows = CTX_GROUP * seq
    return pl.pallas_call(
        functools.partial(_attn64_ctx_kernel, seq=seq),
        out_shape=out,
        grid=(n_seq // CTX_GROUP, pairs),
        in_specs=[pl.BlockSpec((rows, LANES), lambda s, p: (s, C_NQ // LANES + p)),
                  pl.BlockSpec((rows, LANES), lambda s, p: (s, C_NK // LANES + p)),
                  pl.BlockSpec((rows, LANES), lambda s, p: (s, C_NV // LANES + p))],
        out_specs=pl.BlockSpec((rows, LANES), lambda s, p: (s, p)),
        compiler_params=_cp(("parallel", "parallel"), 40),
        name="na_attn_ctx",
    )(z, z, z)


def _na_plan(rows):
    krows = min(NA_KROWS, rows)
    wr = min(NA_WIN_R, rows)
    nblk = rows // NA_QROWS
    kbase = np.zeros((nblk,), np.int32)
    drmaps = np.zeros((nblk, NA_QROWS, krows), np.int32)
    invalid = 2 * NA_WIN_R - 1
    for blk in range(nblk):
        r0 = blk * NA_QROWS
        kb = int(np.clip(r0 - wr // 2, 0, rows - krows))
        kbase[blk] = kb
        for rr in range(NA_QROWS):
            r = r0 + rr
            w0 = int(np.clip(r - wr // 2, 0, rows - wr))
            assert kb <= w0 and w0 + wr <= kb + krows
            for kk in range(krows):
                krow = kb + kk
                drmaps[blk, rr, kk] = (krow - r + NA_WIN_R - 1) if (w0 <= krow < w0 + wr) else invalid
    pats, pat_of = np.unique(drmaps, axis=0, return_inverse=True)
    return kbase, pats, np.asarray(pat_of, np.int32).reshape(nblk), krows


def _na_bias_table(rpb, pats):
    h = rpb.shape[0]
    c = np.arange(GRID_W)[:, None]
    kc = np.arange(GRID_W)[None, :]
    cs = np.clip(c - NA_WIN_C // 2, 0, GRID_W - NA_WIN_C)
    valid = (kc >= cs) & (kc < cs + NA_WIN_C)
    idx = np.clip(kc - c + NA_WIN_C - 1, 0, 2 * NA_WIN_C - 2)
    e = jnp.where(valid[None, None], rpb[:, :, idx].astype(F32) * LOG2E, NEG_INF)
    e = jnp.concatenate([e, jnp.full((h, 1, GRID_W, GRID_W), NEG_INF, F32)], axis=1)
    b = e[:, pats]
    p, qr, kr = pats.shape
    return b.transpose(1, 0, 2, 4, 3, 5).reshape(p, h, qr * GRID_W, kr * GRID_W)


def _na_lat_kernel(pat_ref, kb_ref, q_ref, k_ref, v_ref, kc_ref, vc_ref, bias_ref, prev_ref, o_ref, s_ref, *, nk):
    del pat_ref, prev_ref
    blk = pl.program_id(2)
    kstart = pl.multiple_of(kb_ref[blk] * GRID_W, GRID_W)
    lo = _lane_lo()
    tq = q_ref.shape[0]
    n_units = bias_ref.shape[0]
    chunks = _chunks((nk, kc_ref.shape[0]), ATT_CK)

    def lanes_of(unit):
        return slice((unit // 2) * LANES, (unit // 2 + 1) * LANES)

    def q_of(unit):
        q = q_ref[:, lanes_of(unit)] * HEAD64_QSCALE
        return (jnp.where(lo, q, 0.0) if unit % 2 == 0 else jnp.where(lo, 0.0, q)).astype(BF16)

    def kv_chunk(refs, unit, chunk):
        si, st, n, _ = chunk
        if si == 0:
            return refs[0][pl.ds(kstart + st, n), lanes_of(unit)].astype(BF16)
        return refs[1][st:st + n, lanes_of(unit)].astype(BF16)

    def stage_a(unit, qm, chunk, mrun):
        si, st, n, off = chunk
        s = _scores(qm, kv_chunk((k_ref, kc_ref), unit, chunk))
        if si == 0:
            s = s + bias_ref[unit, :, st:st + n]
        s_ref[unit % 2, :, off:off + n] = s
        return _tile_fold(jnp.maximum, mrun, s)

    def stage_b(unit, chunk, m, lrun, acc):
        _, _, n, off = chunk
        p = jnp.exp2(s_ref[unit % 2, :, off:off + n] - m)
        lrun = _tile_fold(jnp.add, lrun, p)
        acc = acc + jnp.dot(p.astype(BF16), kv_chunk((v_ref, vc_ref), unit, chunk), preferred_element_type=F32)
        return lrun, acc

    neg = jnp.full((tq, LANES), -jnp.inf, F32)
    zero = jnp.zeros((tq, LANES), F32)
    mrun = neg
    qm = q_of(0)
    for chunk in chunks:
        mrun = stage_a(0, qm, chunk, mrun)
    outs = []
    for unit in range(n_units):
        m = jnp.max(mrun, axis=-1, keepdims=True)
        lrun, acc, mrun = zero, zero, neg
        if unit + 1 < n_units:
            qm = q_of(unit + 1)
        for chunk in chunks:
            if unit + 1 < n_units:
                mrun = stage_a(unit + 1, qm, chunk, mrun)
            lrun, acc = stage_b(unit, chunk, m, lrun, acc)
        outs.append(acc / jnp.sum(lrun, axis=-1, keepdims=True))
    for hq in range(n_units // 2):
        o_ref[:, hq * LANES:(hq + 1) * LANES] = jnp.where(lo, outs[2 * hq], outs[2 * hq + 1]).astype(o_ref.dtype)


def _na_attn_lat(z, cache_k, cache_v, bias, plan, prev, layer, t_ctx, db, ds, past):
    kbase, _, pat_of, krows = plan
    hs = NA_STEP_HEADS
    w = hs * NA_HD
    tq = NA_QROWS * GRID_W
    nk = krows * GRID_W
    nblk = ds // tq
    q0 = t_ctx // tq
    s0 = t_ctx // ds
    grid_spec = pltpu.PrefetchScalarGridSpec(
        num_scalar_prefetch=2,
        grid=(db, NA_HEADS // hs, nblk),
        in_specs=[pl.BlockSpec((tq, w), lambda b, p, i, pat, kb: (q0 + b * nblk + i, C_NQ // w + p)),
                  pl.BlockSpec((ds, w), lambda b, p, i, pat, kb: (s0 + b, C_NK // w + p)),
                  pl.BlockSpec((ds, w), lambda b, p, i, pat, kb: (s0 + b, C_NV // w + p)),
                  pl.BlockSpec((None, None, past, w), lambda b, p, i, pat, kb: (b, layer, 0, p)),
                  pl.BlockSpec((None, None, past, w), lambda b, p, i, pat, kb: (b, layer, 0, p)),
                  pl.BlockSpec((None, hs, tq, nk), lambda b, p, i, pat, kb: (pat[i], p, 0, 0)),
                  pl.BlockSpec(memory_space=pl.ANY)],
        out_specs=pl.BlockSpec((tq, w), lambda b, p, i, pat, kb: (q0 + b * nblk + i, p)),
        scratch_shapes=[pltpu.VMEM((2, tq, nk + past), F32)],
    )
    return pl.pallas_call(
        functools.partial(_na_lat_kernel, nk=nk),
        out_shape=jax.ShapeDtypeStruct(prev.shape, prev.dtype),
        grid_spec=grid_spec,
        input_output_aliases={8: 0},
        compiler_params=_cp(("parallel", "parallel", "arbitrary"), 56),
        name="na_attn_lat",
    )(jnp.asarray(pat_of), jnp.asarray(kbase), z, z, z, cache_k, cache_v, bias, prev)


def _pool_kernel(*refs, t):
    u_ref, w_ref, sc_ref = refs[:3]
    o_ref, pad_ref = refs[-2:]
    halo = 8
    pos = lax.broadcasted_iota(jnp.int32, (t, POOL_GROUP), 0)
    zeros = jnp.zeros((halo, POOL_GROUP), F32)
    for gi, w in enumerate(POOL_WINDOWS):
        sl = slice(gi * POOL_GROUP, (gi + 1) * POOL_GROUP)
        u = u_ref[:, sl]
        pad_ref[0:halo, :] = zeros
        pad_ref[halo + t:2 * halo + t, :] = zeros
        pad_ref[halo:halo + t, :] = u
        tot = None
        for d in range(-(w // 2), w // 2):
            part = pad_ref[halo + d:halo + d + t, :]
            tot = part if tot is None else tot + part
        cnt = (jnp.minimum(pos + w // 2, t) - jnp.maximum(pos - w // 2, 0)).astype(F32)
        pooled = (tot / cnt - u).astype(BF16)
        mixed = jnp.dot(pooled, w_ref[gi], preferred_element_type=F32)
        o_ref[:, sl] = (mixed * sc_ref[:, sl]).astype(o_ref.dtype)


def _pool(z, pool_w, pool_scale, out, prev, n_seq, seq, row0, name):
    width = len(POOL_WINDOWS) * POOL_GROUP
    aliased = prev is not None
    in_specs = [pl.BlockSpec((seq, width), lambda s: (row0 + s, C_POOL // width)),
                pl.BlockSpec((len(POOL_WINDOWS), POOL_GROUP, POOL_GROUP), lambda s: (0, 0, 0)),
                pl.BlockSpec((1, width), lambda s: (0, 0))]
    args = [z, pool_w, pool_scale]
    if aliased:
        in_specs.append(pl.BlockSpec(memory_space=pl.ANY))
        args.append(prev)
        out = jax.ShapeDtypeStruct(prev.shape, prev.dtype)
    return pl.pallas_call(
        functools.partial(_pool_kernel, t=seq),
        out_shape=out,
        grid=(n_seq,),
        in_specs=in_specs,
        out_specs=pl.BlockSpec((seq, width), lambda s: (row0 + s, 0)),
        scratch_shapes=[pltpu.VMEM((seq + 16, POOL_GROUP), F32)],
        input_output_aliases={3: 0} if aliased else {},
        compiler_params=_cp(("parallel",), 56),
        name=name,
    )(*args)


def _diff_prep_kernel(q_ref, k_ref, v_ref, cos_ref, sin_ref, qo_ref, ko_ref, vo_ref):
    cos = cos_ref[...]
    sin = sin_ref[...]
    lane = lax.broadcasted_iota(jnp.int32, (1, LANES), 1)
    first = (lane % DIFF_HD) < DIFF_HD // 2
    for h in range(DIFF_HEADS):
        sl = slice(h * LANES, (h + 1) * LANES)
        qo_ref[:, sl] = (_rope_block(q_ref[:, sl], cos, sin, first, DIFF_HD // 2) * HEAD64_QSCALE).astype(BF16)
        ko_ref[:, sl] = _rope_block(k_ref[:, sl], cos, sin, first, DIFF_HD // 2).astype(BF16)
    vo_ref[...] = v_ref[...].astype(BF16)


def _diff_prep(z, cos_d, sin_d, rope_row_of):
    t = z.shape[0]
    tm = 512
    w = DIFF_HEADS * LANES
    sds = jax.ShapeDtypeStruct((t, w), BF16)
    return pl.pallas_call(
        _diff_prep_kernel,
        out_shape=(sds, sds, sds),
        grid=(t // tm,),
        in_specs=[pl.BlockSpec((tm, w), lambda i: (i, C_DQ // w)),
                  pl.BlockSpec((tm, w), lambda i: (i, C_DK // w)),
                  pl.BlockSpec((tm, w), lambda i: (i, C_DV // w)),
                  pl.BlockSpec((tm, LANES), lambda i: (rope_row_of(i, tm), 0)),
                  pl.BlockSpec((tm, LANES), lambda i: (rope_row_of(i, tm), 0))],
        out_specs=(pl.BlockSpec((tm, w), lambda i: (i, 0)),) * 3,
        compiler_params=_cp(("parallel",), 40),
        name="diff_prep",
    )(z, z, z, cos_d, sin_d)


def _diff_lambda(linit_ref, lam_ref):
    lam_init = linit_ref[0]
    lp = lam_ref[...]
    lam = (jnp.exp(jnp.sum(lp[0:1] * lp[1:2], axis=-1, keepdims=True))
           - jnp.exp(jnp.sum(lp[2:3] * lp[3:4], axis=-1, keepdims=True)) + lam_init)
    return lam, lam_init


def _diff_ctx_kernel(linit_ref, lam_ref, g_ref, q_ref, k_ref, v_ref, o_ref, *, seq):
    lam, lam_init = _diff_lambda(linit_ref, lam_ref)
    lo = _lane_lo()
    for g in range(CTX_GROUP):
        rows = slice(g * seq, (g + 1) * seq)
        q = q_ref[rows, :]
        k = k_ref[rows, :]
        zero = jnp.zeros_like(q)
        p1, l1 = _softmax2_parts([_scores(jnp.where(lo, q, zero), k)])
        p2, l2 = _softmax2_parts([_scores(jnp.where(lo, zero, q), k)])
        o = _pv([p1[0] * (1.0 / l1) - p2[0] * (lam / l2)], [v_ref[rows, :]])
        o_ref[rows, :] = (_rms(o, g_ref[...]) * (1.0 - lam_init)).astype(o_ref.dtype)


def _diff_attn_ctx(linit, lam_p, g, q, k, v, out, layer, n_seq, seq):
    smem = pl.BlockSpec(memory_space=pltpu.SMEM)
    rows = CTX_GROUP * seq
    return pl.pallas_call(
        functools.partial(_diff_ctx_kernel, seq=seq),
        out_shape=out,
        grid=(n_seq // CTX_GROUP, DIFF_HEADS),
        in_specs=[smem,
                  pl.BlockSpec((None, 4, DIFF_HD), lambda s, h: (layer, 0, 0)),
                  pl.BlockSpec((1, LANES), lambda s, h: (0, 0)),
                  pl.BlockSpec((rows, LANES), lambda s, h: (s, h)),
                  pl.BlockSpec((rows, LANES), lambda s, h: (s, h)),
                  pl.BlockSpec((rows, LANES), lambda s, h: (s, h))],
        out_specs=pl.BlockSpec((rows, LANES), lambda s, h: (s, h)),
        compiler_params=_cp(("parallel", "parallel"), 40),
        name="diff_attn_ctx",
    )(linit, lam_p, g, q, k, v)


def _diff_lat_kernel(linit_ref, lam_ref, g_ref, q_ref, kl_ref, kc_ref, vl_ref, vc_ref, prev_ref, o_ref, s_ref,
                     *, n_sub):
    del prev_ref
    tq = ATT_TQ
    lam, lam_init = _diff_lambda(linit_ref, lam_ref)
    lo = _lane_lo()
    k_refs = (kl_ref, kc_ref)
    v_refs = (vl_ref, vc_ref)
    chunks = _chunks((kl_ref.shape[0], kc_ref.shape[0]), ATT_CK)

    def q_maps(qs):
        q = q_ref[qs * tq:(qs + 1) * tq, :]
        zero = jnp.zeros_like(q)
        return jnp.where(lo, q, zero), jnp.where(lo, zero, q)

    def stage_a(qms, slot, chunk, mruns):
        si, st, n, off = chunk
        k = k_refs[si][st:st + n, :].astype(BF16)
        out = []
        for mi in range(2):
            s = _scores(qms[mi], k)
            s_ref[slot, mi, :, off:off + n] = s
            out.append(_tile_fold(jnp.maximum, mruns[mi], s))
        return out

    def stage_b1(slot, chunk, ms, lruns):
        _, _, n, off = chunk
        out = []
        for mi in range(2):
            e = jnp.exp2(s_ref[slot, mi, :, off:off + n] - ms[mi])
            s_ref[slot, mi, :, off:off + n] = e
            out.append(_tile_fold(jnp.add, lruns[mi], e))
        return out

    def stage_b2(slot, chunk, rho, acc):
        si, st, n, off = chunk
        p = s_ref[slot, 0, :, off:off + n] - s_ref[slot, 1, :, off:off + n] * rho
        return acc + jnp.dot(p.astype(BF16), v_refs[si][st:st + n, :].astype(BF16), preferred_element_type=F32)

    neg = jnp.full((tq, LANES), -jnp.inf, F32)
    zero = jnp.zeros((tq, LANES), F32)
    mruns = [neg, neg]
    qms = q_maps(0)
    for chunk in chunks:
        mruns = stage_a(qms, 0, chunk, mruns)
    for qs in range(n_sub):
        slot = qs % 2
        ms = [jnp.max(mr, axis=-1, keepdims=True) for mr in mruns]
        lruns, mruns = [zero, zero], [neg, neg]
        if qs + 1 < n_sub:
            qms = q_maps(qs + 1)
        for chunk in chunks:
            if qs + 1 < n_sub:
                mruns = stage_a(qms, 1 - slot, chunk, mruns)
            lruns = stage_b1(slot, chunk, ms, lruns)
        l1 = jnp.sum(lruns[0], axis=-1, keepdims=True)
        rho = lam * l1 / jnp.sum(lruns[1], axis=-1, keepdims=True)
        acc = zero
        for chunk in chunks:
            acc = stage_b2(slot, chunk, rho, acc)
        o_ref[qs * tq:(qs + 1) * tq, :] = (_rms(acc / l1, g_ref[...]) * (1.0 - lam_init)).astype(o_ref.dtype)


def _diff_attn_lat(linit, lam_p, g, q, k, v, cache_k, cache_v, prev, layer, t_ctx, db, ds, past):
    smem = pl.BlockSpec(memory_space=pltpu.SMEM)
    n_sub = 4
    tqb = n_sub * ATT_TQ
    nq = ds // tqb
    q0 = t_ctx // tqb
    s0 = t_ctx // ds
    return pl.pallas_call(
        functools.partial(_diff_lat_kernel, n_sub=n_sub),
        out_shape=jax.ShapeDtypeStruct(prev.shape, prev.dtype),
        grid=(db, DIFF_HEADS, nq),
        in_specs=[smem,
                  pl.BlockSpec((None, 4, DIFF_HD), lambda b, h, i: (layer, 0, 0)),
                  pl.BlockSpec((1, LANES), lambda b, h, i: (0, 0)),
                  pl.BlockSpec((tqb, LANES), lambda b, h, i: (q0 + b * nq + i, h)),
                  pl.BlockSpec((ds, LANES), lambda b, h, i: (s0 + b, h)),
                  pl.BlockSpec((None, None, past, LANES), lambda b, h, i: (b, layer, 0, h)),
                  pl.BlockSpec((ds, LANES), lambda b, h, i: (s0 + b, h)),
                  pl.BlockSpec((None, None, past, LANES), lambda b, h, i: (b, layer, 0, h)),
                  pl.BlockSpec(memory_space=pl.ANY)],
        out_specs=pl.BlockSpec((tqb, LANES), lambda b, h, i: (q0 + b * nq + i, h)),
        scratch_shapes=[pltpu.VMEM((2, 2, ATT_TQ, ds + past), F32)],
        input_output_aliases={8: 0},
        compiler_params=_cp(("parallel", "parallel", "arbitrary"), 56),
        name="diff_attn_lat",
    )(linit, lam_p, g, q, k, cache_k, v, cache_v, prev)


def _merge_out_kernel(h_ref, wg_ref, oa_ref, ob_ref, oc_ref, od_ref, wbr_ref, wo_ref, x_ref, mod_ref, gn_ref,
                      x1_ref, h2_ref, *, d):
    n = pl.program_id(1)
    tn = MERGE_TN

    @pl.when(n == 0)
    def _():
        x1_ref[...] = jnp.zeros_like(x1_ref)

    gates = jnp.dot(h_ref[...], wg_ref[...], preferred_element_type=F32)
    merged = None
    for bi, o_ref in enumerate((oa_ref, ob_ref, oc_ref, od_ref)):
        term = (jax.nn.sigmoid(gates[:, bi * tn:(bi + 1) * tn])
                * jnp.dot(o_ref[...], wbr_ref[bi], preferred_element_type=F32))
        merged = term if merged is None else merged + term
    x1_ref[...] += jnp.dot(merged.astype(BF16), wo_ref[...], preferred_element_type=F32)

    @pl.when(n == pl.num_programs(1) - 1)
    def _():
        mod = mod_ref[0]
        g1 = mod[:, 2 * d:3 * d]
        sh2 = mod[:, 3 * d:4 * d]
        sc2 = mod[:, 4 * d:5 * d]
        x1 = x_ref[...] + g1 * _rms(x1_ref[...], gn_ref[1:2, :])
        x1_ref[...] = x1
        h2_ref[...] = (_rms(x1, gn_ref[2:3, :]) * (1.0 + sc2) + sh2).astype(BF16)


def _merge_out(h, w_gates, outs, w_br, w_o, x, mod_l, gn, row_of):
    t, d = x.shape
    bw = outs[0].shape[1]
    tm, tn = 512, MERGE_TN
    row_spec = pl.BlockSpec((tm, d), lambda i, n: (i, 0))
    o_spec = pl.BlockSpec((tm, bw), lambda i, n: (i, 0))
    return pl.pallas_call(
        functools.partial(_merge_out_kernel, d=d),
        out_shape=(jax.ShapeDtypeStruct((t, d), F32), jax.ShapeDtypeStruct((t, d), BF16)),
        grid=(t // tm, d // tn),
        in_specs=[row_spec,
                  pl.BlockSpec((d, 4 * tn), lambda i, n: (0, n)),
                  o_spec, o_spec, o_spec, o_spec,
                  pl.BlockSpec((4, bw, tn), lambda i, n: (0, 0, n)),
                  pl.BlockSpec((tn, d), lambda i, n: (n, 0)),
                  row_spec,
                  pl.BlockSpec((1, 1, 6 * d), lambda i, n: (row_of(i, tm), 0, 0)),
                  pl.BlockSpec((4, d), lambda i, n: (0, 0))],
        out_specs=(row_spec, row_spec),
        compiler_params=_cp(("parallel", "arbitrary"), 56),
        name="merge_out",
    )(h, w_gates, *outs, w_br, w_o, x, mod_l, gn)


def _ffn_kernel(*refs, d, emit_next):
    h_ref, wu_ref, wd_ref, x_ref, mod_ref, gn_ref = refs[:6]
    o_ref = refs[8] if emit_next else refs[6]
    f = pl.program_id(1)

    @pl.when(f == 0)
    def _():
        o_ref[...] = jnp.zeros_like(o_ref)

    u = jnp.dot(h_ref[...], wu_ref[...], preferred_element_type=F32)
    u = jnp.square(jnp.maximum(u, 0.0)).astype(BF16)
    o_ref[...] += jnp.dot(u, wd_ref[...], preferred_element_type=F32)

    @pl.when(f == pl.num_programs(1) - 1)
    def _():
        g2 = mod_ref[0][:, 5 * d:6 * d]
        x_new = x_ref[...] + g2 * _rms(o_ref[...], gn_ref[3:4, :])
        o_ref[...] = x_new
        if emit_next:
            modn_ref, gnn_ref, hn_ref = refs[6], refs[7], refs[9]
            modn = modn_ref[0]
            hn_ref[...] = (_rms(x_new, gnn_ref[0:1, :]) * (1.0 + modn[:, d:2 * d]) + modn[:, 0:d]).astype(BF16)


def _ffn(h2, w_up, w_down, x1, mod_l, gn, row_of, mod_next=None, gn_next=None, row0=0, n_rows=None):
    t, d = x1.shape
    n_rows = t if n_rows is None else n_rows
    dff = w_up.shape[1]
    tm, tf = 512, 1024
    i0 = row0 // tm
    emit_next = mod_next is not None
    mod_spec = pl.BlockSpec((1, 1, 6 * d), lambda i, f: (row_of(i0 + i, tm), 0, 0))
    gn_spec = pl.BlockSpec((4, d), lambda i, f: (0, 0))
    in_row_spec = pl.BlockSpec((tm, d), lambda i, f: (i0 + i, 0))
    out_row_spec = pl.BlockSpec((tm, d), lambda i, f: (i, 0))
    in_specs = [in_row_spec,
                pl.BlockSpec((d, tf), lambda i, f: (0, f)),
                pl.BlockSpec((tf, d), lambda i, f: (f, 0)),
                in_row_spec, mod_spec, gn_spec]
    args = [h2, w_up, w_down, x1, mod_l, gn]
    out_shape = jax.ShapeDtypeStruct((n_rows, d), F32)
    out_specs = out_row_spec
    if emit_next:
        in_specs += [mod_spec, gn_spec]
        args += [mod_next, gn_next]
        out_shape = (out_shape, jax.ShapeDtypeStruct((n_rows, d), BF16))
        out_specs = (out_row_spec, out_row_spec)
    return pl.pallas_call(
        functools.partial(_ffn_kernel, d=d, emit_next=emit_next),
        out_shape=out_shape,
        grid=(n_rows // tm, dff // tf),
        in_specs=in_specs,
        out_specs=out_specs,
        compiler_params=_cp(("parallel", "arbitrary"), 58),
        name="ffn",
    )(*args)


def _prep_weights(w_in, w_uq, w_ukv):
    depth, d, _ = w_in.shape
    sizes = (MLA_Q_LORA, MLA_KV_LORA, MLA_ROPE, 512, 512, 512, 512, 512, 512, 512, 4 * d)
    offs = np.cumsum((0,) + sizes)
    part = lambda i: w_in[:, :, offs[i]:offs[i + 1]]
    q_c, kv_c, k_r, na_q, na_k, na_v, pool, dq, dk, dv, gates = (part(i) for i in range(11))
    kr_blk = jnp.pad(k_r, ((0, 0), (0, 0), (MLA_NOPE, LANES - MLA_NOPE - MLA_ROPE)))
    w_all = jnp.concatenate([q_c, dq, dk, dv, na_q, na_k, na_v, pool, kv_c, kr_blk], axis=-1).astype(BF16)
    w_gates = gates.reshape(depth, d, 4, d // MERGE_TN, MERGE_TN).transpose(0, 1, 3, 2, 4)
    w_gates = w_gates.reshape(depth, d, 4 * d).astype(BF16)
    hd = MLA_NOPE + MLA_ROPE
    wuq = jnp.pad(w_uq.reshape(depth, MLA_Q_LORA, MLA_HEADS, hd), ((0, 0), (0, 0), (0, 0), (0, LANES - hd)))
    wuq = wuq.reshape(depth, MLA_Q_LORA, MLA_HEADS * LANES).astype(BF16)
    wkv4 = w_ukv.reshape(depth, MLA_KV_LORA, MLA_HEADS, MLA_NOPE + MLA_V)
    wk = jnp.pad(wkv4[..., :MLA_NOPE], ((0, 0), (0, 0), (0, 0), (0, LANES - MLA_NOPE)))
    wk = wk.reshape(depth, MLA_KV_LORA, MLA_HEADS * LANES)
    wv = wkv4[..., MLA_NOPE:].reshape(depth, MLA_KV_LORA, MLA_HEADS * MLA_V)
    wkv = jnp.concatenate([wk, wv], axis=-1).astype(BF16)
    return w_all, w_gates, wuq, wkv


def kernel(x_prompt, x_sample, cache_mla_ckv, cache_mla_krope, cache_na_k, cache_na_v, cache_diff_k, cache_diff_v, c, c_ctx, w_mod, b_mod, g_norm, w_in, g_q_lora, g_kv_lora, w_uq, w_ukv, na_rpb, pool_w, pool_scale, diff_lambda, diff_norm_g, w_br, w_o, w_up, w_down):
    nb, seq, d = x_prompt.shape
    db, ds, _ = x_sample.shape
    depth = w_in.shape[0]
    past = cache_mla_ckv.shape[2]
    t_ctx = nb * seq
    t_lat = db * ds
    t = t_ctx + t_lat
    assert t_ctx % ds == 0 and ds % 1024 == 0 and t_ctx % 1024 == 0 and seq % 8 == 0 and nb % CTX_GROUP == 0

    def row_of(i, tm):
        n_ctx = t_ctx // tm
        return jnp.where(i < n_ctx, 0, 1 + (i - n_ctx) // (ds // tm))

    def rope_row_of(i, tm):
        n_ctx = t_ctx // tm
        return jnp.where(i < n_ctx, 0, 1 + (i - n_ctx) % (ds // tm))

    w_all, w_gates, wuq, wkv = _prep_weights(w_in, w_uq, w_ukv)
    w_br_b = w_br.astype(BF16)
    w_o_b = w_o.astype(BF16)
    w_up_b = w_up.astype(BF16)
    w_down_b = w_down.astype(BF16)
    pool_w_b = pool_w.astype(BF16)
    cos_m, sin_m, cos_d, sin_d = _rope_tables(ds, 512)
    rows = ds // GRID_W
    na_plan = _na_plan(rows)
    cache_kr_pad = jnp.pad(cache_mla_krope, ((0, 0), (0, 0), (0, 0), (MLA_NOPE, LANES - MLA_NOPE - MLA_ROPE)))
    cache_nk = cache_na_k.reshape(db, depth, past, NA_HEADS * NA_HD)
    cache_nv = cache_na_v.reshape(db, depth, past, NA_HEADS * NA_HD)
    cache_dk = cache_diff_k.reshape(db, depth, past, DIFF_HEADS * 2 * DIFF_HD)
    cache_dv = cache_diff_v.reshape(db, depth, past, DIFF_HEADS * 2 * DIFF_HD)

    n_rows = 1 + db
    r_pad = -(-n_rows // 8) * 8
    cond = jnp.concatenate([c_ctx[None, :], c, jnp.zeros((r_pad - n_rows, d), F32)], axis=0)
    mod = _modulation(cond, w_mod, b_mod)
    mods = [mod[li].reshape(r_pad, 1, 6 * d) for li in range(depth)]

    branch_sds = jax.ShapeDtypeStruct((t, 512), BF16)
    x, h = _norm_mod(x_prompt.reshape(t_ctx, d), x_sample.reshape(t_lat, d), mods[0], g_norm[0][0:1], row_of)
    states = []
    for li in range(depth):
        mod_l = mods[li]
        gn = g_norm[li]
        z = _matmul(h, w_all[li], 1024, 896, F32, "in_proj")

        q_a, ckv, k_a, v_a = _mla_prep(z, cos_m, sin_m, g_q_lora[li][None, :], g_kv_lora[li][None, :],
                                       wuq[li], wkv[li], rope_row_of)
        kc_a, vc_a = _mla_cache(cache_mla_ckv, cache_kr_pad, wkv[li], li)
        o_a = _mla_attn_ctx(q_a, k_a, v_a, branch_sds, nb, seq)
        o_a = _mla_attn_lat(q_a, k_a, v_a, kc_a, vc_a, o_a, t_ctx, db, ds, past)

        bias = _na_bias_table(na_rpb[li], na_plan[1])
        o_b = _na_attn_ctx(z, branch_sds, nb, seq)
        o_b = _na_attn_lat(z, cache_nk, cache_nv, bias, na_plan, o_b, li, t_ctx, db, ds, past)

        o_c = _pool(z, pool_w_b[li], pool_scale[li][None, :], branch_sds, None, nb, seq, 0, "pool_ctx")
        o_c = _pool(z, pool_w_b[li], pool_scale[li][None, :], None, o_c, db, ds, t_ctx // ds, "pool_lat")

        q_d, k_d, v_d = _diff_prep(z, cos_d, sin_d, rope_row_of)
        linit = jnp.full((1,), 0.8 - 0.6 * math.exp(-0.3 * li), F32)
        g_d = diff_norm_g[li][None, :]
        o_d = _diff_attn_ctx(linit, diff_lambda, g_d, q_d, k_d, v_d, branch_sds, li, nb, seq)
        o_d = _diff_attn_lat(linit, diff_lambda, g_d, q_d, k_d, v_d, cache_dk, cache_dv, o_d,
                             li, t_ctx, db, ds, past)

        x1, h2 = _merge_out(h, w_gates[li], (o_a, o_b, o_c, o_d), w_br_b[li], w_o_b[li], x, mod_l, gn, row_of)
        if li + 1 < depth:
            x, h = _ffn(h2, w_up_b[li], w_down_b[li], x1, mod_l, gn, row_of, mods[li + 1], g_norm[li + 1])
        else:
            y_prompt = _ffn(h2, w_up_b[li], w_down_b[li], x1, mod_l, gn, row_of, row0=0, n_rows=t_ctx)
            y_sample = _ffn(h2, w_up_b[li], w_down_b[li], x1, mod_l, gn, row_of, row0=t_ctx, n_rows=t_lat)

        zc = z[:t_ctx]
        states.append((ckv[:t_ctx], zc[:, C_KR + MLA_NOPE:C_KR + MLA_NOPE + MLA_ROPE], zc[:, C_NK:C_NK + 512],
                       zc[:, C_NV:C_NV + 512], zc[:, C_DK:C_DK + 512], zc[:, C_DV:C_DV + 512]))

    def stacked(k, tail):
        per_layer = [s[k].reshape(nb, seq, -1) for s in states]
        return jnp.stack(per_layer, axis=1).reshape(nb, depth, seq, *tail)

    return (y_prompt.reshape(nb, seq, d), y_sample.reshape(db, ds, d),
            stacked(0, (MLA_KV_LORA,)), stacked(1, (MLA_ROPE,)),
            stacked(2, (NA_HEADS, NA_HD)), stacked(3, (NA_HEADS, NA_HD)),
            stacked(4, (DIFF_HEADS, 2 * DIFF_HD)), stacked(5, (DIFF_HEADS, 2 * DIFF_HD)))
```

```python
import functools
import math

import numpy as np
import jax
import jax.numpy as jnp
from jax import lax
from jax.experimental import pallas as pl
from jax.experimental.pallas import tpu as pltpu

F32 = jnp.float32
BF16 = jnp.bfloat16

GRID_W = 64
MLA_HEADS = 8
MLA_NOPE = 64
MLA_ROPE = 32
MLA_V = 64
MLA_Q_LORA = 512
MLA_KV_LORA = 256
NA_HEADS = 8
NA_HD = 64
NA_WIN_R = 8
NA_WIN_C = 16
POOL_WINDOWS = (2, 4, 8, 16)
POOL_GROUP = 128
DIFF_HEADS = 4
DIFF_HD = 64
ROPE_THETA = 10000.0
RMS_EPS = 1e-6
NEG_INF = -1e30
LOG2E = math.log2(math.e)
MLA_QSCALE = (MLA_NOPE + MLA_ROPE) ** -0.5 * LOG2E
HEAD64_QSCALE = 64 ** -0.5 * LOG2E
LANES = 128
MIB = 1024 * 1024

C_QC = 0
C_DQ = 512
C_DK = 1024
C_DV = 1536
C_NQ = 2048
C_NK = 2560
C_NV = 3072
C_POOL = 3584
C_KVC = 4096
C_KR = 4352
Z_COLS = 4480

MERGE_TN = 256
NA_STEP_HEADS = 4
NA_QROWS = 8
NA_KROWS = 16
ATT_TQ = 256
ATT_CK = 512
CTX_GROUP = 4
PREP_TM = 1024


def _cp(sem, vmem_mib):
    return pltpu.CompilerParams(dimension_semantics=sem, vmem_limit_bytes=vmem_mib * MIB)


def _rms(x, g):
    return x * lax.rsqrt(jnp.mean(x * x, axis=-1, keepdims=True) + RMS_EPS) * g


def _scores(q, k):
    return lax.dot_general(q, k, (((1,), (1,)), ((), ())), preferred_element_type=F32)


def _softmax2_parts(ss):
    m = functools.reduce(jnp.maximum, [jnp.max(s, axis=-1, keepdims=True) for s in ss])
    ps = [jnp.exp2(s - m) for s in ss]
    l = functools.reduce(jnp.add, [jnp.sum(p, axis=-1, keepdims=True) for p in ps])
    return ps, l


def _pv(ps, vs):
    acc = None
    for p, v in zip(ps, vs):
        t = jnp.dot(p.astype(BF16), v, preferred_element_type=F32)
        acc = t if acc is None else acc + t
    return acc


def _lane_lo():
    return lax.broadcasted_iota(jnp.int32, (1, LANES), 1) < 64


def _tile_fold(op, run, x):
    for c in range(x.shape[1] // LANES):
        run = op(run, x[:, c * LANES:(c + 1) * LANES])
    return run


def _chunks(sizes, ck):
    out, off = [], 0
    for si, n in enumerate(sizes):
        step = min(ck, n)
        for st in range(0, n, step):
            out.append((si, st, step, off))
            off += step
    return out


def _mod_kernel(c_ref, w_ref, b_ref, o_ref):
    c = c_ref[...]
    s = c * jax.nn.sigmoid(c)
    o_ref[0] = jnp.dot(s.astype(BF16), w_ref[0].astype(BF16), preferred_element_type=F32) + b_ref[0]


def _modulation(cond, w_mod, b_mod):
    depth, d, n = w_mod.shape
    r = cond.shape[0]
    tn = 1024
    return pl.pallas_call(
        _mod_kernel,
        out_shape=jax.ShapeDtypeStruct((depth, r, n), F32),
        grid=(depth, n // tn),
        in_specs=[pl.BlockSpec((r, d), lambda l, j: (0, 0)),
                  pl.BlockSpec((1, d, tn), lambda l, j: (l, 0, j)),
                  pl.BlockSpec((1, 1, tn), lambda l, j: (l, 0, j))],
        out_specs=pl.BlockSpec((1, r, tn), lambda l, j: (l, 0, j)),
        compiler_params=_cp(("parallel", "parallel"), 40),
        name="modulation",
    )(cond, w_mod, b_mod.reshape(depth, 1, n))


def _norm_mod_kernel(xp_ref, xs_ref, mod_ref, g_ref, x_ref, h_ref, *, d, n_ctx):
    mod = mod_ref[0]
    sh = mod[:, 0:d]
    sc = mod[:, d:2 * d]

    def emit(src_ref):
        x = src_ref[...]
        x_ref[...] = x
        h_ref[...] = (_rms(x, g_ref[...]) * (1.0 + sc) + sh).astype(BF16)

    @pl.when(pl.program_id(0) < n_ctx)
    def _():
        emit(xp_ref)

    @pl.when(pl.program_id(0) >= n_ctx)
    def _():
        emit(xs_ref)


def _norm_mod(xp, xs, mod_l, g, row_of):
    (t_ctx, d), t_lat = xp.shape, xs.shape[0]
    tm = 512
    n_ctx = t_ctx // tm
    t = t_ctx + t_lat
    return pl.pallas_call(
        functools.partial(_norm_mod_kernel, d=d, n_ctx=n_ctx),
        out_shape=(jax.ShapeDtypeStruct((t, d), F32), jax.ShapeDtypeStruct((t, d), BF16)),
        grid=(t // tm,),
        in_specs=[pl.BlockSpec((tm, d), lambda i: (jnp.minimum(i, n_ctx - 1), 0)),
                  pl.BlockSpec((tm, d), lambda i: (jnp.maximum(i - n_ctx, 0), 0)),
                  pl.BlockSpec((1, 1, 6 * d), lambda i: (row_of(i, tm), 0, 0)),
                  pl.BlockSpec((1, d), lambda i: (0, 0))],
        out_specs=(pl.BlockSpec((tm, d), lambda i: (i, 0)), pl.BlockSpec((tm, d), lambda i: (i, 0))),
        compiler_params=_cp(("parallel",), 48),
        name="norm_mod",
    )(xp, xs, mod_l, g)


def _mm_kernel(a_ref, b_ref, o_ref):
    o_ref[...] = jnp.dot(a_ref[...], b_ref[...], preferred_element_type=F32).astype(o_ref.dtype)


def _matmul(a, b, layer, tm, tn, out_dtype, name):
    m, k = a.shape
    n = b.shape[2]
    return pl.pallas_call(
        _mm_kernel,
        out_shape=jax.ShapeDtypeStruct((m, n), out_dtype),
        grid=(m // tm, n // tn),
        in_specs=[pl.BlockSpec((tm, k), lambda i, j: (i, 0)),
                  pl.BlockSpec((None, k, tn), lambda i, j: (layer, 0, j))],
        out_specs=pl.BlockSpec((tm, tn), lambda i, j: (i, j)),
        compiler_params=_cp(("parallel", "arbitrary"), 48),
        name=name,
    )(a, b)


def _rope_block(x, cos, sin, first_half, half):
    partner = jnp.where(first_half, -pltpu.roll(x, LANES - half, 1), pltpu.roll(x, half, 1))
    return x * cos + partner * sin


def _rope_tables(n_lat, ident_rows):
    t = jnp.arange(n_lat)
    row = (t // GRID_W).astype(F32)
    col = (t % GRID_W).astype(F32)

    def angles(rot_dim):
        n_freq = rot_dim // 4
        inv = ROPE_THETA ** (-jnp.arange(n_freq, dtype=F32) / n_freq)
        return jnp.concatenate([row[:, None] * inv, col[:, None] * inv], axis=-1)

    a_m = angles(MLA_ROPE)
    zeros64 = jnp.zeros((n_lat, 64), F32)
    zeros32 = jnp.zeros((n_lat, 32), F32)
    cos_m = jnp.concatenate([zeros64 + 1.0, jnp.cos(a_m), jnp.cos(a_m), zeros32 + 1.0], axis=-1)
    sin_m = jnp.concatenate([zeros64, jnp.sin(a_m), jnp.sin(a_m), zeros32], axis=-1)
    a_d = angles(DIFF_HD)
    cos_d = jnp.tile(jnp.cos(a_d), (1, 4))
    sin_d = jnp.tile(jnp.sin(a_d), (1, 4))
    one = jnp.ones((ident_rows, LANES), F32)
    zero = jnp.zeros((ident_rows, LANES), F32)
    cat = lambda a, b: jnp.concatenate([a, b], axis=0)
    return cat(one, cos_m), cat(zero, sin_m), cat(one, cos_d), cat(zero, sin_d)


def _mla_prep_kernel(qc_ref, kvc_ref, kr_ref, cos_ref, sin_ref, gq_ref, gkv_ref, wuq_ref, wkv_ref,
                     q_ref, ckv_ref, k_ref, v_ref):
    cos = cos_ref[...]
    sin = sin_ref[...]
    lane = lax.broadcasted_iota(jnp.int32, (1, LANES), 1)
    first = lane < MLA_NOPE + MLA_ROPE // 2
    rope = lambda x: _rope_block(x, cos, sin, first, MLA_ROPE // 2)
    qn = _rms(qc_ref[...], gq_ref[...]).astype(BF16)
    q = jnp.dot(qn, wuq_ref[...], preferred_element_type=F32)
    for h in range(MLA_HEADS):
        sl = slice(h * LANES, (h + 1) * LANES)
        q_ref[:, sl] = (rope(q[:, sl]) * MLA_QSCALE).astype(BF16)
    ckv = _rms(kvc_ref[...], gkv_ref[...])
    ckv_ref[...] = ckv
    kv = jnp.dot(ckv.astype(BF16), wkv_ref[...], preferred_element_type=F32)
    krr = rope(kr_ref[...])
    for h in range(MLA_HEADS):
        sl = slice(h * LANES, (h + 1) * LANES)
        k_ref[:, sl] = (kv[:, sl] + krr).astype(BF16)
    v_ref[...] = kv[:, MLA_HEADS * LANES:].astype(BF16)


def _mla_prep(z, cos_m, sin_m, gq, gkv, wuq, wkv, layer, rope_row_of):
    t = z.shape[0]
    tm = PREP_TM
    kw = MLA_HEADS * LANES
    vw = MLA_HEADS * MLA_V
    return pl.pallas_call(
        _mla_prep_kernel,
        out_shape=(jax.ShapeDtypeStruct((t, kw), BF16), jax.ShapeDtypeStruct((t, MLA_KV_LORA), F32),
                   jax.ShapeDtypeStruct((t, kw), BF16), jax.ShapeDtypeStruct((t, vw), BF16)),
        grid=(t // tm,),
        in_specs=[pl.BlockSpec((tm, MLA_Q_LORA), lambda i: (i, C_QC // MLA_Q_LORA)),
                  pl.BlockSpec((tm, MLA_KV_LORA), lambda i: (i, C_KVC // MLA_KV_LORA)),
                  pl.BlockSpec((tm, LANES), lambda i: (i, C_KR // LANES)),
                  pl.BlockSpec((tm, LANES), lambda i: (rope_row_of(i, tm), 0)),
                  pl.BlockSpec((tm, LANES), lambda i: (rope_row_of(i, tm), 0)),
                  pl.BlockSpec((1, MLA_Q_LORA), lambda i: (0, 0)),
                  pl.BlockSpec((1, MLA_KV_LORA), lambda i: (0, 0)),
                  pl.BlockSpec((None, MLA_Q_LORA, kw), lambda i: (layer, 0, 0)),
                  pl.BlockSpec((None, MLA_KV_LORA, kw + vw), lambda i: (layer, 0, 0))],
        out_specs=(pl.BlockSpec((tm, kw), lambda i: (i, 0)),
                   pl.BlockSpec((tm, MLA_KV_LORA), lambda i: (i, 0)),
                   pl.BlockSpec((tm, kw), lambda i: (i, 0)),
                   pl.BlockSpec((tm, vw), lambda i: (i, 0))),
        compiler_params=_cp(("parallel",), 56),
        name="mla_prep",
    )(z, z, z, cos_m, sin_m, gq, gkv, wuq, wkv)


def _mla_cache_kernel(ckv_ref, kr_ref, wkv_ref, k_ref, v_ref):
    kv = jnp.dot(ckv_ref[...].astype(BF16), wkv_ref[...], preferred_element_type=F32)
    kr = kr_ref[...]
    for h in range(MLA_HEADS):
        sl = slice(h * LANES, (h + 1) * LANES)
        k_ref[:, sl] = (kv[:, sl] + kr).astype(BF16)
    v_ref[...] = kv[:, MLA_HEADS * LANES:].astype(BF16)


def _mla_cache(cache_ckv, cache_kr_pad, wkv, layer):
    db, _, past, _ = cache_ckv.shape
    kw = MLA_HEADS * LANES
    vw = MLA_HEADS * MLA_V
    return pl.pallas_call(
        _mla_cache_kernel,
        out_shape=(jax.ShapeDtypeStruct((db * past, kw), BF16), jax.ShapeDtypeStruct((db * past, vw), BF16)),
        grid=(db,),
        in_specs=[pl.BlockSpec((None, None, past, MLA_KV_LORA), lambda b: (b, layer, 0, 0)),
                  pl.BlockSpec((None, None, past, LANES), lambda b: (b, layer, 0, 0)),
                  pl.BlockSpec((None, MLA_KV_LORA, kw + vw), lambda b: (layer, 0, 0))],
        out_specs=(pl.BlockSpec((past, kw), lambda b: (b, 0)), pl.BlockSpec((past, vw), lambda b: (b, 0))),
        compiler_params=_cp(("parallel",), 40),
        name="mla_cache",
    )(cache_ckv, cache_kr_pad, wkv)


def _mla_ctx_kernel(q_ref, k_ref, v_ref, o_ref, *, seq):
    lo = _lane_lo()
    for g in range(CTX_GROUP):
        rows = slice(g * seq, (g + 1) * seq)
        v = v_ref[rows, :]
        outs = []
        for hh in range(2):
            sl = slice(hh * LANES, (hh + 1) * LANES)
            ps, l = _softmax2_parts([_scores(q_ref[rows, sl], k_ref[rows, sl])])
            outs.append(_pv(ps, [v]) / l)
        o_ref[rows, :] = jnp.where(lo, outs[0], outs[1]).astype(o_ref.dtype)


def _mla_attn_ctx(q, k, v, out, n_seq, seq):
    pairs = MLA_HEADS // 2
    rows = CTX_GROUP * seq
    return pl.pallas_call(
        functools.partial(_mla_ctx_kernel, seq=seq),
        out_shape=out,
        grid=(n_seq // CTX_GROUP, pairs),
        in_specs=[pl.BlockSpec((rows, 2 * LANES), lambda s, p: (s, p)),
                  pl.BlockSpec((rows, 2 * LANES), lambda s, p: (s, p)),
                  pl.BlockSpec((rows, LANES), lambda s, p: (s, p))],
        out_specs=pl.BlockSpec((rows, LANES), lambda s, p: (s, p)),
        compiler_params=_cp(("parallel", "parallel"), 40),
        name="mla_attn_ctx",
    )(q, k, v)


def _mla_lat_kernel(q_ref, kl_ref, kc_ref, vl_ref, vc_ref, prev_ref, o_ref, s_ref, *, n_sub):
    del prev_ref
    tq = ATT_TQ
    k_refs = (kl_ref, kc_ref)
    v_refs = (vl_ref, vc_ref)
    chunks = _chunks((kl_ref.shape[0], kc_ref.shape[0]), ATT_CK)
    units = [(qs, hh) for qs in range(n_sub) for hh in range(2)]

    def stage_a(unit, slot, chunk, mrun):
        qs, hh = unit
        si, st, n, off = chunk
        sl = slice(hh * LANES, (hh + 1) * LANES)
        s = _scores(q_ref[qs * tq:(qs + 1) * tq, sl], k_refs[si][st:st + n, sl])
        s_ref[slot, :, off:off + n] = s
        return _tile_fold(jnp.maximum, mrun, s)

    def stage_b(slot, chunk, m, lrun, acc):
        si, st, n, off = chunk
        p = jnp.exp2(s_ref[slot, :, off:off + n] - m)
        lrun = _tile_fold(jnp.add, lrun, p)
        acc = acc + jnp.dot(p.astype(BF16), v_refs[si][st:st + n, :], preferred_element_type=F32)
        return lrun, acc

    neg = jnp.full((tq, LANES), -jnp.inf, F32)
    zero = jnp.zeros((tq, LANES), F32)
    mrun = neg
    for chunk in chunks:
        mrun = stage_a(units[0], 0, chunk, mrun)
    outs = {}
    for ui, unit in enumerate(units):
        slot = ui % 2
        m = jnp.max(mrun, axis=-1, keepdims=True)
        lrun, acc, mrun = zero, zero, neg
        for chunk in chunks:
            if ui + 1 < len(units):
                mrun = stage_a(units[ui + 1], 1 - slot, chunk, mrun)
            lrun, acc = stage_b(slot, chunk, m, lrun, acc)
        outs[unit] = acc / jnp.sum(lrun, axis=-1, keepdims=True)
    lo = _lane_lo()
    for qs in range(n_sub):
        o_ref[qs * tq:(qs + 1) * tq, :] = jnp.where(lo, outs[(qs, 0)], outs[(qs, 1)]).astype(o_ref.dtype)


def _mla_attn_lat(q, k, v, kc, vc, prev, t_ctx, db, ds, past):
    pairs = MLA_HEADS // 2
    n_sub = 2
    tqb = n_sub * ATT_TQ
    nq = ds // tqb
    q0 = t_ctx // tqb
    s0 = t_ctx // ds
    return pl.pallas_call(
        functools.partial(_mla_lat_kernel, n_sub=n_sub),
        out_shape=jax.ShapeDtypeStruct(prev.shape, prev.dtype),
        grid=(db, pairs, nq),
        in_specs=[pl.BlockSpec((tqb, 2 * LANES), lambda b, p, i: (q0 + b * nq + i, p)),
                  pl.BlockSpec((ds, 2 * LANES), lambda b, p, i: (s0 + b, p)),
                  pl.BlockSpec((past, 2 * LANES), lambda b, p, i: (b, p)),
                  pl.BlockSpec((ds, LANES), lambda b, p, i: (s0 + b, p)),
                  pl.BlockSpec((past, LANES), lambda b, p, i: (b, p)),
                  pl.BlockSpec(memory_space=pl.ANY)],
        out_specs=pl.BlockSpec((tqb, LANES), lambda b, p, i: (q0 + b * nq + i, p)),
        scratch_shapes=[pltpu.VMEM((2, ATT_TQ, ds + past), F32)],
        input_output_aliases={5: 0},
        compiler_params=_cp(("parallel", "parallel", "arbitrary"), 56),
        name="mla_attn_lat",
    )(q, k, kc, v, vc, prev)


def _attn64_ctx_kernel(q_ref, k_ref, v_ref, o_ref, *, seq):
    lo = _lane_lo()
    for g in range(CTX_GROUP):
        rows = slice(g * seq, (g + 1) * seq)
        q = q_ref[rows, :] * HEAD64_QSCALE
        k = k_ref[rows, :].astype(BF16)
        v = v_ref[rows, :].astype(BF16)
        outs = []
        for hh in range(2):
            qm = jnp.where(lo if hh == 0 else jnp.logical_not(lo), q, 0.0).astype(BF16)
            ps, l = _softmax2_parts([_scores(qm, k)])
            outs.append(_pv(ps, [v]) / l)
        o_ref[rows, :] = jnp.where(lo, outs[0], outs[1]).astype(o_ref.dtype)


def _na_attn_ctx(z, out, n_seq, seq):
    pairs = NA_HEADS // 2
    rows = CTX_GROUP * seq
    return pl.pallas_call(
        functools.partial(_attn64_ctx_kernel, seq=seq),
        out_shape=out,
        grid=(n_seq // CTX_GROUP, pairs),
        in_specs=[pl.BlockSpec((rows, LANES), lambda s, p: (s, C_NQ // LANES + p)),
                  pl.BlockSpec((rows, LANES), lambda s, p: (s, C_NK // LANES + p)),
                  pl.BlockSpec((rows, LANES), lambda s, p: (s, C_NV // LANES + p))],
        out_specs=pl.BlockSpec((rows, LANES), lambda s, p: (s, p)),
        compiler_params=_cp(("parallel", "parallel"), 40),
        name="na_attn_ctx",
    )(z, z, z)


def _na_plan(rows):
    krows = min(NA_KROWS, rows)
    wr = min(NA_WIN_R, rows)
    nblk = rows // NA_QROWS
    kbase = np.zeros((nblk,), np.int32)
    drmaps = np.zeros((nblk, NA_QROWS, krows), np.int32)
    invalid = 2 * NA_WIN_R - 1
    for blk in range(nblk):
        r0 = blk * NA_QROWS
        kb = int(np.clip(r0 - wr // 2, 0, rows - krows))
        kbase[blk] = kb
        for rr in range(NA_QROWS):
            r = r0 + rr
            w0 = int(np.clip(r - wr // 2, 0, rows - wr))
            assert kb <= w0 and w0 + wr <= kb + krows
            for kk in range(krows):
                krow = kb + kk
                drmaps[blk, rr, kk] = (krow - r + NA_WIN_R - 1) if (w0 <= krow < w0 + wr) else invalid
    pats, pat_of = np.unique(drmaps, axis=0, return_inverse=True)
    return kbase, pats, np.asarray(pat_of, np.int32).reshape(nblk), krows


def _na_bias_kernel(e_ref, o_ref, *, pats):
    lo = _lane_lo()
    n_pat, qr, kr = pats.shape
    for pi in range(n_pat):
        for rr in range(qr):
            for k2 in range(kr // 2):
                a, b = int(pats[pi, rr, 2 * k2]), int(pats[pi, rr, 2 * k2 + 1])
                blk = e_ref[a] if a == b else jnp.where(lo, e_ref[a], e_ref[b])
                o_ref[pi, rr * GRID_W:(rr + 1) * GRID_W, k2 * LANES:(k2 + 1) * LANES] = blk


def _na_bias_table(rpb, pats):
    depth, h = rpb.shape[:2]
    n_dr = 2 * NA_WIN_R
    c = np.arange(GRID_W)[:, None]
    kc = np.arange(GRID_W)[None, :]
    cs = np.clip(c - NA_WIN_C // 2, 0, GRID_W - NA_WIN_C)
    valid = (kc >= cs) & (kc < cs + NA_WIN_C)
    idx = np.clip(kc - c + NA_WIN_C - 1, 0, 2 * NA_WIN_C - 2)
    e = jnp.where(valid[None, None, None], rpb[:, :, :, idx].astype(F32) * LOG2E, NEG_INF)
    e = jnp.concatenate([e, jnp.full((depth, h, 1, GRID_W, GRID_W), NEG_INF, F32)], axis=2)
    e = jnp.concatenate([e, e], axis=-1)
    n_pat, qr, kr = pats.shape
    assert kr % 2 == 0
    return pl.pallas_call(
        functools.partial(_na_bias_kernel, pats=pats),
        out_shape=jax.ShapeDtypeStruct((depth, n_pat, h, qr * GRID_W, kr * GRID_W), F32),
        grid=(depth, h),
        in_specs=[pl.BlockSpec((None, None, n_dr, GRID_W, LANES), lambda l, hd: (l, hd, 0, 0, 0))],
        out_specs=pl.BlockSpec((None, n_pat, None, qr * GRID_W, kr * GRID_W), lambda l, hd: (l, 0, hd, 0, 0)),
        compiler_params=_cp(("parallel", "parallel"), 40),
        name="na_bias",
    )(e)


def _na_lat_kernel(pat_ref, kb_ref, q_ref, k_ref, v_ref, kc_ref, vc_ref, bias_ref, prev_ref, o_ref, s_ref, *, nk):
    del pat_ref, prev_ref
    blk = pl.program_id(2)
    kstart = pl.multiple_of(kb_ref[blk] * GRID_W, GRID_W)
    lo = _lane_lo()
    tq = q_ref.shape[0]
    n_units = bias_ref.shape[0]
    chunks = _chunks((nk, kc_ref.shape[0]), ATT_CK)

    def lanes_of(unit):
        return slice((unit // 2) * LANES, (unit // 2 + 1) * LANES)

    def q_of(unit):
        q = q_ref[:, lanes_of(unit)] * HEAD64_QSCALE
        return (jnp.where(lo, q, 0.0) if unit % 2 == 0 else jnp.where(lo, 0.0, q)).astype(BF16)

    def kv_chunk(refs, unit, chunk):
        si, st, n, _ = chunk
        if si == 0:
            return refs[0][pl.ds(kstart + st, n), lanes_of(unit)].astype(BF16)
        return refs[1][st:st + n, lanes_of(unit)].astype(BF16)

    def stage_a(unit, qm, chunk, mrun):
        si, st, n, off = chunk
        s = _scores(qm, kv_chunk((k_ref, kc_ref), unit, chunk))
        if si == 0:
            s = s + bias_ref[unit, :, st:st + n]
        s_ref[unit % 2, :, off:off + n] = s
        return _tile_fold(jnp.maximum, mrun, s)

    def stage_b(unit, chunk, m, lrun, acc):
        _, _, n, off = chunk
        p = jnp.exp2(s_ref[unit % 2, :, off:off + n] - m)
        lrun = _tile_fold(jnp.add, lrun, p)
        acc = acc + jnp.dot(p.astype(BF16), kv_chunk((v_ref, vc_ref), unit, chunk), preferred_element_type=F32)
        return lrun, acc

    neg = jnp.full((tq, LANES), -jnp.inf, F32)
    zero = jnp.zeros((tq, LANES), F32)
    mrun = neg
    qm = q_of(0)
    for chunk in chunks:
        mrun = stage_a(0, qm, chunk, mrun)
    outs = []
    for unit in range(n_units):
        m = jnp.max(mrun, axis=-1, keepdims=True)
        lrun, acc, mrun = zero, zero, neg
        if unit + 1 < n_units:
            qm = q_of(unit + 1)
        for chunk in chunks:
            if unit + 1 < n_units:
                mrun = stage_a(unit + 1, qm, chunk, mrun)
            lrun, acc = stage_b(unit, chunk, m, lrun, acc)
        outs.append(acc / jnp.sum(lrun, axis=-1, keepdims=True))
    for hq in range(n_units // 2):
        o_ref[:, hq * LANES:(hq + 1) * LANES] = jnp.where(lo, outs[2 * hq], outs[2 * hq + 1]).astype(o_ref.dtype)


def _na_attn_lat(z, cache_k, cache_v, bias, plan, prev, layer, t_ctx, db, ds, past):
    kbase, _, pat_of, krows = plan
    hs = NA_STEP_HEADS
    w = hs * NA_HD
    tq = NA_QROWS * GRID_W
    nk = krows * GRID_W
    nblk = ds // tq
    q0 = t_ctx // tq
    s0 = t_ctx // ds
    grid_spec = pltpu.PrefetchScalarGridSpec(
        num_scalar_prefetch=2,
        grid=(db, NA_HEADS // hs, nblk),
        in_specs=[pl.BlockSpec((tq, w), lambda b, p, i, pat, kb: (q0 + b * nblk + i, C_NQ // w + p)),
                  pl.BlockSpec((ds, w), lambda b, p, i, pat, kb: (s0 + b, C_NK // w + p)),
                  pl.BlockSpec((ds, w), lambda b, p, i, pat, kb: (s0 + b, C_NV // w + p)),
                  pl.BlockSpec((None, None, past, w), lambda b, p, i, pat, kb: (b, layer, 0, p)),
                  pl.BlockSpec((None, None, past, w), lambda b, p, i, pat, kb: (b, layer, 0, p)),
                  pl.BlockSpec((None, None, hs, tq, nk), lambda b, p, i, pat, kb: (layer, pat[i], p, 0, 0)),
                  pl.BlockSpec(memory_space=pl.ANY)],
        out_specs=pl.BlockSpec((tq, w), lambda b, p, i, pat, kb: (q0 + b * nblk + i, p)),
        scratch_shapes=[pltpu.VMEM((2, tq, nk + past), F32)],
    )
    return pl.pallas_call(
        functools.partial(_na_lat_kernel, nk=nk),
        out_shape=jax.ShapeDtypeStruct(prev.shape, prev.dtype),
        grid_spec=grid_spec,
        input_output_aliases={8: 0},
        compiler_params=_cp(("parallel", "parallel", "arbitrary"), 56),
        name="na_attn_lat",
    )(jnp.asarray(pat_of), jnp.asarray(kbase), z, z, z, cache_k, cache_v, bias, prev)


def _pool_kernel(*refs, t):
    u_ref, w_ref, sc_ref = refs[:3]
    o_ref, pad_ref = refs[-2:]
    halo = 8
    pos = lax.broadcasted_iota(jnp.int32, (t, POOL_GROUP), 0)
    zeros = jnp.zeros((halo, POOL_GROUP), F32)
    for gi, w in enumerate(POOL_WINDOWS):
        sl = slice(gi * POOL_GROUP, (gi + 1) * POOL_GROUP)
        u = u_ref[:, sl]
        pad_ref[0:halo, :] = zeros
        pad_ref[halo + t:2 * halo + t, :] = zeros
        pad_ref[halo:halo + t, :] = u
        tot = None
        for d in range(-(w // 2), w // 2):
            part = pad_ref[halo + d:halo + d + t, :]
            tot = part if tot is None else tot + part
        cnt = (jnp.minimum(pos + w // 2, t) - jnp.maximum(pos - w // 2, 0)).astype(F32)
        pooled = (tot / cnt - u).astype(BF16)
        mixed = jnp.dot(pooled, w_ref[gi], preferred_element_type=F32)
        o_ref[:, sl] = (mixed * sc_ref[:, sl]).astype(o_ref.dtype)


def _pool(z, pool_w, pool_scale, layer, out, prev, n_seq, seq, row0, name):
    width = len(POOL_WINDOWS) * POOL_GROUP
    aliased = prev is not None
    in_specs = [pl.BlockSpec((seq, width), lambda s: (row0 + s, C_POOL // width)),
                pl.BlockSpec((None, len(POOL_WINDOWS), POOL_GROUP, POOL_GROUP), lambda s: (layer, 0, 0, 0)),
                pl.BlockSpec((1, width), lambda s: (0, 0))]
    args = [z, pool_w, pool_scale]
    if aliased:
        in_specs.append(pl.BlockSpec(memory_space=pl.ANY))
        args.append(prev)
        out = jax.ShapeDtypeStruct(prev.shape, prev.dtype)
    return pl.pallas_call(
        functools.partial(_pool_kernel, t=seq),
        out_shape=out,
        grid=(n_seq,),
        in_specs=in_specs,
        out_specs=pl.BlockSpec((seq, width), lambda s: (row0 + s, 0)),
        scratch_shapes=[pltpu.VMEM((seq + 16, POOL_GROUP), F32)],
        input_output_aliases={3: 0} if aliased else {},
        compiler_params=_cp(("parallel",), 56),
        name=name,
    )(*args)


def _diff_prep_kernel(q_ref, k_ref, v_ref, cos_ref, sin_ref, qo_ref, ko_ref, vo_ref):
    cos = cos_ref[...]
    sin = sin_ref[...]
    lane = lax.broadcasted_iota(jnp.int32, (1, LANES), 1)
    first = (lane % DIFF_HD) < DIFF_HD // 2
    for h in range(DIFF_HEADS):
        sl = slice(h * LANES, (h + 1) * LANES)
        qo_ref[:, sl] = (_rope_block(q_ref[:, sl], cos, sin, first, DIFF_HD // 2) * HEAD64_QSCALE).astype(BF16)
        ko_ref[:, sl] = _rope_block(k_ref[:, sl], cos, sin, first, DIFF_HD // 2).astype(BF16)
    vo_ref[...] = v_ref[...].astype(BF16)


def _diff_prep(z, cos_d, sin_d, rope_row_of):
    t = z.shape[0]
    tm = PREP_TM
    w = DIFF_HEADS * LANES
    sds = jax.ShapeDtypeStruct((t, w), BF16)
    return pl.pallas_call(
        _diff_prep_kernel,
        out_shape=(sds, sds, sds),
        grid=(t // tm,),
        in_specs=[pl.BlockSpec((tm, w), lambda i: (i, C_DQ // w)),
                  pl.BlockSpec((tm, w), lambda i: (i, C_DK // w)),
                  pl.BlockSpec((tm, w), lambda i: (i, C_DV // w)),
                  pl.BlockSpec((tm, LANES), lambda i: (rope_row_of(i, tm), 0)),
                  pl.BlockSpec((tm, LANES), lambda i: (rope_row_of(i, tm), 0))],
        out_specs=(pl.BlockSpec((tm, w), lambda i: (i, 0)),) * 3,
        compiler_params=_cp(("parallel",), 40),
        name="diff_prep",
    )(z, z, z, cos_d, sin_d)


def _diff_lambda(linit_ref, lam_ref):
    lam_init = linit_ref[0]
    lp = lam_ref[...]
    lam = (jnp.exp(jnp.sum(lp[0:1] * lp[1:2], axis=-1, keepdims=True))
           - jnp.exp(jnp.sum(lp[2:3] * lp[3:4], axis=-1, keepdims=True)) + lam_init)
    return lam, lam_init


def _diff_ctx_kernel(linit_ref, lam_ref, g_ref, q_ref, k_ref, v_ref, o_ref, *, seq):
    lam, lam_init = _diff_lambda(linit_ref, lam_ref)
    lo = _lane_lo()
    for g in range(CTX_GROUP):
        rows = slice(g * seq, (g + 1) * seq)
        q = q_ref[rows, :]
        k = k_ref[rows, :]
        zero = jnp.zeros_like(q)
        p1, l1 = _softmax2_parts([_scores(jnp.where(lo, q, zero), k)])
        p2, l2 = _softmax2_parts([_scores(jnp.where(lo, zero, q), k)])
        o = _pv([p1[0] * (1.0 / l1) - p2[0] * (lam / l2)], [v_ref[rows, :]])
        o_ref[rows, :] = (_rms(o, g_ref[...]) * (1.0 - lam_init)).astype(o_ref.dtype)


def _diff_attn_ctx(linit, lam_p, g, q, k, v, out, layer, n_seq, seq):
    smem = pl.BlockSpec(memory_space=pltpu.SMEM)
    rows = CTX_GROUP * seq
    return pl.pallas_call(
        functools.partial(_diff_ctx_kernel, seq=seq),
        out_shape=out,
        grid=(n_seq // CTX_GROUP, DIFF_HEADS),
        in_specs=[smem,
                  pl.BlockSpec((None, 4, DIFF_HD), lambda s, h: (layer, 0, 0)),
                  pl.BlockSpec((1, LANES), lambda s, h: (0, 0)),
                  pl.BlockSpec((rows, LANES), lambda s, h: (s, h)),
                  pl.BlockSpec((rows, LANES), lambda s, h: (s, h)),
                  pl.BlockSpec((rows, LANES), lambda s, h: (s, h))],
        out_specs=pl.BlockSpec((rows, LANES), lambda s, h: (s, h)),
        compiler_params=_cp(("parallel", "parallel"), 40),
        name="diff_attn_ctx",
    )(linit, lam_p, g, q, k, v)


def _diff_lat_kernel(linit_ref, lam_ref, g_ref, q_ref, kl_ref, kc_ref, vl_ref, vc_ref, prev_ref, o_ref, s_ref,
                     *, n_sub):
    del prev_ref
    tq = ATT_TQ
    lam, lam_init = _diff_lambda(linit_ref, lam_ref)
    lo = _lane_lo()
    k_refs = (kl_ref, kc_ref)
    v_refs = (vl_ref, vc_ref)
    chunks = _chunks((kl_ref.shape[0], kc_ref.shape[0]), ATT_CK)

    def q_maps(qs):
        q = q_ref[qs * tq:(qs + 1) * tq, :]
        zero = jnp.zeros_like(q)
        return jnp.where(lo, q, zero), jnp.where(lo, zero, q)

    def stage_a(qms, slot, chunk, mruns):
        si, st, n, off = chunk
        k = k_refs[si][st:st + n, :].astype(BF16)
        out = []
        for mi in range(2):
            s = _scores(qms[mi], k)
            s_ref[slot, mi, :, off:off + n] = s
            out.append(_tile_fold(jnp.maximum, mruns[mi], s))
        return out

    def stage_b1(slot, chunk, ms, lruns):
        _, _, n, off = chunk
        out = []
        for mi in range(2):
            e = jnp.exp2(s_ref[slot, mi, :, off:off + n] - ms[mi])
            s_ref[slot, mi, :, off:off + n] = e
            out.append(_tile_fold(jnp.add, lruns[mi], e))
        return out

    def stage_b2(slot, chunk, rho, acc):
        si, st, n, off = chunk
        p = s_ref[slot, 0, :, off:off + n] - s_ref[slot, 1, :, off:off + n] * rho
        return acc + jnp.dot(p.astype(BF16), v_refs[si][st:st + n, :].astype(BF16), preferred_element_type=F32)

    neg = jnp.full((tq, LANES), -jnp.inf, F32)
    zero = jnp.zeros((tq, LANES), F32)
    mruns = [neg, neg]
    qms = q_maps(0)
    for chunk in chunks:
        mruns = stage_a(qms, 0, chunk, mruns)
    for qs in range(n_sub):
        slot = qs % 2
        ms = [jnp.max(mr, axis=-1, keepdims=True) for mr in mruns]
        lruns, mruns = [zero, zero], [neg, neg]
        if qs + 1 < n_sub:
            qms = q_maps(qs + 1)
        for chunk in chunks:
            if qs + 1 < n_sub:
                mruns = stage_a(qms, 1 - slot, chunk, mruns)
            lruns = stage_b1(slot, chunk, ms, lruns)
        l1 = jnp.sum(lruns[0], axis=-1, keepdims=True)
        rho = lam * l1 / jnp.sum(lruns[1], axis=-1, keepdims=True)
        acc = zero
        for chunk in chunks:
            acc = stage_b2(slot, chunk, rho, acc)
        o_ref[qs * tq:(qs + 1) * tq, :] = (_rms(acc / l1, g_ref[...]) * (1.0 - lam_init)).astype(o_ref.dtype)


def _diff_attn_lat(linit, lam_p, g, q, k, v, cache_k, cache_v, prev, layer, t_ctx, db, ds, past):
    smem = pl.BlockSpec(memory_space=pltpu.SMEM)
    n_sub = 4
    tqb = n_sub * ATT_TQ
    nq = ds // tqb
    q0 = t_ctx // tqb
    s0 = t_ctx // ds
    return pl.pallas_call(
        functools.partial(_diff_lat_kernel, n_sub=n_sub),
        out_shape=jax.ShapeDtypeStruct(prev.shape, prev.dtype),
        grid=(db, DIFF_HEADS, nq),
        in_specs=[smem,
                  pl.BlockSpec((None, 4, DIFF_HD), lambda b, h, i: (layer, 0, 0)),
                  pl.BlockSpec((1, LANES), lambda b, h, i: (0, 0)),
                  pl.BlockSpec((tqb, LANES), lambda b, h, i: (q0 + b * nq + i, h)),
                  pl.BlockSpec((ds, LANES), lambda b, h, i: (s0 + b, h)),
                  pl.BlockSpec((None, None, past, LANES), lambda b, h, i: (b, layer, 0, h)),
                  pl.BlockSpec((ds, LANES), lambda b, h, i: (s0 + b, h)),
                  pl.BlockSpec((None, None, past, LANES), lambda b, h, i: (b, layer, 0, h)),
                  pl.BlockSpec(memory_space=pl.ANY)],
        out_specs=pl.BlockSpec((tqb, LANES), lambda b, h, i: (q0 + b * nq + i, h)),
        scratch_shapes=[pltpu.VMEM((2, 2, ATT_TQ, ds + past), F32)],
        input_output_aliases={8: 0},
        compiler_params=_cp(("parallel", "parallel", "arbitrary"), 56),
        name="diff_attn_lat",
    )(linit, lam_p, g, q, k, cache_k, v, cache_v, prev)


def _merge_out_kernel(h_ref, wg0_ref, wg1_ref, wg2_ref, wg3_ref, oa_ref, ob_ref, oc_ref, od_ref, wbr_ref, wo_ref,
                      x_ref, mod_ref, gn_ref, x1_ref, h2_ref, *, d):
    n = pl.program_id(1)

    @pl.when(n == 0)
    def _():
        x1_ref[...] = jnp.zeros_like(x1_ref)

    h = h_ref[...]
    merged = None
    for bi, (wg_ref, o_ref) in enumerate(((wg0_ref, oa_ref), (wg1_ref, ob_ref), (wg2_ref, oc_ref), (wg3_ref, od_ref))):
        term = (jax.nn.sigmoid(jnp.dot(h, wg_ref[...], preferred_element_type=F32))
                * jnp.dot(o_ref[...], wbr_ref[bi], preferred_element_type=F32))
        merged = term if merged is None else merged + term
    x1_ref[...] += jnp.dot(merged.astype(BF16), wo_ref[...], preferred_element_type=F32)

    @pl.when(n == pl.num_programs(1) - 1)
    def _():
        mod = mod_ref[0]
        g1 = mod[:, 2 * d:3 * d]
        sh2 = mod[:, 3 * d:4 * d]
        sc2 = mod[:, 4 * d:5 * d]
        x1 = x_ref[...] + g1 * _rms(x1_ref[...], gn_ref[1:2, :])
        x1_ref[...] = x1
        h2_ref[...] = (_rms(x1, gn_ref[2:3, :]) * (1.0 + sc2) + sh2).astype(BF16)


def _merge_out(h, w_gates, outs, w_br, w_o, layer, x, mod_l, gn, row_of):
    t, d = x.shape
    bw = outs[0].shape[1]
    tm, tn = 512, MERGE_TN
    nt = d // tn
    row_spec = pl.BlockSpec((tm, d), lambda i, n: (i, 0))
    o_spec = pl.BlockSpec((tm, bw), lambda i, n: (i, 0))
    g_specs = [pl.BlockSpec((None, d, tn), functools.partial(lambda i, n, bi: (layer, 0, bi * nt + n), bi=bi))
               for bi in range(4)]
    return pl.pallas_call(
        functools.partial(_merge_out_kernel, d=d),
        out_shape=(jax.ShapeDtypeStruct((t, d), F32), jax.ShapeDtypeStruct((t, d), BF16)),
        grid=(t // tm, nt),
        in_specs=[row_spec, *g_specs,
                  o_spec, o_spec, o_spec, o_spec,
                  pl.BlockSpec((None, 4, bw, tn), lambda i, n: (layer, 0, 0, n)),
                  pl.BlockSpec((None, tn, d), lambda i, n: (layer, n, 0)),
                  row_spec,
                  pl.BlockSpec((1, 1, 6 * d), lambda i, n: (row_of(i, tm), 0, 0)),
                  pl.BlockSpec((4, d), lambda i, n: (0, 0))],
        out_specs=(row_spec, row_spec),
        compiler_params=_cp(("parallel", "arbitrary"), 56),
        name="merge_out",
    )(h, w_gates, w_gates, w_gates, w_gates, *outs, w_br, w_o, x, mod_l, gn)


def _ffn_kernel(*refs, d, emit_next):
    h_ref, wu_ref, wd_ref, x_ref, mod_ref, gn_ref = refs[:6]
    o_ref = refs[8] if emit_next else refs[6]
    f = pl.program_id(1)

    @pl.when(f == 0)
    def _():
        o_ref[...] = jnp.zeros_like(o_ref)

    u = jnp.dot(h_ref[...], wu_ref[...], preferred_element_type=F32)
    u = jnp.square(jnp.maximum(u, 0.0)).astype(BF16)
    o_ref[...] += jnp.dot(u, wd_ref[...], preferred_element_type=F32)

    @pl.when(f == pl.num_programs(1) - 1)
    def _():
        g2 = mod_ref[0][:, 5 * d:6 * d]
        x_new = x_ref[...] + g2 * _rms(o_ref[...], gn_ref[3:4, :])
        o_ref[...] = x_new
        if emit_next:
            modn_ref, gnn_ref, hn_ref = refs[6], refs[7], refs[9]
            modn = modn_ref[0]
            hn_ref[...] = (_rms(x_new, gnn_ref[0:1, :]) * (1.0 + modn[:, d:2 * d]) + modn[:, 0:d]).astype(BF16)


def _ffn(h2, w_up, w_down, layer, x1, mod_l, gn, row_of, mod_next=None, gn_next=None, row0=0, n_rows=None):
    t, d = x1.shape
    n_rows = t if n_rows is None else n_rows
    dff = w_up.shape[2]
    tm, tf = 512, 1024
    i0 = row0 // tm
    emit_next = mod_next is not None
    mod_spec = pl.BlockSpec((1, 1, 6 * d), lambda i, f: (row_of(i0 + i, tm), 0, 0))
    gn_spec = pl.BlockSpec((4, d), lambda i, f: (0, 0))
    in_row_spec = pl.BlockSpec((tm, d), lambda i, f: (i0 + i, 0))
    out_row_spec = pl.BlockSpec((tm, d), lambda i, f: (i, 0))
    in_specs = [in_row_spec,
                pl.BlockSpec((None, d, tf), lambda i, f: (layer, 0, f)),
                pl.BlockSpec((None, tf, d), lambda i, f: (layer, f, 0)),
                in_row_spec, mod_spec, gn_spec]
    args = [h2, w_up, w_down, x1, mod_l, gn]
    out_shape = jax.ShapeDtypeStruct((n_rows, d), F32)
    out_specs = out_row_spec
    if emit_next:
        in_specs += [mod_spec, gn_spec]
        args += [mod_next, gn_next]
        out_shape = (out_shape, jax.ShapeDtypeStruct((n_rows, d), BF16))
        out_specs = (out_row_spec, out_row_spec)
    return pl.pallas_call(
        functools.partial(_ffn_kernel, d=d, emit_next=emit_next),
        out_shape=out_shape,
        grid=(n_rows // tm, dff // tf),
        in_specs=in_specs,
        out_specs=out_specs,
        compiler_params=_cp(("parallel", "arbitrary"), 58),
        name="ffn",
    )(*args)


def _prep_weights(w_in, w_uq, w_ukv):
    depth, d, _ = w_in.shape
    sizes = (MLA_Q_LORA, MLA_KV_LORA, MLA_ROPE, 512, 512, 512, 512, 512, 512, 512, 4 * d)
    offs = np.cumsum((0,) + sizes)
    part = lambda i: w_in[:, :, offs[i]:offs[i + 1]]
    q_c, kv_c, k_r, na_q, na_k, na_v, pool, dq, dk, dv, gates = (part(i) for i in range(11))
    kr_blk = jnp.pad(k_r, ((0, 0), (0, 0), (MLA_NOPE, LANES - MLA_NOPE - MLA_ROPE)))
    w_all = jnp.concatenate([q_c, dq, dk, dv, na_q, na_k, na_v, pool, kv_c, kr_blk], axis=-1).astype(BF16)
    w_gates = gates.astype(BF16)
    hd = MLA_NOPE + MLA_ROPE
    wuq = jnp.pad(w_uq.reshape(depth, MLA_Q_LORA, MLA_HEADS, hd), ((0, 0), (0, 0), (0, 0), (0, LANES - hd)))
    wuq = wuq.reshape(depth, MLA_Q_LORA, MLA_HEADS * LANES).astype(BF16)
    wkv4 = w_ukv.reshape(depth, MLA_KV_LORA, MLA_HEADS, MLA_NOPE + MLA_V)
    wk = jnp.pad(wkv4[..., :MLA_NOPE], ((0, 0), (0, 0), (0, 0), (0, LANES - MLA_NOPE)))
    wk = wk.reshape(depth, MLA_KV_LORA, MLA_HEADS * LANES)
    wv = wkv4[..., MLA_NOPE:].reshape(depth, MLA_KV_LORA, MLA_HEADS * MLA_V)
    wkv = jnp.concatenate([wk, wv], axis=-1).astype(BF16)
    return w_all, w_gates, wuq, wkv


def kernel(x_prompt, x_sample, cache_mla_ckv, cache_mla_krope, cache_na_k, cache_na_v, cache_diff_k, cache_diff_v, c, c_ctx, w_mod, b_mod, g_norm, w_in, g_q_lora, g_kv_lora, w_uq, w_ukv, na_rpb, pool_w, pool_scale, diff_lambda, diff_norm_g, w_br, w_o, w_up, w_down):
    nb, seq, d = x_prompt.shape
    db, ds, _ = x_sample.shape
    depth = w_in.shape[0]
    past = cache_mla_ckv.shape[2]
    t_ctx = nb * seq
    t_lat = db * ds
    t = t_ctx + t_lat
    assert t_ctx % ds == 0 and ds % 1024 == 0 and t_ctx % 1024 == 0 and seq % 8 == 0 and nb % CTX_GROUP == 0

    def row_of(i, tm):
        n_ctx = t_ctx // tm
        return jnp.where(i < n_ctx, 0, 1 + (i - n_ctx) // (ds // tm))

    def rope_row_of(i, tm):
        n_ctx = t_ctx // tm
        return jnp.where(i < n_ctx, 0, 1 + (i - n_ctx) % (ds // tm))

    w_all, w_gates, wuq, wkv = _prep_weights(w_in, w_uq, w_ukv)
    w_br_b = w_br.astype(BF16)
    w_o_b = w_o.astype(BF16)
    w_up_b = w_up.astype(BF16)
    w_down_b = w_down.astype(BF16)
    pool_w_b = pool_w.astype(BF16)
    cos_m, sin_m, cos_d, sin_d = _rope_tables(ds, PREP_TM)
    rows = ds // GRID_W
    na_plan = _na_plan(rows)
    bias = _na_bias_table(na_rpb, na_plan[1])
    cache_kr_pad = jnp.pad(cache_mla_krope, ((0, 0), (0, 0), (0, 0), (MLA_NOPE, LANES - MLA_NOPE - MLA_ROPE)))
    cache_nk = cache_na_k.reshape(db, depth, past, NA_HEADS * NA_HD)
    cache_nv = cache_na_v.reshape(db, depth, past, NA_HEADS * NA_HD)
    cache_dk = cache_diff_k.reshape(db, depth, past, DIFF_HEADS * 2 * DIFF_HD)
    cache_dv = cache_diff_v.reshape(db, depth, past, DIFF_HEADS * 2 * DIFF_HD)

    n_rows = 1 + db
    r_pad = -(-n_rows // 8) * 8
    cond = jnp.concatenate([c_ctx[None, :], c, jnp.zeros((r_pad - n_rows, d), F32)], axis=0)
    mod = _modulation(cond, w_mod, b_mod)
    mods = [mod[li].reshape(r_pad, 1, 6 * d) for li in range(depth)]

    branch_sds = jax.ShapeDtypeStruct((t, 512), BF16)
    x, h = _norm_mod(x_prompt.reshape(t_ctx, d), x_sample.reshape(t_lat, d), mods[0], g_norm[0][0:1], row_of)
    states = []
    for li in range(depth):
        mod_l = mods[li]
        gn = g_norm[li]
        z = _matmul(h, w_all, li, 1024, 896, F32, "in_proj")

        q_a, ckv, k_a, v_a = _mla_prep(z, cos_m, sin_m, g_q_lora[li][None, :], g_kv_lora[li][None, :],
                                       wuq, wkv, li, rope_row_of)
        kc_a, vc_a = _mla_cache(cache_mla_ckv, cache_kr_pad, wkv, li)
        o_a = _mla_attn_ctx(q_a, k_a, v_a, branch_sds, nb, seq)
        o_a = _mla_attn_lat(q_a, k_a, v_a, kc_a, vc_a, o_a, t_ctx, db, ds, past)

        o_b = _na_attn_ctx(z, branch_sds, nb, seq)
        o_b = _na_attn_lat(z, cache_nk, cache_nv, bias, na_plan, o_b, li, t_ctx, db, ds, past)

        o_c = _pool(z, pool_w_b, pool_scale[li][None, :], li, branch_sds, None, nb, seq, 0, "pool_ctx")
        o_c = _pool(z, pool_w_b, pool_scale[li][None, :], li, None, o_c, db, ds, t_ctx // ds, "pool_lat")

        q_d, k_d, v_d = _diff_prep(z, cos_d, sin_d, rope_row_of)
        linit = jnp.full((1,), 0.8 - 0.6 * math.exp(-0.3 * li), F32)
        g_d = diff_norm_g[li][None, :]
        o_d = _diff_attn_ctx(linit, diff_lambda, g_d, q_d, k_d, v_d, branch_sds, li, nb, seq)
        o_d = _diff_attn_lat(linit, diff_lambda, g_d, q_d, k_d, v_d, cache_dk, cache_dv, o_d,
                             li, t_ctx, db, ds, past)

        x1, h2 = _merge_out(h, w_gates, (o_a, o_b, o_c, o_d), w_br_b, w_o_b, li, x, mod_l, gn, row_of)
        if li + 1 < depth:
            x, h = _ffn(h2, w_up_b, w_down_b, li, x1, mod_l, gn, row_of, mods[li + 1], g_norm[li + 1])
        else:
            y_prompt = _ffn(h2, w_up_b, w_down_b, li, x1, mod_l, gn, row_of, row0=0, n_rows=t_ctx)
            y_sample = _ffn(h2, w_up_b, w_down_b, li, x1, mod_l, gn, row_of, row0=t_ctx, n_rows=t_lat)

        zc = z[:t_ctx]
        states.append((ckv[:t_ctx], zc[:, C_KR + MLA_NOPE:C_KR + MLA_NOPE + MLA_ROPE], zc[:, C_NK:C_NK + 512],
                       zc[:, C_NV:C_NV + 512], zc[:, C_DK:C_DK + 512], zc[:, C_DV:C_DV + 512]))

    def stacked(k, tail):
        per_layer = [s[k].reshape(nb, seq, -1) for s in states]
        return jnp.stack(per_layer, axis=1).reshape(nb, depth, seq, *tail)

    return (y_prompt.reshape(nb, seq, d), y_sample.reshape(db, ds, d),
            stacked(0, (MLA_KV_LORA,)), stacked(1, (MLA_ROPE,)),
            stacked(2, (NA_HEADS, NA_HD)), stacked(3, (NA_HEADS, NA_HD)),
            stacked(4, (DIFF_HEADS, 2 * DIFF_HD)), stacked(5, (DIFF_HEADS, 2 * DIFF_HD)))
```

```python
import functools
import math

import numpy as np
import jax
import jax.numpy as jnp
from jax import lax
from jax.experimental import pallas as pl
from jax.experimental.pallas import tpu as pltpu

F32 = jnp.float32
BF16 = jnp.bfloat16

GRID_W = 64
MLA_HEADS = 8
MLA_NOPE = 64
MLA_ROPE = 32
MLA_V = 64
MLA_Q_LORA = 512
MLA_KV_LORA = 256
NA_HEADS = 8
NA_HD = 64
NA_WIN_R = 8
NA_WIN_C = 16
POOL_WINDOWS = (2, 4, 8, 16)
POOL_GROUP = 128
DIFF_HEADS = 4
DIFF_HD = 64
ROPE_THETA = 10000.0
RMS_EPS = 1e-6
NEG_INF = -1e30
LOG2E = math.log2(math.e)
MLA_QSCALE = (MLA_NOPE + MLA_ROPE) ** -0.5 * LOG2E
HEAD64_QSCALE = 64 ** -0.5 * LOG2E
LANES = 128
MIB = 1024 * 1024

C_QC = 0
C_DQ = 512
C_DK = 1024
C_DV = 1536
C_NQ = 2048
C_NK = 2560
C_NV = 3072
C_POOL = 3584
C_KVC = 4096
C_KR = 4352
Z_COLS = 4480

MERGE_TN = 256
NA_STEP_HEADS = 4
NA_QROWS = 8
NA_KROWS = 16
ATT_TQ = 256
DIFF_TQ = 256
DIFF_SUBTILES = 4
ATT_CK = 512
CTX_GROUP = 4
EPI_ROWS = 16
PREP_TM = 1024


def _cp(sem, vmem_mib):
    return pltpu.CompilerParams(dimension_semantics=sem, vmem_limit_bytes=vmem_mib * MIB)


def _rms(x, g):
    return x * lax.rsqrt(jnp.mean(x * x, axis=-1, keepdims=True) + RMS_EPS) * g


def _scores(q, k):
    return lax.dot_general(q, k, (((1,), (1,)), ((), ())), preferred_element_type=F32)


def _lane_lo():
    return lax.broadcasted_iota(jnp.int32, (1, LANES), 1) < 64


def _tile_fold(op, run, x):
    for c in range(x.shape[1] // LANES):
        run = op(run, x[:, c * LANES:(c + 1) * LANES])
    return run


def _chunks(sizes, ck):
    out, off = [], 0
    for si, n in enumerate(sizes):
        step = min(ck, n)
        for st in range(0, n, step):
            out.append((si, st, step, off))
            off += step
    return out


def _mod_kernel(c_ref, w_ref, b_ref, o_ref):
    c = c_ref[...]
    s = c * jax.nn.sigmoid(c)
    o_ref[0] = jnp.dot(s.astype(BF16), w_ref[0].astype(BF16), preferred_element_type=F32) + b_ref[0]


def _modulation(cond, w_mod, b_mod):
    depth, d, n = w_mod.shape
    r = cond.shape[0]
    tn = 1024
    return pl.pallas_call(
        _mod_kernel,
        out_shape=jax.ShapeDtypeStruct((depth, r, n), F32),
        grid=(depth, n // tn),
        in_specs=[pl.BlockSpec((r, d), lambda l, j: (0, 0)),
                  pl.BlockSpec((1, d, tn), lambda l, j: (l, 0, j)),
                  pl.BlockSpec((1, 1, tn), lambda l, j: (l, 0, j))],
        out_specs=pl.BlockSpec((1, r, tn), lambda l, j: (l, 0, j)),
        compiler_params=_cp(("parallel", "parallel"), 40),
        name="modulation",
    )(cond, w_mod, b_mod.reshape(depth, 1, n))


def _norm_mod_kernel(xp_ref, xs_ref, mod_ref, g_ref, x_ref, h_ref, *, d, n_ctx):
    mod = mod_ref[0]
    sh = mod[:, 0:d]
    sc = mod[:, d:2 * d]

    def emit(src_ref):
        x = src_ref[...]
        x_ref[...] = x
        h_ref[...] = (_rms(x, g_ref[...]) * (1.0 + sc) + sh).astype(BF16)

    @pl.when(pl.program_id(0) < n_ctx)
    def _():
        emit(xp_ref)

    @pl.when(pl.program_id(0) >= n_ctx)
    def _():
        emit(xs_ref)


def _norm_mod(xp, xs, mod_l, g, row_of):
    (t_ctx, d), t_lat = xp.shape, xs.shape[0]
    tm = 512
    n_ctx = t_ctx // tm
    t = t_ctx + t_lat
    return pl.pallas_call(
        functools.partial(_norm_mod_kernel, d=d, n_ctx=n_ctx),
        out_shape=(jax.ShapeDtypeStruct((t, d), F32), jax.ShapeDtypeStruct((t, d), BF16)),
        grid=(t // tm,),
        in_specs=[pl.BlockSpec((tm, d), lambda i: (jnp.minimum(i, n_ctx - 1), 0)),
                  pl.BlockSpec((tm, d), lambda i: (jnp.maximum(i - n_ctx, 0), 0)),
                  pl.BlockSpec((1, 1, 6 * d), lambda i: (row_of(i, tm), 0, 0)),
                  pl.BlockSpec((1, d), lambda i: (0, 0))],
        out_specs=(pl.BlockSpec((tm, d), lambda i: (i, 0)), pl.BlockSpec((tm, d), lambda i: (i, 0))),
        compiler_params=_cp(("parallel",), 48),
        name="norm_mod",
    )(xp, xs, mod_l, g)


def _mm_kernel(a_ref, b_ref, o_ref):
    o_ref[...] = jnp.dot(a_ref[...], b_ref[...], preferred_element_type=F32).astype(o_ref.dtype)


def _matmul(a, b, layer, tm, tn, out_dtype, name):
    m, k = a.shape
    n = b.shape[2]
    return pl.pallas_call(
        _mm_kernel,
        out_shape=jax.ShapeDtypeStruct((m, n), out_dtype),
        grid=(m // tm, n // tn),
        in_specs=[pl.BlockSpec((tm, k), lambda i, j: (i, 0)),
                  pl.BlockSpec((None, k, tn), lambda i, j: (layer, 0, j))],
        out_specs=pl.BlockSpec((tm, tn), lambda i, j: (i, j)),
        compiler_params=_cp(("parallel", "arbitrary"), 48),
        name=name,
    )(a, b)


def _rope_block(x, cos, sin, first_half, half):
    partner = jnp.where(first_half, -pltpu.roll(x, LANES - half, 1), pltpu.roll(x, half, 1))
    return x * cos + partner * sin


def _rope_tables(n_lat, ident_rows):
    t = jnp.arange(n_lat)
    row = (t // GRID_W).astype(F32)
    col = (t % GRID_W).astype(F32)

    def angles(rot_dim):
        n_freq = rot_dim // 4
        inv = ROPE_THETA ** (-jnp.arange(n_freq, dtype=F32) / n_freq)
        return jnp.concatenate([row[:, None] * inv, col[:, None] * inv], axis=-1)

    a_m = angles(MLA_ROPE)
    zeros64 = jnp.zeros((n_lat, 64), F32)
    zeros32 = jnp.zeros((n_lat, 32), F32)
    cos_m = jnp.concatenate([zeros64 + 1.0, jnp.cos(a_m), jnp.cos(a_m), zeros32 + 1.0], axis=-1)
    sin_m = jnp.concatenate([zeros64, jnp.sin(a_m), jnp.sin(a_m), zeros32], axis=-1)
    a_d = angles(DIFF_HD)
    cos_d = jnp.tile(jnp.cos(a_d), (1, 4))
    sin_d = jnp.tile(jnp.sin(a_d), (1, 4))
    one = jnp.ones((ident_rows, LANES), F32)
    zero = jnp.zeros((ident_rows, LANES), F32)
    cat = lambda a, b: jnp.concatenate([a, b], axis=0)
    return cat(one, cos_m), cat(zero, sin_m), cat(one, cos_d), cat(zero, sin_d)


def _mla_prep_kernel(qc_ref, kvc_ref, kr_ref, cos_ref, sin_ref, gq_ref, gkv_ref, wuq_ref, wkv_ref,
                     q_ref, ckv_ref, k_ref, v_ref):
    cos = cos_ref[...]
    sin = sin_ref[...]
    lane = lax.broadcasted_iota(jnp.int32, (1, LANES), 1)
    first = lane < MLA_NOPE + MLA_ROPE // 2
    rope = lambda x: _rope_block(x, cos, sin, first, MLA_ROPE // 2)
    qn = _rms(qc_ref[...], gq_ref[...]).astype(BF16)
    q = jnp.dot(qn, wuq_ref[...], preferred_element_type=F32)
    for h in range(MLA_HEADS):
        sl = slice(h * LANES, (h + 1) * LANES)
        q_ref[:, sl] = (rope(q[:, sl]) * MLA_QSCALE).astype(BF16)
    ckv = _rms(kvc_ref[...], gkv_ref[...])
    ckv_ref[...] = ckv
    kv = jnp.dot(ckv.astype(BF16), wkv_ref[...], preferred_element_type=F32)
    krr = rope(kr_ref[...])
    for h in range(MLA_HEADS):
        sl = slice(h * LANES, (h + 1) * LANES)
        k_ref[:, sl] = (kv[:, sl] + krr).astype(BF16)
    v_ref[...] = kv[:, MLA_HEADS * LANES:].astype(BF16)


def _mla_prep(z, cos_m, sin_m, gq, gkv, wuq, wkv, layer, rope_row_of):
    t = z.shape[0]
    tm = PREP_TM
    kw = MLA_HEADS * LANES
    vw = MLA_HEADS * MLA_V
    return pl.pallas_call(
        _mla_prep_kernel,
        out_shape=(jax.ShapeDtypeStruct((t, kw), BF16), jax.ShapeDtypeStruct((t, MLA_KV_LORA), F32),
                   jax.ShapeDtypeStruct((t, kw), BF16), jax.ShapeDtypeStruct((t, vw), BF16)),
        grid=(t // tm,),
        in_specs=[pl.BlockSpec((tm, MLA_Q_LORA), lambda i: (i, C_QC // MLA_Q_LORA)),
                  pl.BlockSpec((tm, MLA_KV_LORA), lambda i: (i, C_KVC // MLA_KV_LORA)),
                  pl.BlockSpec((tm, LANES), lambda i: (i, C_KR // LANES)),
                  pl.BlockSpec((tm, LANES), lambda i: (rope_row_of(i, tm), 0)),
                  pl.BlockSpec((tm, LANES), lambda i: (rope_row_of(i, tm), 0)),
                  pl.BlockSpec((1, MLA_Q_LORA), lambda i: (0, 0)),
                  pl.BlockSpec((1, MLA_KV_LORA), lambda i: (0, 0)),
                  pl.BlockSpec((None, MLA_Q_LORA, kw), lambda i: (layer, 0, 0)),
                  pl.BlockSpec((None, MLA_KV_LORA, kw + vw), lambda i: (layer, 0, 0))],
        out_specs=(pl.BlockSpec((tm, kw), lambda i: (i, 0)),
                   pl.BlockSpec((tm, MLA_KV_LORA), lambda i: (i, 0)),
                   pl.BlockSpec((tm, kw), lambda i: (i, 0)),
                   pl.BlockSpec((tm, vw), lambda i: (i, 0))),
        compiler_params=_cp(("parallel",), 56),
        name="mla_prep",
    )(z, z, z, cos_m, sin_m, gq, gkv, wuq, wkv)


def _mla_cache_kernel(ckv_ref, kr_ref, wkv_ref, k_ref, v_ref):
    kv = jnp.dot(ckv_ref[...].astype(BF16), wkv_ref[...], preferred_element_type=F32)
    kr = kr_ref[...]
    for h in range(MLA_HEADS):
        sl = slice(h * LANES, (h + 1) * LANES)
        k_ref[:, sl] = (kv[:, sl] + kr).astype(BF16)
    v_ref[...] = kv[:, MLA_HEADS * LANES:].astype(BF16)


def _mla_cache(cache_ckv, cache_kr_pad, wkv, layer):
    db, _, past, _ = cache_ckv.shape
    kw = MLA_HEADS * LANES
    vw = MLA_HEADS * MLA_V
    return pl.pallas_call(
        _mla_cache_kernel,
        out_shape=(jax.ShapeDtypeStruct((db * past, kw), BF16), jax.ShapeDtypeStruct((db * past, vw), BF16)),
        grid=(db,),
        in_specs=[pl.BlockSpec((None, None, past, MLA_KV_LORA), lambda b: (b, layer, 0, 0)),
                  pl.BlockSpec((None, None, past, LANES), lambda b: (b, layer, 0, 0)),
                  pl.BlockSpec((None, MLA_KV_LORA, kw + vw), lambda b: (layer, 0, 0))],
        out_specs=(pl.BlockSpec((past, kw), lambda b: (b, 0)), pl.BlockSpec((past, vw), lambda b: (b, 0))),
        compiler_params=_cp(("parallel",), 40),
        name="mla_cache",
    )(cache_ckv, cache_kr_pad, wkv)


def _phased_softmax_pv(ss, vs):
    ms = [jnp.max(s, axis=-1, keepdims=True) for s in ss]
    ps = [jnp.exp2(s - m) for s, m in zip(ss, ms)]
    ls = [jnp.sum(p, axis=-1, keepdims=True) for p in ps]
    return [jnp.dot(p.astype(BF16), v, preferred_element_type=F32) / l for p, v, l in zip(ps, vs, ls)]


def _mla_ctx_kernel(q_ref, k_ref, v_ref, o_ref, *, seq):
    lo = _lane_lo()
    rows = [slice(g * seq, (g + 1) * seq) for g in range(CTX_GROUP)]
    lanes = [slice(hh * LANES, (hh + 1) * LANES) for hh in range(2)]
    units = [(r, sl) for r in rows for sl in lanes]
    outs = _phased_softmax_pv([_scores(q_ref[r, sl], k_ref[r, sl]) for r, sl in units],
                              [v_ref[r, :] for r, _ in units])
    for g, r in enumerate(rows):
        o_ref[r, :] = jnp.where(lo, outs[2 * g], outs[2 * g + 1]).astype(o_ref.dtype)


def _mla_attn_ctx(q, k, v, out, n_seq, seq):
    pairs = MLA_HEADS // 2
    rows = CTX_GROUP * seq
    return pl.pallas_call(
        functools.partial(_mla_ctx_kernel, seq=seq),
        out_shape=out,
        grid=(n_seq // CTX_GROUP, pairs),
        in_specs=[pl.BlockSpec((rows, 2 * LANES), lambda s, p: (s, p)),
                  pl.BlockSpec((rows, 2 * LANES), lambda s, p: (s, p)),
                  pl.BlockSpec((rows, LANES), lambda s, p: (s, p))],
        out_specs=pl.BlockSpec((rows, LANES), lambda s, p: (s, p)),
        compiler_params=_cp(("parallel", "parallel"), 40),
        name="mla_attn_ctx",
    )(q, k, v)


def _mla_lat_kernel(q_ref, kl_ref, kc_ref, vl_ref, vc_ref, prev_ref, o_ref, s_ref, *, n_sub):
    del prev_ref
    tq = ATT_TQ
    k_refs = (kl_ref, kc_ref)
    v_refs = (vl_ref, vc_ref)
    chunks = _chunks((kl_ref.shape[0], kc_ref.shape[0]), ATT_CK)
    units = [(qs, hh) for qs in range(n_sub) for hh in range(2)]

    def stage_a(unit, slot, chunk, mrun):
        qs, hh = unit
        si, st, n, off = chunk
        sl = slice(hh * LANES, (hh + 1) * LANES)
        s = _scores(q_ref[qs * tq:(qs + 1) * tq, sl], k_refs[si][st:st + n, sl])
        s_ref[slot, :, off:off + n] = s
        return _tile_fold(jnp.maximum, mrun, s)

    def stage_b(slot, chunk, m, lrun, acc):
        si, st, n, off = chunk
        p = jnp.exp2(s_ref[slot, :, off:off + n] - m)
        lrun = _tile_fold(jnp.add, lrun, p)
        acc = acc + jnp.dot(p.astype(BF16), v_refs[si][st:st + n, :], preferred_element_type=F32)
        return lrun, acc

    neg = jnp.full((tq, LANES), -jnp.inf, F32)
    zero = jnp.zeros((tq, LANES), F32)
    mrun = neg
    for chunk in chunks:
        mrun = stage_a(units[0], 0, chunk, mrun)
    outs = {}
    for ui, unit in enumerate(units):
        slot = ui % 2
        m = jnp.max(mrun, axis=-1, keepdims=True)
        lrun, acc, mrun = zero, zero, neg
        for chunk in chunks:
            if ui + 1 < len(units):
                mrun = stage_a(units[ui + 1], 1 - slot, chunk, mrun)
            lrun, acc = stage_b(slot, chunk, m, lrun, acc)
        outs[unit] = acc / jnp.sum(lrun, axis=-1, keepdims=True)
    lo = _lane_lo()
    for qs in range(n_sub):
        o_ref[qs * tq:(qs + 1) * tq, :] = jnp.where(lo, outs[(qs, 0)], outs[(qs, 1)]).astype(o_ref.dtype)


def _mla_attn_lat(q, k, v, kc, vc, prev, t_ctx, db, ds, past):
    pairs = MLA_HEADS // 2
    n_sub = 2
    tqb = n_sub * ATT_TQ
    nq = ds // tqb
    q0 = t_ctx // tqb
    s0 = t_ctx // ds
    return pl.pallas_call(
        functools.partial(_mla_lat_kernel, n_sub=n_sub),
        out_shape=jax.ShapeDtypeStruct(prev.shape, prev.dtype),
        grid=(db, pairs, nq),
        in_specs=[pl.BlockSpec((tqb, 2 * LANES), lambda b, p, i: (q0 + b * nq + i, p)),
                  pl.BlockSpec((ds, 2 * LANES), lambda b, p, i: (s0 + b, p)),
                  pl.BlockSpec((past, 2 * LANES), lambda b, p, i: (b, p)),
                  pl.BlockSpec((ds, LANES), lambda b, p, i: (s0 + b, p)),
                  pl.BlockSpec((past, LANES), lambda b, p, i: (b, p)),
                  pl.BlockSpec(memory_space=pl.ANY)],
        out_specs=pl.BlockSpec((tqb, LANES), lambda b, p, i: (q0 + b * nq + i, p)),
        scratch_shapes=[pltpu.VMEM((2, ATT_TQ, ds + past), F32)],
        input_output_aliases={5: 0},
        compiler_params=_cp(("parallel", "parallel", "arbitrary"), 56),
        name="mla_attn_lat",
    )(q, k, kc, v, vc, prev)


def _attn64_ctx_kernel(q_ref, k_ref, v_ref, o_ref, *, seq):
    lo = _lane_lo()
    rows = [slice(g * seq, (g + 1) * seq) for g in range(CTX_GROUP)]
    qs = [q_ref[r, :] * HEAD64_QSCALE for r in rows]
    ks = [k_ref[r, :].astype(BF16) for r in rows]
    vs = [v_ref[r, :].astype(BF16) for r in rows]
    ss, vv = [], []
    for q, k, v in zip(qs, ks, vs):
        ss.append(_scores(jnp.where(lo, q, 0.0).astype(BF16), k))
        ss.append(_scores(jnp.where(lo, 0.0, q).astype(BF16), k))
        vv += [v, v]
    outs = _phased_softmax_pv(ss, vv)
    for g, r in enumerate(rows):
        o_ref[r, :] = jnp.where(lo, outs[2 * g], outs[2 * g + 1]).astype(o_ref.dtype)


def _na_attn_ctx(z, out, n_seq, seq):
    pairs = NA_HEADS // 2
    rows = CTX_GROUP * seq
    return pl.pallas_call(
        functools.partial(_attn64_ctx_kernel, seq=seq),
        out_shape=out,
        grid=(n_seq // CTX_GROUP, pairs),
        in_specs=[pl.BlockSpec((rows, LANES), lambda s, p: (s, C_NQ // LANES + p)),
                  pl.BlockSpec((rows, LANES), lambda s, p: (s, C_NK // LANES + p)),
                  pl.BlockSpec((rows, LANES), lambda s, p: (s, C_NV // LANES + p))],
        out_specs=pl.BlockSpec((rows, LANES), lambda s, p: (s, p)),
        compiler_params=_cp(("parallel", "parallel"), 40),
        name="na_attn_ctx",
    )(z, z, z)


def _na_plan(rows):
    krows = min(NA_KROWS, rows)
    wr = min(NA_WIN_R, rows)
    nblk = rows // NA_QROWS
    kbase = np.zeros((nblk,), np.int32)
    drmaps = np.zeros((nblk, NA_QROWS, krows), np.int32)
    invalid = 2 * NA_WIN_R - 1
    for blk in range(nblk):
        r0 = blk * NA_QROWS
        kb = int(np.clip(r0 - wr // 2, 0, rows - krows))
        kbase[blk] = kb
        for rr in range(NA_QROWS):
            r = r0 + rr
            w0 = int(np.clip(r - wr // 2, 0, rows - wr))
            assert kb <= w0 and w0 + wr <= kb + krows
            for kk in range(krows):
                krow = kb + kk
                drmaps[blk, rr, kk] = (krow - r + NA_WIN_R - 1) if (w0 <= krow < w0 + wr) else invalid
    pats, pat_of = np.unique(drmaps, axis=0, return_inverse=True)
    return kbase, pats, np.asarray(pat_of, np.int32).reshape(nblk), krows


def _na_bias_kernel(e_ref, o_ref, *, pats):
    lo = _lane_lo()
    n_pat, qr, kr = pats.shape
    for pi in range(n_pat):
        for rr in range(qr):
            for k2 in range(kr // 2):
                a, b = int(pats[pi, rr, 2 * k2]), int(pats[pi, rr, 2 * k2 + 1])
                blk = e_ref[a] if a == b else jnp.where(lo, e_ref[a], e_ref[b])
                o_ref[pi, rr * GRID_W:(rr + 1) * GRID_W, k2 * LANES:(k2 + 1) * LANES] = blk


def _na_bias_table(rpb, pats):
    depth, h = rpb.shape[:2]
    n_dr = 2 * NA_WIN_R
    c = np.arange(GRID_W)[:, None]
    kc = np.arange(GRID_W)[None, :]
    cs = np.clip(c - NA_WIN_C // 2, 0, GRID_W - NA_WIN_C)
    valid = (kc >= cs) & (kc < cs + NA_WIN_C)
    idx = np.clip(kc - c + NA_WIN_C - 1, 0, 2 * NA_WIN_C - 2)
    e = jnp.where(valid[None, None, None], rpb[:, :, :, idx].astype(F32) * LOG2E, NEG_INF)
    e = jnp.concatenate([e, jnp.full((depth, h, 1, GRID_W, GRID_W), NEG_INF, F32)], axis=2)
    e = jnp.concatenate([e, e], axis=-1)
    n_pat, qr, kr = pats.shape
    assert kr % 2 == 0
    return pl.pallas_call(
        functools.partial(_na_bias_kernel, pats=pats),
        out_shape=jax.ShapeDtypeStruct((depth, n_pat, h, qr * GRID_W, kr * GRID_W), F32),
        grid=(depth, h),
        in_specs=[pl.BlockSpec((None, None, n_dr, GRID_W, LANES), lambda l, hd: (l, hd, 0, 0, 0))],
        out_specs=pl.BlockSpec((None, n_pat, None, qr * GRID_W, kr * GRID_W), lambda l, hd: (l, 0, hd, 0, 0)),
        compiler_params=_cp(("parallel", "parallel"), 40),
        name="na_bias",
    )(e)


def _na_lat_kernel(pat_ref, kb_ref, q_ref, k_ref, v_ref, kc_ref, vc_ref, bias_ref, prev_ref, o_ref, s_ref, *, nk):
    del pat_ref, prev_ref
    blk = pl.program_id(2)
    kstart = pl.multiple_of(kb_ref[blk] * GRID_W, GRID_W)
    lo = _lane_lo()
    tq = q_ref.shape[0]
    n_units = bias_ref.shape[0]
    chunks = _chunks((nk, kc_ref.shape[0]), ATT_CK)

    def lanes_of(unit):
        return slice((unit // 2) * LANES, (unit // 2 + 1) * LANES)

    def q_of(unit):
        q = q_ref[:, lanes_of(unit)] * HEAD64_QSCALE
        return (jnp.where(lo, q, 0.0) if unit % 2 == 0 else jnp.where(lo, 0.0, q)).astype(BF16)

    def kv_chunk(refs, unit, chunk):
        si, st, n, _ = chunk
        if si == 0:
            return refs[0][pl.ds(kstart + st, n), lanes_of(unit)].astype(BF16)
        return refs[1][st:st + n, lanes_of(unit)].astype(BF16)

    def stage_a(unit, qm, chunk, mrun):
        si, st, n, off = chunk
        s = _scores(qm, kv_chunk((k_ref, kc_ref), unit, chunk))
        if si == 0:
            s = s + bias_ref[unit, :, st:st + n]
        s_ref[unit % 2, :, off:off + n] = s
        return _tile_fold(jnp.maximum, mrun, s)

    def stage_b(unit, chunk, m, lrun, acc):
        _, _, n, off = chunk
        p = jnp.exp2(s_ref[unit % 2, :, off:off + n] - m)
        lrun = _tile_fold(jnp.add, lrun, p)
        acc = acc + jnp.dot(p.astype(BF16), kv_chunk((v_ref, vc_ref), unit, chunk), preferred_element_type=F32)
        return lrun, acc

    neg = jnp.full((tq, LANES), -jnp.inf, F32)
    zero = jnp.zeros((tq, LANES), F32)
    mrun = neg
    qm = q_of(0)
    for chunk in chunks:
        mrun = stage_a(0, qm, chunk, mrun)
    outs = []
    for unit in range(n_units):
        m = jnp.max(mrun, axis=-1, keepdims=True)
        lrun, acc, mrun = zero, zero, neg
        if unit + 1 < n_units:
            qm = q_of(unit + 1)
        for chunk in chunks:
            if unit + 1 < n_units:
                mrun = stage_a(unit + 1, qm, chunk, mrun)
            lrun, acc = stage_b(unit, chunk, m, lrun, acc)
        outs.append(acc / jnp.sum(lrun, axis=-1, keepdims=True))
    for hq in range(n_units // 2):
        o_ref[:, hq * LANES:(hq + 1) * LANES] = jnp.where(lo, outs[2 * hq], outs[2 * hq + 1]).astype(o_ref.dtype)


def _na_attn_lat(z, cache_k, cache_v, bias, plan, prev, layer, t_ctx, db, ds, past):
    kbase, _, pat_of, krows = plan
    hs = NA_STEP_HEADS
    w = hs * NA_HD
    tq = NA_QROWS * GRID_W
    nk = krows * GRID_W
    nblk = ds // tq
    q0 = t_ctx // tq
    s0 = t_ctx // ds
    grid_spec = pltpu.PrefetchScalarGridSpec(
        num_scalar_prefetch=2,
        grid=(db, NA_HEADS // hs, nblk),
        in_specs=[pl.BlockSpec((tq, w), lambda b, p, i, pat, kb: (q0 + b * nblk + i, C_NQ // w + p)),
                  pl.BlockSpec((ds, w), lambda b, p, i, pat, kb: (s0 + b, C_NK // w + p)),
                  pl.BlockSpec((ds, w), lambda b, p, i, pat, kb: (s0 + b, C_NV // w + p)),
                  pl.BlockSpec((None, None, past, w), lambda b, p, i, pat, kb: (b, layer, 0, p)),
                  pl.BlockSpec((None, None, past, w), lambda b, p, i, pat, kb: (b, layer, 0, p)),
                  pl.BlockSpec((None, None, hs, tq, nk), lambda b, p, i, pat, kb: (layer, pat[i], p, 0, 0)),
                  pl.BlockSpec(memory_space=pl.ANY)],
        out_specs=pl.BlockSpec((tq, w), lambda b, p, i, pat, kb: (q0 + b * nblk + i, p)),
        scratch_shapes=[pltpu.VMEM((2, tq, nk + past), F32)],
    )
    return pl.pallas_call(
        functools.partial(_na_lat_kernel, nk=nk),
        out_shape=jax.ShapeDtypeStruct(prev.shape, prev.dtype),
        grid_spec=grid_spec,
        input_output_aliases={8: 0},
        compiler_params=_cp(("parallel", "parallel", "arbitrary"), 56),
        name="na_attn_lat",
    )(jnp.asarray(pat_of), jnp.asarray(kbase), z, z, z, cache_k, cache_v, bias, prev)


def _pool_kernel(*refs, t):
    u_ref, w_ref, sc_ref = refs[:3]
    o_ref, pad_ref = refs[-2:]
    halo = 8
    pos = lax.broadcasted_iota(jnp.int32, (t, POOL_GROUP), 0)
    zeros = jnp.zeros((halo, POOL_GROUP), F32)
    for gi, w in enumerate(POOL_WINDOWS):
        sl = slice(gi * POOL_GROUP, (gi + 1) * POOL_GROUP)
        u = u_ref[:, sl]
        pad_ref[0:halo, :] = zeros
        pad_ref[halo + t:2 * halo + t, :] = zeros
        pad_ref[halo:halo + t, :] = u
        tot = None
        for d in range(-(w // 2), w // 2):
            part = pad_ref[halo + d:halo + d + t, :]
            tot = part if tot is None else tot + part
        cnt = (jnp.minimum(pos + w // 2, t) - jnp.maximum(pos - w // 2, 0)).astype(F32)
        pooled = (tot / cnt - u).astype(BF16)
        mixed = jnp.dot(pooled, w_ref[gi], preferred_element_type=F32)
        o_ref[:, sl] = (mixed * sc_ref[:, sl]).astype(o_ref.dtype)


def _pool(z, pool_w, pool_scale, layer, out, prev, n_seq, seq, row0, name):
    width = len(POOL_WINDOWS) * POOL_GROUP
    aliased = prev is not None
    in_specs = [pl.BlockSpec((seq, width), lambda s: (row0 + s, C_POOL // width)),
                pl.BlockSpec((None, len(POOL_WINDOWS), POOL_GROUP, POOL_GROUP), lambda s: (layer, 0, 0, 0)),
                pl.BlockSpec((1, width), lambda s: (0, 0))]
    args = [z, pool_w, pool_scale]
    if aliased:
        in_specs.append(pl.BlockSpec(memory_space=pl.ANY))
        args.append(prev)
        out = jax.ShapeDtypeStruct(prev.shape, prev.dtype)
    return pl.pallas_call(
        functools.partial(_pool_kernel, t=seq),
        out_shape=out,
        grid=(n_seq,),
        in_specs=in_specs,
        out_specs=pl.BlockSpec((seq, width), lambda s: (row0 + s, 0)),
        scratch_shapes=[pltpu.VMEM((seq + 16, POOL_GROUP), F32)],
        input_output_aliases={3: 0} if aliased else {},
        compiler_params=_cp(("parallel",), 56),
        name=name,
    )(*args)


def _diff_prep_kernel(q_ref, k_ref, v_ref, cos_ref, sin_ref, qo_ref, ko_ref, vo_ref):
    cos = cos_ref[...]
    sin = sin_ref[...]
    lane = lax.broadcasted_iota(jnp.int32, (1, LANES), 1)
    first = (lane % DIFF_HD) < DIFF_HD // 2
    for h in range(DIFF_HEADS):
        sl = slice(h * LANES, (h + 1) * LANES)
        qo_ref[:, sl] = (_rope_block(q_ref[:, sl], cos, sin, first, DIFF_HD // 2) * HEAD64_QSCALE).astype(BF16)
        ko_ref[:, sl] = _rope_block(k_ref[:, sl], cos, sin, first, DIFF_HD // 2).astype(BF16)
    vo_ref[...] = v_ref[...].astype(BF16)


def _diff_prep(z, cos_d, sin_d, rope_row_of):
    t = z.shape[0]
    tm = PREP_TM
    w = DIFF_HEADS * LANES
    sds = jax.ShapeDtypeStruct((t, w), BF16)
    return pl.pallas_call(
        _diff_prep_kernel,
        out_shape=(sds, sds, sds),
        grid=(t // tm,),
        in_specs=[pl.BlockSpec((tm, w), lambda i: (i, C_DQ // w)),
                  pl.BlockSpec((tm, w), lambda i: (i, C_DK // w)),
                  pl.BlockSpec((tm, w), lambda i: (i, C_DV // w)),
                  pl.BlockSpec((tm, LANES), lambda i: (rope_row_of(i, tm), 0)),
                  pl.BlockSpec((tm, LANES), lambda i: (rope_row_of(i, tm), 0))],
        out_specs=(pl.BlockSpec((tm, w), lambda i: (i, 0)),) * 3,
        compiler_params=_cp(("parallel",), 40),
        name="diff_prep",
    )(z, z, z, cos_d, sin_d)


def _diff_lambda(linit_ref, lam_ref):
    lam_init = linit_ref[0]
    lp = lam_ref[...]
    lam = (jnp.exp(jnp.sum(lp[0:1] * lp[1:2], axis=-1, keepdims=True))
           - jnp.exp(jnp.sum(lp[2:3] * lp[3:4], axis=-1, keepdims=True)) + lam_init)
    return lam, lam_init


def _diff_ctx_kernel(linit_ref, lam_ref, g_ref, q_ref, k_ref, v_ref, o_ref, *, seq):
    lam, lam_init = _diff_lambda(linit_ref, lam_ref)
    lo = _lane_lo()
    rows = [slice(g * seq, (g + 1) * seq) for g in range(CTX_GROUP)]
    ss = []
    for r in rows:
        q = q_ref[r, :]
        zero = jnp.zeros_like(q)
        ss.append(_scores(jnp.where(lo, q, zero), k_ref[r, :]))
        ss.append(_scores(jnp.where(lo, zero, q), k_ref[r, :]))
    ms = [jnp.max(s, axis=-1, keepdims=True) for s in ss]
    es = [jnp.exp2(s - m) for s, m in zip(ss, ms)]
    ls = [jnp.sum(e, axis=-1, keepdims=True) for e in es]
    ps = [(es[2 * g] * (1.0 / ls[2 * g]) - es[2 * g + 1] * (lam / ls[2 * g + 1])).astype(BF16)
          for g in range(CTX_GROUP)]
    os = [jnp.dot(p, v_ref[r, :], preferred_element_type=F32) for p, r in zip(ps, rows)]
    for o, r in zip(os, rows):
        o_ref[r, :] = (_rms(o, g_ref[...]) * (1.0 - lam_init)).astype(o_ref.dtype)


def _diff_attn_ctx(linit, lam_p, g, q, k, v, out, layer, n_seq, seq):
    smem = pl.BlockSpec(memory_space=pltpu.SMEM)
    rows = CTX_GROUP * seq
    return pl.pallas_call(
        functools.partial(_diff_ctx_kernel, seq=seq),
        out_shape=out,
        grid=(n_seq // CTX_GROUP, DIFF_HEADS),
        in_specs=[smem,
                  pl.BlockSpec((None, 4, DIFF_HD), lambda s, h: (layer, 0, 0)),
                  pl.BlockSpec((1, LANES), lambda s, h: (0, 0)),
                  pl.BlockSpec((rows, LANES), lambda s, h: (s, h)),
                  pl.BlockSpec((rows, LANES), lambda s, h: (s, h)),
                  pl.BlockSpec((rows, LANES), lambda s, h: (s, h))],
        out_specs=pl.BlockSpec((rows, LANES), lambda s, h: (s, h)),
        compiler_params=_cp(("parallel", "parallel"), 40),
        name="diff_attn_ctx",
    )(linit, lam_p, g, q, k, v)


def _diff_lat_kernel(linit_ref, lam_ref, g_ref, q_ref, kl_ref, kc_ref, vl_ref, vc_ref, prev_ref, o_ref, s_ref,
                     *, n_sub):
    del prev_ref
    tq = DIFF_TQ
    lam, lam_init = _diff_lambda(linit_ref, lam_ref)
    lo = _lane_lo()
    k_refs = (kl_ref, kc_ref)
    v_refs = (vl_ref, vc_ref)
    chunks = _chunks((kl_ref.shape[0], kc_ref.shape[0]), ATT_CK)

    def q_maps(qs):
        q = q_ref[qs * tq:(qs + 1) * tq, :]
        zero = jnp.zeros_like(q)
        return jnp.where(lo, q, zero), jnp.where(lo, zero, q)

    def stage_a(qms, slot, chunk, mruns):
        si, st, n, off = chunk
        k = k_refs[si][st:st + n, :].astype(BF16)
        out = []
        for mi in range(2):
            s = _scores(qms[mi], k)
            s_ref[slot, mi, :, off:off + n] = s
            out.append(_tile_fold(jnp.maximum, mruns[mi], s))
        return out

    def stage_b1(slot, chunk, ms, lruns):
        _, _, n, off = chunk
        out = []
        for mi in range(2):
            e = jnp.exp2(s_ref[slot, mi, :, off:off + n] - ms[mi])
            s_ref[slot, mi, :, off:off + n] = e
            out.append(_tile_fold(jnp.add, lruns[mi], e))
        return out

    def stage_b2(slot, chunk, rho, acc):
        si, st, n, off = chunk
        p = s_ref[slot, 0, :, off:off + n] - s_ref[slot, 1, :, off:off + n] * rho
        return acc + jnp.dot(p.astype(BF16), v_refs[si][st:st + n, :].astype(BF16), preferred_element_type=F32)

    neg = jnp.full((tq, LANES), -jnp.inf, F32)
    zero = jnp.zeros((tq, LANES), F32)
    mruns = [neg, neg]
    qms = q_maps(0)
    for chunk in chunks:
        mruns = stage_a(qms, 0, chunk, mruns)
    for qs in range(n_sub):
        slot = qs % 2
        ms = [jnp.max(mr, axis=-1, keepdims=True) for mr in mruns]
        lruns, mruns = [zero, zero], [neg, neg]
        if qs + 1 < n_sub:
            qms = q_maps(qs + 1)
        for chunk in chunks:
            if qs + 1 < n_sub:
                mruns = stage_a(qms, 1 - slot, chunk, mruns)
            lruns = stage_b1(slot, chunk, ms, lruns)
        l1 = jnp.sum(lruns[0], axis=-1, keepdims=True)
        rho = lam * l1 / jnp.sum(lruns[1], axis=-1, keepdims=True)
        acc = zero
        for chunk in chunks:
            acc = stage_b2(slot, chunk, rho, acc)
        o_ref[qs * tq:(qs + 1) * tq, :] = (_rms(acc / l1, g_ref[...]) * (1.0 - lam_init)).astype(o_ref.dtype)


def _diff_attn_lat(linit, lam_p, g, q, k, v, cache_k, cache_v, prev, layer, t_ctx, db, ds, past):
    smem = pl.BlockSpec(memory_space=pltpu.SMEM)
    n_sub = DIFF_SUBTILES
    tqb = n_sub * DIFF_TQ
    nq = ds // tqb
    q0 = t_ctx // tqb
    s0 = t_ctx // ds
    return pl.pallas_call(
        functools.partial(_diff_lat_kernel, n_sub=n_sub),
        out_shape=jax.ShapeDtypeStruct(prev.shape, prev.dtype),
        grid=(db, DIFF_HEADS, nq),
        in_specs=[smem,
                  pl.BlockSpec((None, 4, DIFF_HD), lambda b, h, i: (layer, 0, 0)),
                  pl.BlockSpec((1, LANES), lambda b, h, i: (0, 0)),
                  pl.BlockSpec((tqb, LANES), lambda b, h, i: (q0 + b * nq + i, h)),
                  pl.BlockSpec((ds, LANES), lambda b, h, i: (s0 + b, h)),
                  pl.BlockSpec((None, None, past, LANES), lambda b, h, i: (b, layer, 0, h)),
                  pl.BlockSpec((ds, LANES), lambda b, h, i: (s0 + b, h)),
                  pl.BlockSpec((None, None, past, LANES), lambda b, h, i: (b, layer, 0, h)),
                  pl.BlockSpec(memory_space=pl.ANY)],
        out_specs=pl.BlockSpec((tqb, LANES), lambda b, h, i: (q0 + b * nq + i, h)),
        scratch_shapes=[pltpu.VMEM((2, 2, DIFF_TQ, ds + past), F32)],
        input_output_aliases={8: 0},
        compiler_params=_cp(("parallel", "parallel", "arbitrary"), 56),
        name="diff_attn_lat",
    )(linit, lam_p, g, q, k, cache_k, v, cache_v, prev)


def _merge_out_kernel(h_ref, wg0_ref, wg1_ref, wg2_ref, wg3_ref, oa_ref, ob_ref, oc_ref, od_ref, wbr_ref, wo_ref,
                      x_ref, mod_ref, gn_ref, x1_ref, h2_ref, *, d):
    n = pl.program_id(1)

    @pl.when(n == 0)
    def _():
        x1_ref[...] = jnp.zeros_like(x1_ref)

    h = h_ref[...]
    merged = None
    for bi, (wg_ref, o_ref) in enumerate(((wg0_ref, oa_ref), (wg1_ref, ob_ref), (wg2_ref, oc_ref), (wg3_ref, od_ref))):
        term = (jax.nn.sigmoid(jnp.dot(h, wg_ref[...], preferred_element_type=F32))
                * jnp.dot(o_ref[...], wbr_ref[bi], preferred_element_type=F32))
        merged = term if merged is None else merged + term
    x1_ref[...] += jnp.dot(merged.astype(BF16), wo_ref[...], preferred_element_type=F32)

    @pl.when(n == pl.num_programs(1) - 1)
    def _():
        mod = mod_ref[0]
        gain_mix = gn_ref[1:2, :] * mod[:, 2 * d:3 * d]
        gain_mlp = gn_ref[2:3, :] * (1.0 + mod[:, 4 * d:5 * d])
        sh2 = mod[:, 3 * d:4 * d]
        for r0 in range(0, x_ref.shape[0], EPI_ROWS):
            rows = slice(r0, r0 + EPI_ROWS)
            x1 = x_ref[rows, :] + _rms(x1_ref[rows, :], gain_mix)
            x1_ref[rows, :] = x1
            h2_ref[rows, :] = (_rms(x1, gain_mlp) + sh2).astype(BF16)


def _merge_out(h, w_gates, outs, w_br, w_o, layer, x, mod_l, gn, row_of):
    t, d = x.shape
    bw = outs[0].shape[1]
    tm, tn = 512, MERGE_TN
    nt = d // tn
    row_spec = pl.BlockSpec((tm, d), lambda i, n: (i, 0))
    o_spec = pl.BlockSpec((tm, bw), lambda i, n: (i, 0))
    g_specs = [pl.BlockSpec((None, d, tn), functools.partial(lambda i, n, bi: (layer, 0, bi * nt + n), bi=bi))
               for bi in range(4)]
    return pl.pallas_call(
        functools.partial(_merge_out_kernel, d=d),
        out_shape=(jax.ShapeDtypeStruct((t, d), F32), jax.ShapeDtypeStruct((t, d), BF16)),
        grid=(t // tm, nt),
        in_specs=[row_spec, *g_specs,
                  o_spec, o_spec, o_spec, o_spec,
                  pl.BlockSpec((None, 4, bw, tn), lambda i, n: (layer, 0, 0, n)),
                  pl.BlockSpec((None, tn, d), lambda i, n: (layer, n, 0)),
                  row_spec,
                  pl.BlockSpec((1, 1, 6 * d), lambda i, n: (row_of(i, tm), 0, 0)),
                  pl.BlockSpec((4, d), lambda i, n: (0, 0))],
        out_specs=(row_spec, row_spec),
        compiler_params=_cp(("parallel", "arbitrary"), 56),
        name="merge_out",
    )(h, w_gates, w_gates, w_gates, w_gates, *outs, w_br, w_o, x, mod_l, gn)


def _ffn_kernel(*refs, d, emit_next):
    h_ref, wu_ref, wd_ref, x_ref, mod_ref, gn_ref = refs[:6]
    o_ref = refs[8] if emit_next else refs[6]
    f = pl.program_id(1)

    @pl.when(f == 0)
    def _():
        o_ref[...] = jnp.zeros_like(o_ref)

    u = jnp.dot(h_ref[...], wu_ref[...], preferred_element_type=F32)
    u = jnp.square(jnp.maximum(u, 0.0)).astype(BF16)
    o_ref[...] += jnp.dot(u, wd_ref[...], preferred_element_type=F32)

    @pl.when(f == pl.num_programs(1) - 1)
    def _():
        gain_out = gn_ref[3:4, :] * mod_ref[0][:, 5 * d:6 * d]
        if emit_next:
            modn_ref, gnn_ref, hn_ref = refs[6], refs[7], refs[9]
            gain_next = gnn_ref[0:1, :] * (1.0 + modn_ref[0][:, d:2 * d])
            shn = modn_ref[0][:, 0:d]
        for r0 in range(0, x_ref.shape[0], EPI_ROWS):
            rows = slice(r0, r0 + EPI_ROWS)
            x_new = x_ref[rows, :] + _rms(o_ref[rows, :], gain_out)
            o_ref[rows, :] = x_new
            if emit_next:
                hn_ref[rows, :] = (_rms(x_new, gain_next) + shn).astype(BF16)


def _ffn(h2, w_up, w_down, layer, x1, mod_l, gn, row_of, mod_next=None, gn_next=None, row0=0, n_rows=None):
    t, d = x1.shape
    n_rows = t if n_rows is None else n_rows
    dff = w_up.shape[2]
    tm, tf = 512, 1024
    i0 = row0 // tm
    emit_next = mod_next is not None
    mod_spec = pl.BlockSpec((1, 1, 6 * d), lambda i, f: (row_of(i0 + i, tm), 0, 0))
    gn_spec = pl.BlockSpec((4, d), lambda i, f: (0, 0))
    in_row_spec = pl.BlockSpec((tm, d), lambda i, f: (i0 + i, 0))
    out_row_spec = pl.BlockSpec((tm, d), lambda i, f: (i, 0))
    in_specs = [in_row_spec,
                pl.BlockSpec((None, d, tf), lambda i, f: (layer, 0, f)),
                pl.BlockSpec((None, tf, d), lambda i, f: (layer, f, 0)),
                in_row_spec, mod_spec, gn_spec]
    args = [h2, w_up, w_down, x1, mod_l, gn]
    out_shape = jax.ShapeDtypeStruct((n_rows, d), F32)
    out_specs = out_row_spec
    if emit_next:
        in_specs += [mod_spec, gn_spec]
        args += [mod_next, gn_next]
        out_shape = (out_shape, jax.ShapeDtypeStruct((n_rows, d), BF16))
        out_specs = (out_row_spec, out_row_spec)
    return pl.pallas_call(
        functools.partial(_ffn_kernel, d=d, emit_next=emit_next),
        out_shape=out_shape,
        grid=(n_rows // tm, dff // tf),
        in_specs=in_specs,
        out_specs=out_specs,
        compiler_params=_cp(("parallel", "arbitrary"), 58),
        name="ffn",
    )(*args)


def _prep_weights(w_in, w_uq, w_ukv):
    depth, d, _ = w_in.shape
    sizes = (MLA_Q_LORA, MLA_KV_LORA, MLA_ROPE, 512, 512, 512, 512, 512, 512, 512, 4 * d)
    offs = np.cumsum((0,) + sizes)
    part = lambda i: w_in[:, :, offs[i]:offs[i + 1]]
    q_c, kv_c, k_r, na_q, na_k, na_v, pool, dq, dk, dv, gates = (part(i) for i in range(11))
    kr_blk = jnp.pad(k_r, ((0, 0), (0, 0), (MLA_NOPE, LANES - MLA_NOPE - MLA_ROPE)))
    w_all = jnp.concatenate([q_c, dq, dk, dv, na_q, na_k, na_v, pool, kv_c, kr_blk], axis=-1).astype(BF16)
    w_gates = gates.astype(BF16)
    hd = MLA_NOPE + MLA_ROPE
    wuq = jnp.pad(w_uq.reshape(depth, MLA_Q_LORA, MLA_HEADS, hd), ((0, 0), (0, 0), (0, 0), (0, LANES - hd)))
    wuq = wuq.reshape(depth, MLA_Q_LORA, MLA_HEADS * LANES).astype(BF16)
    wkv4 = w_ukv.reshape(depth, MLA_KV_LORA, MLA_HEADS, MLA_NOPE + MLA_V)
    wk = jnp.pad(wkv4[..., :MLA_NOPE], ((0, 0), (0, 0), (0, 0), (0, LANES - MLA_NOPE)))
    wk = wk.reshape(depth, MLA_KV_LORA, MLA_HEADS * LANES)
    wv = wkv4[..., MLA_NOPE:].reshape(depth, MLA_KV_LORA, MLA_HEADS * MLA_V)
    wkv = jnp.concatenate([wk, wv], axis=-1).astype(BF16)
    return w_all, w_gates, wuq, wkv


def kernel(x_prompt, x_sample, cache_mla_ckv, cache_mla_krope, cache_na_k, cache_na_v, cache_diff_k, cache_diff_v, c, c_ctx, w_mod, b_mod, g_norm, w_in, g_q_lora, g_kv_lora, w_uq, w_ukv, na_rpb, pool_w, pool_scale, diff_lambda, diff_norm_g, w_br, w_o, w_up, w_down):
    nb, seq, d = x_prompt.shape
    db, ds, _ = x_sample.shape
    depth = w_in.shape[0]
    past = cache_mla_ckv.shape[2]
    t_ctx = nb * seq
    t_lat = db * ds
    t = t_ctx + t_lat
    assert t_ctx % ds == 0 and ds % 1024 == 0 and t_ctx % 1024 == 0 and seq % 8 == 0 and nb % CTX_GROUP == 0

    def row_of(i, tm):
        n_ctx = t_ctx // tm
        return jnp.where(i < n_ctx, 0, 1 + (i - n_ctx) // (ds // tm))

    def rope_row_of(i, tm):
        n_ctx = t_ctx // tm
        return jnp.where(i < n_ctx, 0, 1 + (i - n_ctx) % (ds // tm))

    w_all, w_gates, wuq, wkv = _prep_weights(w_in, w_uq, w_ukv)
    w_br_b = w_br.astype(BF16)
    w_o_b = w_o.astype(BF16)
    w_up_b = w_up.astype(BF16)
    w_down_b = w_down.astype(BF16)
    pool_w_b = pool_w.astype(BF16)
    cos_m, sin_m, cos_d, sin_d = _rope_tables(ds, PREP_TM)
    rows = ds // GRID_W
    na_plan = _na_plan(rows)
    bias = _na_bias_table(na_rpb, na_plan[1])
    cache_kr_pad = jnp.pad(cache_mla_krope, ((0, 0), (0, 0), (0, 0), (MLA_NOPE, LANES - MLA_NOPE - MLA_ROPE)))
    cache_nk = cache_na_k.reshape(db, depth, past, NA_HEADS * NA_HD)
    cache_nv = cache_na_v.reshape(db, depth, past, NA_HEADS * NA_HD)
    cache_dk = cache_diff_k.reshape(db, depth, past, DIFF_HEADS * 2 * DIFF_HD)
    cache_dv = cache_diff_v.reshape(db, depth, past, DIFF_HEADS * 2 * DIFF_HD)

    n_rows = 1 + db
    r_pad = -(-n_rows // 8) * 8
    cond = jnp.concatenate([c_ctx[None, :], c, jnp.zeros((r_pad - n_rows, d), F32)], axis=0)
    mod = _modulation(cond, w_mod, b_mod)
    mods = [mod[li].reshape(r_pad, 1, 6 * d) for li in range(depth)]

    branch_sds = jax.ShapeDtypeStruct((t, 512), BF16)
    x, h = _norm_mod(x_prompt.reshape(t_ctx, d), x_sample.reshape(t_lat, d), mods[0], g_norm[0][0:1], row_of)
    states = []
    for li in range(depth):
        mod_l = mods[li]
        gn = g_norm[li]
        z = _matmul(h, w_all, li, 1024, 896, F32, "in_proj")

        q_a, ckv, k_a, v_a = _mla_prep(z, cos_m, sin_m, g_q_lora[li][None, :], g_kv_lora[li][None, :],
                                       wuq, wkv, li, rope_row_of)
        kc_a, vc_a = _mla_cache(cache_mla_ckv, cache_kr_pad, wkv, li)
        o_a = _mla_attn_ctx(q_a, k_a, v_a, branch_sds, nb, seq)
        o_a = _mla_attn_lat(q_a, k_a, v_a, kc_a, vc_a, o_a, t_ctx, db, ds, past)

        o_b = _na_attn_ctx(z, branch_sds, nb, seq)
        o_b = _na_attn_lat(z, cache_nk, cache_nv, bias, na_plan, o_b, li, t_ctx, db, ds, past)

        o_c = _pool(z, pool_w_b, pool_scale[li][None, :], li, branch_sds, None, nb, seq, 0, "pool_ctx")
        o_c = _pool(z, pool_w_b, pool_scale[li][None, :], li, None, o_c, db, ds, t_ctx // ds, "pool_lat")

        q_d, k_d, v_d = _diff_prep(z, cos_d, sin_d, rope_row_of)
        linit = jnp.full((1,), 0.8 - 0.6 * math.exp(-0.3 * li), F32)
        g_d = diff_norm_g[li][None, :]
        o_d = _diff_attn_ctx(linit, diff_lambda, g_d, q_d, k_d, v_d, branch_sds, li, nb, seq)
        o_d = _diff_attn_lat(linit, diff_lambda, g_d, q_d, k_d, v_d, cache_dk, cache_dv, o_d,
                             li, t_ctx, db, ds, past)

        x1, h2 = _merge_out(h, w_gates, (o_a, o_b, o_c, o_d), w_br_b, w_o_b, li, x, mod_l, gn, row_of)
        if li + 1 < depth:
            x, h = _ffn(h2, w_up_b, w_down_b, li, x1, mod_l, gn, row_of, mods[li + 1], g_norm[li + 1])
        else:
            y_prompt = _ffn(h2, w_up_b, w_down_b, li, x1, mod_l, gn, row_of, row0=0, n_rows=t_ctx)
            y_sample = _ffn(h2, w_up_b, w_down_b, li, x1, mod_l, gn, row_of, row0=t_ctx, n_rows=t_lat)

        zc = z[:t_ctx]
        states.append((ckv[:t_ctx], zc[:, C_KR + MLA_NOPE:C_KR + MLA_NOPE + MLA_ROPE], zc[:, C_NK:C_NK + 512],
                       zc[:, C_NV:C_NV + 512], zc[:, C_DK:C_DK + 512], zc[:, C_DV:C_DV + 512]))

    def stacked(k, tail):
        per_layer = [s[k].reshape(nb, seq, -1) for s in states]
        return jnp.stack(per_layer, axis=1).reshape(nb, depth, seq, *tail)

    return (y_prompt.reshape(nb, seq, d), y_sample.reshape(db, ds, d),
            stacked(0, (MLA_KV_LORA,)), stacked(1, (MLA_ROPE,)),
            stacked(2, (NA_HEADS, NA_HD)), stacked(3, (NA_HEADS, NA_HD)),
            stacked(4, (DIFF_HEADS, 2 * DIFF_HD)), stacked(5, (DIFF_HEADS, 2 * DIFF_HD)))
```

```python
import functools
import math

import numpy as np
import jax
import jax.numpy as jnp
from jax import lax
from jax.experimental import pallas as pl
from jax.experimental.pallas import tpu as pltpu

F32 = jnp.float32
BF16 = jnp.bfloat16

GRID_W = 64
MLA_HEADS = 8
MLA_NOPE = 64
MLA_ROPE = 32
MLA_V = 64
MLA_Q_LORA = 512
MLA_KV_LORA = 256
NA_HEADS = 8
NA_HD = 64
NA_WIN_R = 8
NA_WIN_C = 16
POOL_WINDOWS = (2, 4, 8, 16)
POOL_GROUP = 128
DIFF_HEADS = 4
DIFF_HD = 64
ROPE_THETA = 10000.0
RMS_EPS = 1e-6
NEG_INF = -1e30
LOG2E = math.log2(math.e)
MLA_QSCALE = (MLA_NOPE + MLA_ROPE) ** -0.5 * LOG2E
HEAD64_QSCALE = 64 ** -0.5 * LOG2E
LANES = 128
MIB = 1024 * 1024

C_QC = 0
C_DQ = 512
C_DK = 1024
C_DV = 1536
C_NQ = 2048
C_NK = 2560
C_NV = 3072
C_POOL = 3584
C_KVC = 4096
C_KR = 4352
Z_COLS = 4480
GATE_COL0 = 4608

MERGE_TN = 512
NA_STEP_HEADS = 4
NA_QROWS = 8
NA_KROWS = 16
ATT_TQ = 256
DIFF_TQ = 256
DIFF_SUBTILES = 4
ATT_CK = 512
CTX_GROUP = 4
EPI_ROWS = 16
PREP_TM = 1024


def _cp(sem, vmem_mib):
    return pltpu.CompilerParams(dimension_semantics=sem, vmem_limit_bytes=vmem_mib * MIB)


def _rms(x, g):
    return x * lax.rsqrt(jnp.mean(x * x, axis=-1, keepdims=True) + RMS_EPS) * g


def _scores(q, k):
    return lax.dot_general(q, k, (((1,), (1,)), ((), ())), preferred_element_type=F32)


def _lane_lo():
    return lax.broadcasted_iota(jnp.int32, (1, LANES), 1) < 64


def _tile_fold(op, run, x):
    for c in range(x.shape[1] // LANES):
        run = op(run, x[:, c * LANES:(c + 1) * LANES])
    return run


def _chunks(sizes, ck):
    out, off = [], 0
    for si, n in enumerate(sizes):
        step = min(ck, n)
        for st in range(0, n, step):
            out.append((si, st, step, off))
            off += step
    return out


def _mod_kernel(c_ref, w_ref, b_ref, o_ref):
    c = c_ref[...]
    s = c * jax.nn.sigmoid(c)
    o_ref[0] = jnp.dot(s.astype(BF16), w_ref[0].astype(BF16), preferred_element_type=F32) + b_ref[0]


def _modulation(cond, w_mod, b_mod):
    depth, d, n = w_mod.shape
    r = cond.shape[0]
    tn = 1024
    return pl.pallas_call(
        _mod_kernel,
        out_shape=jax.ShapeDtypeStruct((depth, r, n), F32),
        grid=(depth, n // tn),
        in_specs=[pl.BlockSpec((r, d), lambda l, j: (0, 0)),
                  pl.BlockSpec((1, d, tn), lambda l, j: (l, 0, j)),
                  pl.BlockSpec((1, 1, tn), lambda l, j: (l, 0, j))],
        out_specs=pl.BlockSpec((1, r, tn), lambda l, j: (l, 0, j)),
        compiler_params=_cp(("parallel", "parallel"), 40),
        name="modulation",
    )(cond, w_mod, b_mod.reshape(depth, 1, n))


def _norm_mod_kernel(xp_ref, xs_ref, mod_ref, g_ref, x_ref, h_ref, *, d, n_ctx):
    mod = mod_ref[0]
    sh = mod[:, 0:d]
    sc = mod[:, d:2 * d]

    def emit(src_ref):
        x = src_ref[...]
        x_ref[...] = x
        h_ref[...] = (_rms(x, g_ref[...]) * (1.0 + sc) + sh).astype(BF16)

    @pl.when(pl.program_id(0) < n_ctx)
    def _():
        emit(xp_ref)

    @pl.when(pl.program_id(0) >= n_ctx)
    def _():
        emit(xs_ref)


def _norm_mod(xp, xs, mod_l, g, row_of):
    (t_ctx, d), t_lat = xp.shape, xs.shape[0]
    tm = 512
    n_ctx = t_ctx // tm
    t = t_ctx + t_lat
    return pl.pallas_call(
        functools.partial(_norm_mod_kernel, d=d, n_ctx=n_ctx),
        out_shape=(jax.ShapeDtypeStruct((t, d), F32), jax.ShapeDtypeStruct((t, d), BF16)),
        grid=(t // tm,),
        in_specs=[pl.BlockSpec((tm, d), lambda i: (jnp.minimum(i, n_ctx - 1), 0)),
                  pl.BlockSpec((tm, d), lambda i: (jnp.maximum(i - n_ctx, 0), 0)),
                  pl.BlockSpec((1, 1, 6 * d), lambda i: (row_of(i, tm), 0, 0)),
                  pl.BlockSpec((1, d), lambda i: (0, 0))],
        out_specs=(pl.BlockSpec((tm, d), lambda i: (i, 0)), pl.BlockSpec((tm, d), lambda i: (i, 0))),
        compiler_params=_cp(("parallel",), 48),
        name="norm_mod",
    )(xp, xs, mod_l, g)


def _mm_kernel(a_ref, b_ref, o_ref):
    o_ref[...] = jnp.dot(a_ref[...], b_ref[...], preferred_element_type=F32).astype(o_ref.dtype)


def _matmul(a, b, layer, n, tm, tn, out_dtype, name):
    m, k = a.shape
    return pl.pallas_call(
        _mm_kernel,
        out_shape=jax.ShapeDtypeStruct((m, n), out_dtype),
        grid=(m // tm, n // tn),
        in_specs=[pl.BlockSpec((tm, k), lambda i, j: (i, 0)),
                  pl.BlockSpec((None, k, tn), lambda i, j: (layer, 0, j))],
        out_specs=pl.BlockSpec((tm, tn), lambda i, j: (i, j)),
        compiler_params=_cp(("parallel", "arbitrary"), 48),
        name=name,
    )(a, b)


def _rope_block(x, cos, sin, first_half, half):
    partner = jnp.where(first_half, -pltpu.roll(x, LANES - half, 1), pltpu.roll(x, half, 1))
    return x * cos + partner * sin


def _rope_tables(n_lat, ident_rows):
    t = jnp.arange(n_lat)
    row = (t // GRID_W).astype(F32)
    col = (t % GRID_W).astype(F32)

    def angles(rot_dim):
        n_freq = rot_dim // 4
        inv = ROPE_THETA ** (-jnp.arange(n_freq, dtype=F32) / n_freq)
        return jnp.concatenate([row[:, None] * inv, col[:, None] * inv], axis=-1)

    a_m = angles(MLA_ROPE)
    zeros64 = jnp.zeros((n_lat, 64), F32)
    zeros32 = jnp.zeros((n_lat, 32), F32)
    cos_m = jnp.concatenate([zeros64 + 1.0, jnp.cos(a_m), jnp.cos(a_m), zeros32 + 1.0], axis=-1)
    sin_m = jnp.concatenate([zeros64, jnp.sin(a_m), jnp.sin(a_m), zeros32], axis=-1)
    a_d = angles(DIFF_HD)
    cos_d = jnp.tile(jnp.cos(a_d), (1, 4))
    sin_d = jnp.tile(jnp.sin(a_d), (1, 4))
    one = jnp.ones((ident_rows, LANES), F32)
    zero = jnp.zeros((ident_rows, LANES), F32)
    cat = lambda a, b: jnp.concatenate([a, b], axis=0)
    return cat(one, cos_m), cat(zero, sin_m), cat(one, cos_d), cat(zero, sin_d)


def _mla_prep_kernel(qc_ref, kvc_ref, kr_ref, cos_ref, sin_ref, gq_ref, gkv_ref, wuq_ref, wkv_ref,
                     q_ref, ckv_ref, k_ref, v_ref):
    cos = cos_ref[...]
    sin = sin_ref[...]
    lane = lax.broadcasted_iota(jnp.int32, (1, LANES), 1)
    first = lane < MLA_NOPE + MLA_ROPE // 2
    rope = lambda x: _rope_block(x, cos, sin, first, MLA_ROPE // 2)
    qn = _rms(qc_ref[...], gq_ref[...]).astype(BF16)
    q = jnp.dot(qn, wuq_ref[...], preferred_element_type=F32)
    for h in range(MLA_HEADS):
        sl = slice(h * LANES, (h + 1) * LANES)
        q_ref[:, sl] = (rope(q[:, sl]) * MLA_QSCALE).astype(BF16)
    ckv = _rms(kvc_ref[...], gkv_ref[...])
    ckv_ref[...] = ckv
    kv = jnp.dot(ckv.astype(BF16), wkv_ref[...], preferred_element_type=F32)
    krr = rope(kr_ref[...])
    for h in range(MLA_HEADS):
        sl = slice(h * LANES, (h + 1) * LANES)
        k_ref[:, sl] = (kv[:, sl] + krr).astype(BF16)
    v_ref[...] = kv[:, MLA_HEADS * LANES:].astype(BF16)


def _mla_prep(z, cos_m, sin_m, gq, gkv, wuq, wkv, layer, rope_row_of):
    t = z.shape[0]
    tm = PREP_TM
    kw = MLA_HEADS * LANES
    vw = MLA_HEADS * MLA_V
    return pl.pallas_call(
        _mla_prep_kernel,
        out_shape=(jax.ShapeDtypeStruct((t, kw), BF16), jax.ShapeDtypeStruct((t, MLA_KV_LORA), F32),
                   jax.ShapeDtypeStruct((t, kw), BF16), jax.ShapeDtypeStruct((t, vw), BF16)),
        grid=(t // tm,),
        in_specs=[pl.BlockSpec((tm, MLA_Q_LORA), lambda i: (i, C_QC // MLA_Q_LORA)),
                  pl.BlockSpec((tm, MLA_KV_LORA), lambda i: (i, C_KVC // MLA_KV_LORA)),
                  pl.BlockSpec((tm, LANES), lambda i: (i, C_KR // LANES)),
                  pl.BlockSpec((tm, LANES), lambda i: (rope_row_of(i, tm), 0)),
                  pl.BlockSpec((tm, LANES), lambda i: (rope_row_of(i, tm), 0)),
                  pl.BlockSpec((1, MLA_Q_LORA), lambda i: (0, 0)),
                  pl.BlockSpec((1, MLA_KV_LORA), lambda i: (0, 0)),
                  pl.BlockSpec((None, MLA_Q_LORA, kw), lambda i: (layer, 0, 0)),
                  pl.BlockSpec((None, MLA_KV_LORA, kw + vw), lambda i: (layer, 0, 0))],
        out_specs=(pl.BlockSpec((tm, kw), lambda i: (i, 0)),
                   pl.BlockSpec((tm, MLA_KV_LORA), lambda i: (i, 0)),
                   pl.BlockSpec((tm, kw), lambda i: (i, 0)),
                   pl.BlockSpec((tm, vw), lambda i: (i, 0))),
        compiler_params=_cp(("parallel",), 56),
        name="mla_prep",
    )(z, z, z, cos_m, sin_m, gq, gkv, wuq, wkv)


def _mla_cache_kernel(ckv_ref, kr_ref, wkv_ref, k_ref, v_ref):
    kv = jnp.dot(ckv_ref[...].astype(BF16), wkv_ref[...], preferred_element_type=F32)
    kr = kr_ref[...]
    for h in range(MLA_HEADS):
        sl = slice(h * LANES, (h + 1) * LANES)
        k_ref[:, sl] = (kv[:, sl] + kr).astype(BF16)
    v_ref[...] = kv[:, MLA_HEADS * LANES:].astype(BF16)


def _mla_cache(cache_ckv, cache_kr_pad, wkv, layer):
    db, _, past, _ = cache_ckv.shape
    kw = MLA_HEADS * LANES
    vw = MLA_HEADS * MLA_V
    return pl.pallas_call(
        _mla_cache_kernel,
        out_shape=(jax.ShapeDtypeStruct((db * past, kw), BF16), jax.ShapeDtypeStruct((db * past, vw), BF16)),
        grid=(db,),
        in_specs=[pl.BlockSpec((None, None, past, MLA_KV_LORA), lambda b: (b, layer, 0, 0)),
                  pl.BlockSpec((None, None, past, LANES), lambda b: (b, layer, 0, 0)),
                  pl.BlockSpec((None, MLA_KV_LORA, kw + vw), lambda b: (layer, 0, 0))],
        out_specs=(pl.BlockSpec((past, kw), lambda b: (b, 0)), pl.BlockSpec((past, vw), lambda b: (b, 0))),
        compiler_params=_cp(("parallel",), 40),
        name="mla_cache",
    )(cache_ckv, cache_kr_pad, wkv)


def _phased_softmax_pv(ss, vs):
    ms = [jnp.max(s, axis=-1, keepdims=True) for s in ss]
    ps = [jnp.exp2(s - m) for s, m in zip(ss, ms)]
    ls = [jnp.sum(p, axis=-1, keepdims=True) for p in ps]
    return [jnp.dot(p.astype(BF16), v, preferred_element_type=F32) / l for p, v, l in zip(ps, vs, ls)]


def _mla_ctx_kernel(q_ref, k_ref, v_ref, o_ref, *, seq):
    lo = _lane_lo()
    rows = [slice(g * seq, (g + 1) * seq) for g in range(CTX_GROUP)]
    lanes = [slice(hh * LANES, (hh + 1) * LANES) for hh in range(2)]
    units = [(r, sl) for r in rows for sl in lanes]
    outs = _phased_softmax_pv([_scores(q_ref[r, sl], k_ref[r, sl]) for r, sl in units],
                              [v_ref[r, :] for r, _ in units])
    for g, r in enumerate(rows):
        o_ref[r, :] = jnp.where(lo, outs[2 * g], outs[2 * g + 1]).astype(o_ref.dtype)


def _mla_attn_ctx(q, k, v, out, n_seq, seq):
    pairs = MLA_HEADS // 2
    rows = CTX_GROUP * seq
    return pl.pallas_call(
        functools.partial(_mla_ctx_kernel, seq=seq),
        out_shape=out,
        grid=(n_seq // CTX_GROUP, pairs),
        in_specs=[pl.BlockSpec((rows, 2 * LANES), lambda s, p: (s, p)),
                  pl.BlockSpec((rows, 2 * LANES), lambda s, p: (s, p)),
                  pl.BlockSpec((rows, LANES), lambda s, p: (s, p))],
        out_specs=pl.BlockSpec((rows, LANES), lambda s, p: (s, p)),
        compiler_params=_cp(("parallel", "parallel"), 40),
        name="mla_attn_ctx",
    )(q, k, v)


def _mla_lat_kernel(q_ref, kl_ref, kc_ref, vl_ref, vc_ref, prev_ref, o_ref, s_ref, *, n_sub):
    del prev_ref
    tq = ATT_TQ
    k_refs = (kl_ref, kc_ref)
    v_refs = (vl_ref, vc_ref)
    chunks = _chunks((kl_ref.shape[0], kc_ref.shape[0]), ATT_CK)
    units = [(qs, hh) for qs in range(n_sub) for hh in range(2)]

    def stage_a(unit, slot, chunk, mrun):
        qs, hh = unit
        si, st, n, off = chunk
        sl = slice(hh * LANES, (hh + 1) * LANES)
        s = _scores(q_ref[qs * tq:(qs + 1) * tq, sl], k_refs[si][st:st + n, sl])
        s_ref[slot, :, off:off + n] = s
        return _tile_fold(jnp.maximum, mrun, s)

    def stage_b(slot, chunk, m, lrun, acc):
        si, st, n, off = chunk
        p = jnp.exp2(s_ref[slot, :, off:off + n] - m)
        lrun = _tile_fold(jnp.add, lrun, p)
        acc = acc + jnp.dot(p.astype(BF16), v_refs[si][st:st + n, :], preferred_element_type=F32)
        return lrun, acc

    neg = jnp.full((tq, LANES), -jnp.inf, F32)
    zero = jnp.zeros((tq, LANES), F32)
    mrun = neg
    for chunk in chunks:
        mrun = stage_a(units[0], 0, chunk, mrun)
    outs = {}
    for ui, unit in enumerate(units):
        slot = ui % 2
        m = jnp.max(mrun, axis=-1, keepdims=True)
        lrun, acc, mrun = zero, zero, neg
        for chunk in chunks:
            if ui + 1 < len(units):
                mrun = stage_a(units[ui + 1], 1 - slot, chunk, mrun)
            lrun, acc = stage_b(slot, chunk, m, lrun, acc)
        outs[unit] = acc / jnp.sum(lrun, axis=-1, keepdims=True)
    lo = _lane_lo()
    for qs in range(n_sub):
        o_ref[qs * tq:(qs + 1) * tq, :] = jnp.where(lo, outs[(qs, 0)], outs[(qs, 1)]).astype(o_ref.dtype)


def _mla_attn_lat(q, k, v, kc, vc, prev, t_ctx, db, ds, past):
    pairs = MLA_HEADS // 2
    n_sub = 2
    tqb = n_sub * ATT_TQ
    nq = ds // tqb
    q0 = t_ctx // tqb
    s0 = t_ctx // ds
    return pl.pallas_call(
        functools.partial(_mla_lat_kernel, n_sub=n_sub),
        out_shape=jax.ShapeDtypeStruct(prev.shape, prev.dtype),
        grid=(db, pairs, nq),
        in_specs=[pl.BlockSpec((tqb, 2 * LANES), lambda b, p, i: (q0 + b * nq + i, p)),
                  pl.BlockSpec((ds, 2 * LANES), lambda b, p, i: (s0 + b, p)),
                  pl.BlockSpec((past, 2 * LANES), lambda b, p, i: (b, p)),
                  pl.BlockSpec((ds, LANES), lambda b, p, i: (s0 + b, p)),
                  pl.BlockSpec((past, LANES), lambda b, p, i: (b, p)),
                  pl.BlockSpec(memory_space=pl.ANY)],
        out_specs=pl.BlockSpec((tqb, LANES), lambda b, p, i: (q0 + b * nq + i, p)),
        scratch_shapes=[pltpu.VMEM((2, ATT_TQ, ds + past), F32)],
        input_output_aliases={5: 0},
        compiler_params=_cp(("parallel", "parallel", "arbitrary"), 56),
        name="mla_attn_lat",
    )(q, k, kc, v, vc, prev)


def _attn64_ctx_kernel(q_ref, k_ref, v_ref, o_ref, *, seq):
    lo = _lane_lo()
    rows = [slice(g * seq, (g + 1) * seq) for g in range(CTX_GROUP)]
    qs = [q_ref[r, :] * HEAD64_QSCALE for r in rows]
    ks = [k_ref[r, :].astype(BF16) for r in rows]
    vs = [v_ref[r, :].astype(BF16) for r in rows]
    ss, vv = [], []
    for q, k, v in zip(qs, ks, vs):
        ss.append(_scores(jnp.where(lo, q, 0.0).astype(BF16), k))
        ss.append(_scores(jnp.where(lo, 0.0, q).astype(BF16), k))
        vv += [v, v]
    outs = _phased_softmax_pv(ss, vv)
    for g, r in enumerate(rows):
        o_ref[r, :] = jnp.where(lo, outs[2 * g], outs[2 * g + 1]).astype(o_ref.dtype)


def _na_attn_ctx(z, out, n_seq, seq):
    pairs = NA_HEADS // 2
    rows = CTX_GROUP * seq
    return pl.pallas_call(
        functools.partial(_attn64_ctx_kernel, seq=seq),
        out_shape=out,
        grid=(n_seq // CTX_GROUP, pairs),
        in_specs=[pl.BlockSpec((rows, LANES), lambda s, p: (s, C_NQ // LANES + p)),
                  pl.BlockSpec((rows, LANES), lambda s, p: (s, C_NK // LANES + p)),
                  pl.BlockSpec((rows, LANES), lambda s, p: (s, C_NV // LANES + p))],
        out_specs=pl.BlockSpec((rows, LANES), lambda s, p: (s, p)),
        compiler_params=_cp(("parallel", "parallel"), 40),
        name="na_attn_ctx",
    )(z, z, z)


def _na_plan(rows):
    krows = min(NA_KROWS, rows)
    wr = min(NA_WIN_R, rows)
    nblk = rows // NA_QROWS
    kbase = np.zeros((nblk,), np.int32)
    drmaps = np.zeros((nblk, NA_QROWS, krows), np.int32)
    invalid = 2 * NA_WIN_R - 1
    for blk in range(nblk):
        r0 = blk * NA_QROWS
        kb = int(np.clip(r0 - wr // 2, 0, rows - krows))
        kbase[blk] = kb
        for rr in range(NA_QROWS):
            r = r0 + rr
            w0 = int(np.clip(r - wr // 2, 0, rows - wr))
            assert kb <= w0 and w0 + wr <= kb + krows
            for kk in range(krows):
                krow = kb + kk
                drmaps[blk, rr, kk] = (krow - r + NA_WIN_R - 1) if (w0 <= krow < w0 + wr) else invalid
    pats, pat_of = np.unique(drmaps, axis=0, return_inverse=True)
    return kbase, pats, np.asarray(pat_of, np.int32).reshape(nblk), krows


def _na_bias_kernel(e_ref, o_ref, *, pats):
    lo = _lane_lo()
    n_pat, qr, kr = pats.shape
    for pi in range(n_pat):
        for rr in range(qr):
            for k2 in range(kr // 2):
                a, b = int(pats[pi, rr, 2 * k2]), int(pats[pi, rr, 2 * k2 + 1])
                blk = e_ref[a] if a == b else jnp.where(lo, e_ref[a], e_ref[b])
                o_ref[pi, rr * GRID_W:(rr + 1) * GRID_W, k2 * LANES:(k2 + 1) * LANES] = blk


def _na_bias_table(rpb, pats):
    depth, h = rpb.shape[:2]
    n_dr = 2 * NA_WIN_R
    c = np.arange(GRID_W)[:, None]
    kc = np.arange(GRID_W)[None, :]
    cs = np.clip(c - NA_WIN_C // 2, 0, GRID_W - NA_WIN_C)
    valid = (kc >= cs) & (kc < cs + NA_WIN_C)
    idx = np.clip(kc - c + NA_WIN_C - 1, 0, 2 * NA_WIN_C - 2)
    e = jnp.where(valid[None, None, None], rpb[:, :, :, idx].astype(F32) * LOG2E, NEG_INF)
    e = jnp.concatenate([e, jnp.full((depth, h, 1, GRID_W, GRID_W), NEG_INF, F32)], axis=2)
    e = jnp.concatenate([e, e], axis=-1)
    n_pat, qr, kr = pats.shape
    assert kr % 2 == 0
    return pl.pallas_call(
        functools.partial(_na_bias_kernel, pats=pats),
        out_shape=jax.ShapeDtypeStruct((depth, n_pat, h, qr * GRID_W, kr * GRID_W), F32),
        grid=(depth, h),
        in_specs=[pl.BlockSpec((None, None, n_dr, GRID_W, LANES), lambda l, hd: (l, hd, 0, 0, 0))],
        out_specs=pl.BlockSpec((None, n_pat, None, qr * GRID_W, kr * GRID_W), lambda l, hd: (l, 0, hd, 0, 0)),
        compiler_params=_cp(("parallel", "parallel"), 40),
        name="na_bias",
    )(e)


def _na_lat_kernel(pat_ref, kb_ref, q_ref, k_ref, v_ref, kc_ref, vc_ref, bias_ref, prev_ref, o_ref, s_ref, *, nk):
    del pat_ref, prev_ref
    blk = pl.program_id(2)
    kstart = pl.multiple_of(kb_ref[blk] * GRID_W, GRID_W)
    lo = _lane_lo()
    tq = q_ref.shape[0]
    n_units = bias_ref.shape[0]
    chunks = _chunks((nk, kc_ref.shape[0]), ATT_CK)

    def lanes_of(unit):
        return slice((unit // 2) * LANES, (unit // 2 + 1) * LANES)

    def q_of(unit):
        q = q_ref[:, lanes_of(unit)] * HEAD64_QSCALE
        return (jnp.where(lo, q, 0.0) if unit % 2 == 0 else jnp.where(lo, 0.0, q)).astype(BF16)

    def kv_chunk(refs, unit, chunk):
        si, st, n, _ = chunk
        if si == 0:
            return refs[0][pl.ds(kstart + st, n), lanes_of(unit)].astype(BF16)
        return refs[1][st:st + n, lanes_of(unit)].astype(BF16)

    def stage_a(unit, qm, chunk, mrun):
        si, st, n, off = chunk
        s = _scores(qm, kv_chunk((k_ref, kc_ref), unit, chunk))
        if si == 0:
            s = s + bias_ref[unit, :, st:st + n]
        s_ref[unit % 2, :, off:off + n] = s
        return _tile_fold(jnp.maximum, mrun, s)

    def stage_b(unit, chunk, m, lrun, acc):
        _, _, n, off = chunk
        p = jnp.exp2(s_ref[unit % 2, :, off:off + n] - m)
        lrun = _tile_fold(jnp.add, lrun, p)
        acc = acc + jnp.dot(p.astype(BF16), kv_chunk((v_ref, vc_ref), unit, chunk), preferred_element_type=F32)
        return lrun, acc

    neg = jnp.full((tq, LANES), -jnp.inf, F32)
    zero = jnp.zeros((tq, LANES), F32)
    mrun = neg
    qm = q_of(0)
    for chunk in chunks:
        mrun = stage_a(0, qm, chunk, mrun)
    outs = []
    for unit in range(n_units):
        m = jnp.max(mrun, axis=-1, keepdims=True)
        lrun, acc, mrun = zero, zero, neg
        if unit + 1 < n_units:
            qm = q_of(unit + 1)
        for chunk in chunks:
            if unit + 1 < n_units:
                mrun = stage_a(unit + 1, qm, chunk, mrun)
            lrun, acc = stage_b(unit, chunk, m, lrun, acc)
        outs.append(acc / jnp.sum(lrun, axis=-1, keepdims=True))
    for hq in range(n_units // 2):
        o_ref[:, hq * LANES:(hq + 1) * LANES] = jnp.where(lo, outs[2 * hq], outs[2 * hq + 1]).astype(o_ref.dtype)


def _na_attn_lat(z, cache_k, cache_v, bias, plan, prev, layer, t_ctx, db, ds, past):
    kbase, _, pat_of, krows = plan
    hs = NA_STEP_HEADS
    w = hs * NA_HD
    tq = NA_QROWS * GRID_W
    nk = krows * GRID_W
    nblk = ds // tq
    q0 = t_ctx // tq
    s0 = t_ctx // ds
    grid_spec = pltpu.PrefetchScalarGridSpec(
        num_scalar_prefetch=2,
        grid=(db, NA_HEADS // hs, nblk),
        in_specs=[pl.BlockSpec((tq, w), lambda b, p, i, pat, kb: (q0 + b * nblk + i, C_NQ // w + p)),
                  pl.BlockSpec((ds, w), lambda b, p, i, pat, kb: (s0 + b, C_NK // w + p)),
                  pl.BlockSpec((ds, w), lambda b, p, i, pat, kb: (s0 + b, C_NV // w + p)),
                  pl.BlockSpec((None, None, past, w), lambda b, p, i, pat, kb: (b, layer, 0, p)),
                  pl.BlockSpec((None, None, past, w), lambda b, p, i, pat, kb: (b, layer, 0, p)),
                  pl.BlockSpec((None, None, hs, tq, nk), lambda b, p, i, pat, kb: (layer, pat[i], p, 0, 0)),
                  pl.BlockSpec(memory_space=pl.ANY)],
        out_specs=pl.BlockSpec((tq, w), lambda b, p, i, pat, kb: (q0 + b * nblk + i, p)),
        scratch_shapes=[pltpu.VMEM((2, tq, nk + past), F32)],
    )
    return pl.pallas_call(
        functools.partial(_na_lat_kernel, nk=nk),
        out_shape=jax.ShapeDtypeStruct(prev.shape, prev.dtype),
        grid_spec=grid_spec,
        input_output_aliases={8: 0},
        compiler_params=_cp(("parallel", "parallel", "arbitrary"), 56),
        name="na_attn_lat",
    )(jnp.asarray(pat_of), jnp.asarray(kbase), z, z, z, cache_k, cache_v, bias, prev)


def _pool_kernel(*refs, t):
    u_ref, w_ref, sc_ref = refs[:3]
    o_ref, pad_ref = refs[-2:]
    halo = 8
    pos = lax.broadcasted_iota(jnp.int32, (t, POOL_GROUP), 0)
    zeros = jnp.zeros((halo, POOL_GROUP), F32)
    for gi, w in enumerate(POOL_WINDOWS):
        sl = slice(gi * POOL_GROUP, (gi + 1) * POOL_GROUP)
        u = u_ref[:, sl]
        pad_ref[0:halo, :] = zeros
        pad_ref[halo + t:2 * halo + t, :] = zeros
        pad_ref[halo:halo + t, :] = u
        tot = None
        for d in range(-(w // 2), w // 2):
            part = pad_ref[halo + d:halo + d + t, :]
            tot = part if tot is None else tot + part
        cnt = (jnp.minimum(pos + w // 2, t) - jnp.maximum(pos - w // 2, 0)).astype(F32)
        pooled = (tot / cnt - u).astype(BF16)
        mixed = jnp.dot(pooled, w_ref[gi], preferred_element_type=F32)
        o_ref[:, sl] = (mixed * sc_ref[:, sl]).astype(o_ref.dtype)


def _pool(z, pool_w, pool_scale, layer, out, prev, n_seq, seq, row0, name):
    width = len(POOL_WINDOWS) * POOL_GROUP
    aliased = prev is not None
    in_specs = [pl.BlockSpec((seq, width), lambda s: (row0 + s, C_POOL // width)),
                pl.BlockSpec((None, len(POOL_WINDOWS), POOL_GROUP, POOL_GROUP), lambda s: (layer, 0, 0, 0)),
                pl.BlockSpec((1, width), lambda s: (0, 0))]
    args = [z, pool_w, pool_scale]
    if aliased:
        in_specs.append(pl.BlockSpec(memory_space=pl.ANY))
        args.append(prev)
        out = jax.ShapeDtypeStruct(prev.shape, prev.dtype)
    return pl.pallas_call(
        functools.partial(_pool_kernel, t=seq),
        out_shape=out,
        grid=(n_seq,),
        in_specs=in_specs,
        out_specs=pl.BlockSpec((seq, width), lambda s: (row0 + s, 0)),
        scratch_shapes=[pltpu.VMEM((seq + 16, POOL_GROUP), F32)],
        input_output_aliases={3: 0} if aliased else {},
        compiler_params=_cp(("parallel",), 56),
        name=name,
    )(*args)


def _diff_prep_kernel(q_ref, k_ref, v_ref, cos_ref, sin_ref, qo_ref, ko_ref, vo_ref):
    cos = cos_ref[...]
    sin = sin_ref[...]
    lane = lax.broadcasted_iota(jnp.int32, (1, LANES), 1)
    first = (lane % DIFF_HD) < DIFF_HD // 2
    for h in range(DIFF_HEADS):
        sl = slice(h * LANES, (h + 1) * LANES)
        qo_ref[:, sl] = (_rope_block(q_ref[:, sl], cos, sin, first, DIFF_HD // 2) * HEAD64_QSCALE).astype(BF16)
        ko_ref[:, sl] = _rope_block(k_ref[:, sl], cos, sin, first, DIFF_HD // 2).astype(BF16)
    vo_ref[...] = v_ref[...].astype(BF16)


def _diff_prep(z, cos_d, sin_d, rope_row_of):
    t = z.shape[0]
    tm = PREP_TM
    w = DIFF_HEADS * LANES
    sds = jax.ShapeDtypeStruct((t, w), BF16)
    return pl.pallas_call(
        _diff_prep_kernel,
        out_shape=(sds, sds, sds),
        grid=(t // tm,),
        in_specs=[pl.BlockSpec((tm, w), lambda i: (i, C_DQ // w)),
                  pl.BlockSpec((tm, w), lambda i: (i, C_DK // w)),
                  pl.BlockSpec((tm, w), lambda i: (i, C_DV // w)),
                  pl.BlockSpec((tm, LANES), lambda i: (rope_row_of(i, tm), 0)),
                  pl.BlockSpec((tm, LANES), lambda i: (rope_row_of(i, tm), 0))],
        out_specs=(pl.BlockSpec((tm, w), lambda i: (i, 0)),) * 3,
        compiler_params=_cp(("parallel",), 40),
        name="diff_prep",
    )(z, z, z, cos_d, sin_d)


def _diff_lambda(linit_ref, lam_ref):
    lam_init = linit_ref[0]
    lp = lam_ref[...]
    lam = (jnp.exp(jnp.sum(lp[0:1] * lp[1:2], axis=-1, keepdims=True))
           - jnp.exp(jnp.sum(lp[2:3] * lp[3:4], axis=-1, keepdims=True)) + lam_init)
    return lam, lam_init


def _diff_ctx_kernel(linit_ref, lam_ref, g_ref, q_ref, k_ref, v_ref, o_ref, *, seq):
    lam, lam_init = _diff_lambda(linit_ref, lam_ref)
    lo = _lane_lo()
    rows = [slice(g * seq, (g + 1) * seq) for g in range(CTX_GROUP)]
    ss = []
    for r in rows:
        q = q_ref[r, :]
        zero = jnp.zeros_like(q)
        ss.append(_scores(jnp.where(lo, q, zero), k_ref[r, :]))
        ss.append(_scores(jnp.where(lo, zero, q), k_ref[r, :]))
    ms = [jnp.max(s, axis=-1, keepdims=True) for s in ss]
    es = [jnp.exp2(s - m) for s, m in zip(ss, ms)]
    ls = [jnp.sum(e, axis=-1, keepdims=True) for e in es]
    ps = [(es[2 * g] * (1.0 / ls[2 * g]) - es[2 * g + 1] * (lam / ls[2 * g + 1])).astype(BF16)
          for g in range(CTX_GROUP)]
    os = [jnp.dot(p, v_ref[r, :], preferred_element_type=F32) for p, r in zip(ps, rows)]
    for o, r in zip(os, rows):
        o_ref[r, :] = (_rms(o, g_ref[...]) * (1.0 - lam_init)).astype(o_ref.dtype)


def _diff_attn_ctx(linit, lam_p, g, q, k, v, out, layer, n_seq, seq):
    smem = pl.BlockSpec(memory_space=pltpu.SMEM)
    rows = CTX_GROUP * seq
    return pl.pallas_call(
        functools.partial(_diff_ctx_kernel, seq=seq),
        out_shape=out,
        grid=(n_seq // CTX_GROUP, DIFF_HEADS),
        in_specs=[smem,
                  pl.BlockSpec((None, 4, DIFF_HD), lambda s, h: (layer, 0, 0)),
                  pl.BlockSpec((1, LANES), lambda s, h: (0, 0)),
                  pl.BlockSpec((rows, LANES), lambda s, h: (s, h)),
                  pl.BlockSpec((rows, LANES), lambda s, h: (s, h)),
                  pl.BlockSpec((rows, LANES), lambda s, h: (s, h))],
        out_specs=pl.BlockSpec((rows, LANES), lambda s, h: (s, h)),
        compiler_params=_cp(("parallel", "parallel"), 40),
        name="diff_attn_ctx",
    )(linit, lam_p, g, q, k, v)


def _diff_lat_kernel(linit_ref, lam_ref, g_ref, q_ref, kl_ref, kc_ref, vl_ref, vc_ref, prev_ref, o_ref, s_ref,
                     *, n_sub):
    del prev_ref
    tq = DIFF_TQ
    lam, lam_init = _diff_lambda(linit_ref, lam_ref)
    lo = _lane_lo()
    k_refs = (kl_ref, kc_ref)
    v_refs = (vl_ref, vc_ref)
    chunks = _chunks((kl_ref.shape[0], kc_ref.shape[0]), ATT_CK)

    def q_maps(qs):
        q = q_ref[qs * tq:(qs + 1) * tq, :]
        zero = jnp.zeros_like(q)
        return jnp.where(lo, q, zero), jnp.where(lo, zero, q)

    def stage_a(qms, slot, chunk, mruns):
        si, st, n, off = chunk
        k = k_refs[si][st:st + n, :].astype(BF16)
        out = []
        for mi in range(2):
            s = _scores(qms[mi], k)
            s_ref[slot, mi, :, off:off + n] = s
            out.append(_tile_fold(jnp.maximum, mruns[mi], s))
        return out

    def stage_b1(slot, chunk, ms, lruns):
        _, _, n, off = chunk
        out = []
        for mi in range(2):
            e = jnp.exp2(s_ref[slot, mi, :, off:off + n] - ms[mi])
            s_ref[slot, mi, :, off:off + n] = e
            out.append(_tile_fold(jnp.add, lruns[mi], e))
        return out

    def stage_b2(slot, chunk, rho, acc):
        si, st, n, off = chunk
        p = s_ref[slot, 0, :, off:off + n] - s_ref[slot, 1, :, off:off + n] * rho
        return acc + jnp.dot(p.astype(BF16), v_refs[si][st:st + n, :].astype(BF16), preferred_element_type=F32)

    neg = jnp.full((tq, LANES), -jnp.inf, F32)
    zero = jnp.zeros((tq, LANES), F32)
    mruns = [neg, neg]
    qms = q_maps(0)
    for chunk in chunks:
        mruns = stage_a(qms, 0, chunk, mruns)
    for qs in range(n_sub):
        slot = qs % 2
        ms = [jnp.max(mr, axis=-1, keepdims=True) for mr in mruns]
        lruns, mruns = [zero, zero], [neg, neg]
        if qs + 1 < n_sub:
            qms = q_maps(qs + 1)
        for chunk in chunks:
            if qs + 1 < n_sub:
                mruns = stage_a(qms, 1 - slot, chunk, mruns)
            lruns = stage_b1(slot, chunk, ms, lruns)
        l1 = jnp.sum(lruns[0], axis=-1, keepdims=True)
        rho = lam * l1 / jnp.sum(lruns[1], axis=-1, keepdims=True)
        acc = zero
        for chunk in chunks:
            acc = stage_b2(slot, chunk, rho, acc)
        o_ref[qs * tq:(qs + 1) * tq, :] = (_rms(acc / l1, g_ref[...]) * (1.0 - lam_init)).astype(o_ref.dtype)


def _diff_attn_lat(linit, lam_p, g, q, k, v, cache_k, cache_v, prev, layer, t_ctx, db, ds, past):
    smem = pl.BlockSpec(memory_space=pltpu.SMEM)
    n_sub = DIFF_SUBTILES
    tqb = n_sub * DIFF_TQ
    nq = ds // tqb
    q0 = t_ctx // tqb
    s0 = t_ctx // ds
    return pl.pallas_call(
        functools.partial(_diff_lat_kernel, n_sub=n_sub),
        out_shape=jax.ShapeDtypeStruct(prev.shape, prev.dtype),
        grid=(db, DIFF_HEADS, nq),
        in_specs=[smem,
                  pl.BlockSpec((None, 4, DIFF_HD), lambda b, h, i: (layer, 0, 0)),
                  pl.BlockSpec((1, LANES), lambda b, h, i: (0, 0)),
                  pl.BlockSpec((tqb, LANES), lambda b, h, i: (q0 + b * nq + i, h)),
                  pl.BlockSpec((ds, LANES), lambda b, h, i: (s0 + b, h)),
                  pl.BlockSpec((None, None, past, LANES), lambda b, h, i: (b, layer, 0, h)),
                  pl.BlockSpec((ds, LANES), lambda b, h, i: (s0 + b, h)),
                  pl.BlockSpec((None, None, past, LANES), lambda b, h, i: (b, layer, 0, h)),
                  pl.BlockSpec(memory_space=pl.ANY)],
        out_specs=pl.BlockSpec((tqb, LANES), lambda b, h, i: (q0 + b * nq + i, h)),
        scratch_shapes=[pltpu.VMEM((2, 2, DIFF_TQ, ds + past), F32)],
        input_output_aliases={8: 0},
        compiler_params=_cp(("parallel", "parallel", "arbitrary"), 56),
        name="diff_attn_lat",
    )(linit, lam_p, g, q, k, cache_k, v, cache_v, prev)


def _merge_out_kernel(h_ref, wg0_ref, wg1_ref, wg2_ref, wg3_ref, oa_ref, ob_ref, oc_ref, od_ref, wbr_ref, wo_ref,
                      x_ref, mod_ref, gn_ref, x1_ref, h2_ref, *, d):
    n = pl.program_id(1)

    @pl.when(n == 0)
    def _():
        x1_ref[...] = jnp.zeros_like(x1_ref)

    h = h_ref[...]
    merged = None
    for bi, (wg_ref, o_ref) in enumerate(((wg0_ref, oa_ref), (wg1_ref, ob_ref), (wg2_ref, oc_ref), (wg3_ref, od_ref))):
        term = (jax.nn.sigmoid(jnp.dot(h, wg_ref[...], preferred_element_type=F32))
                * jnp.dot(o_ref[...], wbr_ref[bi], preferred_element_type=F32))
        merged = term if merged is None else merged + term
    x1_ref[...] += jnp.dot(merged.astype(BF16), wo_ref[...], preferred_element_type=F32)

    @pl.when(n == pl.num_programs(1) - 1)
    def _():
        mod = mod_ref[0]
        gain_mix = gn_ref[1:2, :] * mod[:, 2 * d:3 * d]
        gain_mlp = gn_ref[2:3, :] * (1.0 + mod[:, 4 * d:5 * d])
        sh2 = mod[:, 3 * d:4 * d]
        for r0 in range(0, x_ref.shape[0], EPI_ROWS):
            rows = slice(r0, r0 + EPI_ROWS)
            x1 = x_ref[rows, :] + _rms(x1_ref[rows, :], gain_mix)
            x1_ref[rows, :] = x1
            h2_ref[rows, :] = (_rms(x1, gain_mlp) + sh2).astype(BF16)


def _merge_out(h, w_gates, outs, w_br, w_o, layer, x, mod_l, gn, row_of):
    t, d = x.shape
    bw = outs[0].shape[1]
    tm, tn = 512, MERGE_TN
    nt = d // tn
    row_spec = pl.BlockSpec((tm, d), lambda i, n: (i, 0))
    o_spec = pl.BlockSpec((tm, bw), lambda i, n: (i, 0))
    g0 = GATE_COL0 // tn
    g_specs = [pl.BlockSpec((None, d, tn), functools.partial(lambda i, n, bi: (layer, 0, g0 + bi * nt + n), bi=bi))
               for bi in range(4)]
    return pl.pallas_call(
        functools.partial(_merge_out_kernel, d=d),
        out_shape=(jax.ShapeDtypeStruct((t, d), F32), jax.ShapeDtypeStruct((t, d), BF16)),
        grid=(t // tm, nt),
        in_specs=[row_spec, *g_specs,
                  o_spec, o_spec, o_spec, o_spec,
                  pl.BlockSpec((None, 4, bw, tn), lambda i, n: (layer, 0, 0, n)),
                  pl.BlockSpec((None, tn, d), lambda i, n: (layer, n, 0)),
                  row_spec,
                  pl.BlockSpec((1, 1, 6 * d), lambda i, n: (row_of(i, tm), 0, 0)),
                  pl.BlockSpec((4, d), lambda i, n: (0, 0))],
        out_specs=(row_spec, row_spec),
        compiler_params=_cp(("parallel", "arbitrary"), 58),
        name="merge_out",
    )(h, w_gates, w_gates, w_gates, w_gates, *outs, w_br, w_o, x, mod_l, gn)


def _ffn_kernel(*refs, d, emit_next):
    h_ref, wu_ref, wd_ref, x_ref, mod_ref, gn_ref = refs[:6]
    o_ref = refs[8] if emit_next else refs[6]
    f = pl.program_id(1)

    @pl.when(f == 0)
    def _():
        o_ref[...] = jnp.zeros_like(o_ref)

    u = jnp.dot(h_ref[...], wu_ref[...], preferred_element_type=F32)
    u = jnp.square(jnp.maximum(u, 0.0)).astype(BF16)
    o_ref[...] += jnp.dot(u, wd_ref[...], preferred_element_type=F32)

    @pl.when(f == pl.num_programs(1) - 1)
    def _():
        gain_out = gn_ref[3:4, :] * mod_ref[0][:, 5 * d:6 * d]
        if emit_next:
            modn_ref, gnn_ref, hn_ref = refs[6], refs[7], refs[9]
            gain_next = gnn_ref[0:1, :] * (1.0 + modn_ref[0][:, d:2 * d])
            shn = modn_ref[0][:, 0:d]
        for r0 in range(0, x_ref.shape[0], EPI_ROWS):
            rows = slice(r0, r0 + EPI_ROWS)
            x_new = x_ref[rows, :] + _rms(o_ref[rows, :], gain_out)
            o_ref[rows, :] = x_new
            if emit_next:
                hn_ref[rows, :] = (_rms(x_new, gain_next) + shn).astype(BF16)


def _ffn(h2, w_up, w_down, layer, x1, mod_l, gn, row_of, mod_next=None, gn_next=None, row0=0, n_rows=None):
    t, d = x1.shape
    n_rows = t if n_rows is None else n_rows
    dff = w_up.shape[2]
    tm, tf = 512, 1024
    i0 = row0 // tm
    emit_next = mod_next is not None
    mod_spec = pl.BlockSpec((1, 1, 6 * d), lambda i, f: (row_of(i0 + i, tm), 0, 0))
    gn_spec = pl.BlockSpec((4, d), lambda i, f: (0, 0))
    in_row_spec = pl.BlockSpec((tm, d), lambda i, f: (i0 + i, 0))
    out_row_spec = pl.BlockSpec((tm, d), lambda i, f: (i, 0))
    in_specs = [in_row_spec,
                pl.BlockSpec((None, d, tf), lambda i, f: (layer, 0, f)),
                pl.BlockSpec((None, tf, d), lambda i, f: (layer, f, 0)),
                in_row_spec, mod_spec, gn_spec]
    args = [h2, w_up, w_down, x1, mod_l, gn]
    out_shape = jax.ShapeDtypeStruct((n_rows, d), F32)
    out_specs = out_row_spec
    if emit_next:
        in_specs += [mod_spec, gn_spec]
        args += [mod_next, gn_next]
        out_shape = (out_shape, jax.ShapeDtypeStruct((n_rows, d), BF16))
        out_specs = (out_row_spec, out_row_spec)
    return pl.pallas_call(
        functools.partial(_ffn_kernel, d=d, emit_next=emit_next),
        out_shape=out_shape,
        grid=(n_rows // tm, dff // tf),
        in_specs=in_specs,
        out_specs=out_specs,
        compiler_params=_cp(("parallel", "arbitrary"), 58),
        name="ffn",
    )(*args)


def _prep_weights(w_in, w_uq, w_ukv):
    depth, d, _ = w_in.shape
    sizes = (MLA_Q_LORA, MLA_KV_LORA, MLA_ROPE, 512, 512, 512, 512, 512, 512, 512, 4 * d)
    offs = np.cumsum((0,) + sizes)
    part = lambda i: w_in[:, :, offs[i]:offs[i + 1]]
    q_c, kv_c, k_r, na_q, na_k, na_v, pool, dq, dk, dv, gates = (part(i) for i in range(11))
    kr_blk = jnp.pad(k_r, ((0, 0), (0, 0), (MLA_NOPE, LANES - MLA_NOPE - MLA_ROPE)))
    gap = jnp.zeros((depth, d, GATE_COL0 - Z_COLS), w_in.dtype)
    w_cat = jnp.concatenate([q_c, dq, dk, dv, na_q, na_k, na_v, pool, kv_c, kr_blk, gap, gates], axis=-1).astype(BF16)
    hd = MLA_NOPE + MLA_ROPE
    wuq = jnp.pad(w_uq.reshape(depth, MLA_Q_LORA, MLA_HEADS, hd), ((0, 0), (0, 0), (0, 0), (0, LANES - hd)))
    wuq = wuq.reshape(depth, MLA_Q_LORA, MLA_HEADS * LANES).astype(BF16)
    wkv4 = w_ukv.reshape(depth, MLA_KV_LORA, MLA_HEADS, MLA_NOPE + MLA_V)
    wk = jnp.pad(wkv4[..., :MLA_NOPE], ((0, 0), (0, 0), (0, 0), (0, LANES - MLA_NOPE)))
    wk = wk.reshape(depth, MLA_KV_LORA, MLA_HEADS * LANES)
    wv = wkv4[..., MLA_NOPE:].reshape(depth, MLA_KV_LORA, MLA_HEADS * MLA_V)
    wkv = jnp.concatenate([wk, wv], axis=-1).astype(BF16)
    return w_cat, wuq, wkv


def kernel(x_prompt, x_sample, cache_mla_ckv, cache_mla_krope, cache_na_k, cache_na_v, cache_diff_k, cache_diff_v, c, c_ctx, w_mod, b_mod, g_norm, w_in, g_q_lora, g_kv_lora, w_uq, w_ukv, na_rpb, pool_w, pool_scale, diff_lambda, diff_norm_g, w_br, w_o, w_up, w_down):
    nb, seq, d = x_prompt.shape
    db, ds, _ = x_sample.shape
    depth = w_in.shape[0]
    past = cache_mla_ckv.shape[2]
    t_ctx = nb * seq
    t_lat = db * ds
    t = t_ctx + t_lat
    assert t_ctx % ds == 0 and ds % 1024 == 0 and t_ctx % 1024 == 0 and seq % 8 == 0 and nb % CTX_GROUP == 0

    def row_of(i, tm):
        n_ctx = t_ctx // tm
        return jnp.where(i < n_ctx, 0, 1 + (i - n_ctx) // (ds // tm))

    def rope_row_of(i, tm):
        n_ctx = t_ctx // tm
        return jnp.where(i < n_ctx, 0, 1 + (i - n_ctx) % (ds // tm))

    w_cat, wuq, wkv = _prep_weights(w_in, w_uq, w_ukv)
    w_br_b = w_br.astype(BF16)
    w_o_b = w_o.astype(BF16)
    w_up_b = w_up.astype(BF16)
    w_down_b = w_down.astype(BF16)
    pool_w_b = pool_w.astype(BF16)
    cos_m, sin_m, cos_d, sin_d = _rope_tables(ds, PREP_TM)
    rows = ds // GRID_W
    na_plan = _na_plan(rows)
    bias = _na_bias_table(na_rpb, na_plan[1])
    cache_kr_pad = jnp.pad(cache_mla_krope, ((0, 0), (0, 0), (0, 0), (MLA_NOPE, LANES - MLA_NOPE - MLA_ROPE)))
    cache_nk = cache_na_k.reshape(db, depth, past, NA_HEADS * NA_HD)
    cache_nv = cache_na_v.reshape(db, depth, past, NA_HEADS * NA_HD)
    cache_dk = cache_diff_k.reshape(db, depth, past, DIFF_HEADS * 2 * DIFF_HD)
    cache_dv = cache_diff_v.reshape(db, depth, past, DIFF_HEADS * 2 * DIFF_HD)

    n_rows = 1 + db
    r_pad = -(-n_rows // 8) * 8
    cond = jnp.concatenate([c_ctx[None, :], c, jnp.zeros((r_pad - n_rows, d), F32)], axis=0)
    mod = _modulation(cond, w_mod, b_mod)
    mods = [mod[li].reshape(r_pad, 1, 6 * d) for li in range(depth)]

    branch_sds = jax.ShapeDtypeStruct((t, 512), BF16)
    x, h = _norm_mod(x_prompt.reshape(t_ctx, d), x_sample.reshape(t_lat, d), mods[0], g_norm[0][0:1], row_of)
    states = []
    for li in range(depth):
        mod_l = mods[li]
        gn = g_norm[li]
        z = _matmul(h, w_cat, li, Z_COLS, 1024, 896, F32, "in_proj")

        q_a, ckv, k_a, v_a = _mla_prep(z, cos_m, sin_m, g_q_lora[li][None, :], g_kv_lora[li][None, :],
                                       wuq, wkv, li, rope_row_of)
        kc_a, vc_a = _mla_cache(cache_mla_ckv, cache_kr_pad, wkv, li)
        o_a = _mla_attn_ctx(q_a, k_a, v_a, branch_sds, nb, seq)
        o_a = _mla_attn_lat(q_a, k_a, v_a, kc_a, vc_a, o_a, t_ctx, db, ds, past)

        o_b = _na_attn_ctx(z, branch_sds, nb, seq)
        o_b = _na_attn_lat(z, cache_nk, cache_nv, bias, na_plan, o_b, li, t_ctx, db, ds, past)

        o_c = _pool(z, pool_w_b, pool_scale[li][None, :], li, branch_sds, None, nb, seq, 0, "pool_ctx")
        o_c = _pool(z, pool_w_b, pool_scale[li][None, :], li, None, o_c, db, ds, t_ctx // ds, "pool_lat")

        q_d, k_d, v_d = _diff_prep(z, cos_d, sin_d, rope_row_of)
        linit = jnp.full((1,), 0.8 - 0.6 * math.exp(-0.3 * li), F32)
        g_d = diff_norm_g[li][None, :]
        o_d = _diff_attn_ctx(linit, diff_lambda, g_d, q_d, k_d, v_d, branch_sds, li, nb, seq)
        o_d = _diff_attn_lat(linit, diff_lambda, g_d, q_d, k_d, v_d, cache_dk, cache_dv, o_d,
                             li, t_ctx, db, ds, past)

        x1, h2 = _merge_out(h, w_cat, (o_a, o_b, o_c, o_d), w_br_b, w_o_b, li, x, mod_l, gn, row_of)
        if li + 1 < depth:
            x, h = _ffn(h2, w_up_b, w_down_b, li, x1, mod_l, gn, row_of, mods[li + 1], g_norm[li + 1])
        else:
            y_prompt = _ffn(h2, w_up_b, w_down_b, li, x1, mod_l, gn, row_of, row0=0, n_rows=t_ctx)
            y_sample = _ffn(h2, w_up_b, w_down_b, li, x1, mod_l, gn, row_of, row0=t_ctx, n_rows=t_lat)

        zc = z[:t_ctx]
        states.append((ckv[:t_ctx], zc[:, C_KR + MLA_NOPE:C_KR + MLA_NOPE + MLA_ROPE], zc[:, C_NK:C_NK + 512],
                       zc[:, C_NV:C_NV + 512], zc[:, C_DK:C_DK + 512], zc[:, C_DV:C_DV + 512]))

    def stacked(k, tail):
        per_layer = [s[k].reshape(nb, seq, -1) for s in states]
        return jnp.stack(per_layer, axis=1).reshape(nb, depth, seq, *tail)

    return (y_prompt.reshape(nb, seq, d), y_sample.reshape(db, ds, d),
            stacked(0, (MLA_KV_LORA,)), stacked(1, (MLA_ROPE,)),
            stacked(2, (NA_HEADS, NA_HD)), stacked(3, (NA_HEADS, NA_HD)),
            stacked(4, (DIFF_HEADS, 2 * DIFF_HD)), stacked(5, (DIFF_HEADS, 2 * DIFF_HD)))
```

```python
import functools
import math

import numpy as np
import jax
import jax.numpy as jnp
from jax import lax
from jax.experimental import pallas as pl
from jax.experimental.pallas import tpu as pltpu

F32 = jnp.float32
BF16 = jnp.bfloat16

GRID_W = 64
MLA_HEADS = 8
MLA_NOPE = 64
MLA_ROPE = 32
MLA_V = 64
MLA_Q_LORA = 512
MLA_KV_LORA = 256
NA_HEADS = 8
NA_HD = 64
NA_WIN_R = 8
NA_WIN_C = 16
POOL_WINDOWS = (2, 4, 8, 16)
POOL_GROUP = 128
DIFF_HEADS = 4
DIFF_HD = 64
ROPE_THETA = 10000.0
RMS_EPS = 1e-6
NEG_INF = -1e30
LOG2E = math.log2(math.e)
MLA_QSCALE = (MLA_NOPE + MLA_ROPE) ** -0.5 * LOG2E
HEAD64_QSCALE = 64 ** -0.5 * LOG2E
LANES = 128
MIB = 1024 * 1024

C_QC = 0
C_DQ = 512
C_DK = 1024
C_DV = 1536
C_NQ = 2048
C_NK = 2560
C_NV = 3072
C_POOL = 3584
C_KVC = 4096
C_KR = 4352
Z_COLS = 4480
GATE_COL0 = 4608

MERGE_TN = 512
NA_STEP_HEADS = 4
NA_QROWS = 8
NA_KROWS = 16
ATT_TQ = 256
DIFF_TQ = 256
DIFF_SUBTILES = 4
ATT_CK = 512
CTX_GROUP = 4
EPI_ROWS = 16
PREP_TM = 1024


def _cp(sem, vmem_mib):
    return pltpu.CompilerParams(dimension_semantics=sem, vmem_limit_bytes=vmem_mib * MIB)


def _rms(x, g):
    return x * lax.rsqrt(jnp.mean(x * x, axis=-1, keepdims=True) + RMS_EPS) * g


def _scores(q, k):
    return lax.dot_general(q, k, (((1,), (1,)), ((), ())), preferred_element_type=F32)


def _lane_lo():
    return lax.broadcasted_iota(jnp.int32, (1, LANES), 1) < 64


def _tile_fold(op, run, x):
    for c in range(x.shape[1] // LANES):
        run = op(run, x[:, c * LANES:(c + 1) * LANES])
    return run


def _chunks(sizes, ck):
    out, off = [], 0
    for si, n in enumerate(sizes):
        step = min(ck, n)
        for st in range(0, n, step):
            out.append((si, st, step, off))
            off += step
    return out


def _mod_kernel(c_ref, w_ref, b_ref, o_ref):
    c = c_ref[...]
    s = c * jax.nn.sigmoid(c)
    o_ref[0] = jnp.dot(s.astype(BF16), w_ref[0].astype(BF16), preferred_element_type=F32) + b_ref[0]


def _modulation(cond, w_mod, b_mod):
    depth, d, n = w_mod.shape
    r = cond.shape[0]
    tn = 1024
    return pl.pallas_call(
        _mod_kernel,
        out_shape=jax.ShapeDtypeStruct((depth, r, n), F32),
        grid=(depth, n // tn),
        in_specs=[pl.BlockSpec((r, d), lambda l, j: (0, 0)),
                  pl.BlockSpec((1, d, tn), lambda l, j: (l, 0, j)),
                  pl.BlockSpec((1, 1, tn), lambda l, j: (l, 0, j))],
        out_specs=pl.BlockSpec((1, r, tn), lambda l, j: (l, 0, j)),
        compiler_params=_cp(("parallel", "parallel"), 40),
        name="modulation",
    )(cond, w_mod, b_mod.reshape(depth, 1, n))


def _norm_mod_kernel(xp_ref, xs_ref, mod_ref, g_ref, x_ref, h_ref, *, d, n_ctx):
    mod = mod_ref[0]
    sh = mod[:, 0:d]
    sc = mod[:, d:2 * d]

    def emit(src_ref):
        x = src_ref[...]
        x_ref[...] = x
        h_ref[...] = (_rms(x, g_ref[...]) * (1.0 + sc) + sh).astype(BF16)

    @pl.when(pl.program_id(0) < n_ctx)
    def _():
        emit(xp_ref)

    @pl.when(pl.program_id(0) >= n_ctx)
    def _():
        emit(xs_ref)


def _norm_mod(xp, xs, mod_l, g, row_of):
    (t_ctx, d), t_lat = xp.shape, xs.shape[0]
    tm = 512
    n_ctx = t_ctx // tm
    t = t_ctx + t_lat
    return pl.pallas_call(
        functools.partial(_norm_mod_kernel, d=d, n_ctx=n_ctx),
        out_shape=(jax.ShapeDtypeStruct((t, d), F32), jax.ShapeDtypeStruct((t, d), BF16)),
        grid=(t // tm,),
        in_specs=[pl.BlockSpec((tm, d), lambda i: (jnp.minimum(i, n_ctx - 1), 0)),
                  pl.BlockSpec((tm, d), lambda i: (jnp.maximum(i - n_ctx, 0), 0)),
                  pl.BlockSpec((1, 1, 6 * d), lambda i: (row_of(i, tm), 0, 0)),
                  pl.BlockSpec((1, d), lambda i: (0, 0))],
        out_specs=(pl.BlockSpec((tm, d), lambda i: (i, 0)), pl.BlockSpec((tm, d), lambda i: (i, 0))),
        compiler_params=_cp(("parallel",), 48),
        name="norm_mod",
    )(xp, xs, mod_l, g)


def _mm_kernel(a_ref, b_ref, o_ref):
    o_ref[...] = jnp.dot(a_ref[...], b_ref[...], preferred_element_type=F32).astype(o_ref.dtype)


def _matmul(a, b, layer, n, tm, tn, out_dtype, name):
    m, k = a.shape
    return pl.pallas_call(
        _mm_kernel,
        out_shape=jax.ShapeDtypeStruct((m, n), out_dtype),
        grid=(m // tm, n // tn),
        in_specs=[pl.BlockSpec((tm, k), lambda i, j: (i, 0)),
                  pl.BlockSpec((None, k, tn), lambda i, j: (layer, 0, j))],
        out_specs=pl.BlockSpec((tm, tn), lambda i, j: (i, j)),
        compiler_params=_cp(("parallel", "arbitrary"), 48),
        name=name,
    )(a, b)


def _rope_block(x, cos, sin, first_half, half):
    partner = jnp.where(first_half, -pltpu.roll(x, LANES - half, 1), pltpu.roll(x, half, 1))
    return x * cos + partner * sin


def _rope_tables(n_lat, ident_rows):
    t = jnp.arange(n_lat)
    row = (t // GRID_W).astype(F32)
    col = (t % GRID_W).astype(F32)

    def angles(rot_dim):
        n_freq = rot_dim // 4
        inv = ROPE_THETA ** (-jnp.arange(n_freq, dtype=F32) / n_freq)
        return jnp.concatenate([row[:, None] * inv, col[:, None] * inv], axis=-1)

    a_m = angles(MLA_ROPE)
    zeros64 = jnp.zeros((n_lat, 64), F32)
    zeros32 = jnp.zeros((n_lat, 32), F32)
    cos_m = jnp.concatenate([zeros64 + 1.0, jnp.cos(a_m), jnp.cos(a_m), zeros32 + 1.0], axis=-1)
    sin_m = jnp.concatenate([zeros64, jnp.sin(a_m), jnp.sin(a_m), zeros32], axis=-1)
    a_d = angles(DIFF_HD)
    cos_d = jnp.tile(jnp.cos(a_d), (1, 4))
    sin_d = jnp.tile(jnp.sin(a_d), (1, 4))
    one = jnp.ones((ident_rows, LANES), F32)
    zero = jnp.zeros((ident_rows, LANES), F32)
    cat = lambda a, b: jnp.concatenate([a, b], axis=0)
    return cat(one, cos_m), cat(zero, sin_m), cat(one, cos_d), cat(zero, sin_d)


def _mla_prep_kernel(qc_ref, kvc_ref, kr_ref, cos_ref, sin_ref, gq_ref, gkv_ref, wuq_ref, wkv_ref,
                     q_ref, ckv_ref, k_ref, v_ref):
    cos = cos_ref[...]
    sin = sin_ref[...]
    lane = lax.broadcasted_iota(jnp.int32, (1, LANES), 1)
    first = lane < MLA_NOPE + MLA_ROPE // 2
    rope = lambda x: _rope_block(x, cos, sin, first, MLA_ROPE // 2)
    qn = _rms(qc_ref[...], gq_ref[...]).astype(BF16)
    q = jnp.dot(qn, wuq_ref[...], preferred_element_type=F32)
    for h in range(MLA_HEADS):
        sl = slice(h * LANES, (h + 1) * LANES)
        q_ref[:, sl] = (rope(q[:, sl]) * MLA_QSCALE).astype(BF16)
    ckv = _rms(kvc_ref[...], gkv_ref[...])
    ckv_ref[...] = ckv
    kv = jnp.dot(ckv.astype(BF16), wkv_ref[...], preferred_element_type=F32)
    krr = rope(kr_ref[...])
    for h in range(MLA_HEADS):
        sl = slice(h * LANES, (h + 1) * LANES)
        k_ref[:, sl] = (kv[:, sl] + krr).astype(BF16)
    v_ref[...] = kv[:, MLA_HEADS * LANES:].astype(BF16)


def _mla_prep(z, cos_m, sin_m, gq, gkv, wuq, wkv, layer, rope_row_of):
    t = z.shape[0]
    tm = PREP_TM
    kw = MLA_HEADS * LANES
    vw = MLA_HEADS * MLA_V
    return pl.pallas_call(
        _mla_prep_kernel,
        out_shape=(jax.ShapeDtypeStruct((t, kw), BF16), jax.ShapeDtypeStruct((t, MLA_KV_LORA), F32),
                   jax.ShapeDtypeStruct((t, kw), BF16), jax.ShapeDtypeStruct((t, vw), BF16)),
        grid=(t // tm,),
        in_specs=[pl.BlockSpec((tm, MLA_Q_LORA), lambda i: (i, C_QC // MLA_Q_LORA)),
                  pl.BlockSpec((tm, MLA_KV_LORA), lambda i: (i, C_KVC // MLA_KV_LORA)),
                  pl.BlockSpec((tm, LANES), lambda i: (i, C_KR // LANES)),
                  pl.BlockSpec((tm, LANES), lambda i: (rope_row_of(i, tm), 0)),
                  pl.BlockSpec((tm, LANES), lambda i: (rope_row_of(i, tm), 0)),
                  pl.BlockSpec((1, MLA_Q_LORA), lambda i: (0, 0)),
                  pl.BlockSpec((1, MLA_KV_LORA), lambda i: (0, 0)),
                  pl.BlockSpec((None, MLA_Q_LORA, kw), lambda i: (layer, 0, 0)),
                  pl.BlockSpec((None, MLA_KV_LORA, kw + vw), lambda i: (layer, 0, 0))],
        out_specs=(pl.BlockSpec((tm, kw), lambda i: (i, 0)),
                   pl.BlockSpec((tm, MLA_KV_LORA), lambda i: (i, 0)),
                   pl.BlockSpec((tm, kw), lambda i: (i, 0)),
                   pl.BlockSpec((tm, vw), lambda i: (i, 0))),
        compiler_params=_cp(("parallel",), 56),
        name="mla_prep",
    )(z, z, z, cos_m, sin_m, gq, gkv, wuq, wkv)


def _mla_cache_kernel(ckv_ref, kr_ref, wkv_ref, k_ref, v_ref):
    kv = jnp.dot(ckv_ref[...].astype(BF16), wkv_ref[...], preferred_element_type=F32)
    kr = kr_ref[...]
    for h in range(MLA_HEADS):
        sl = slice(h * LANES, (h + 1) * LANES)
        k_ref[:, sl] = (kv[:, sl] + kr).astype(BF16)
    v_ref[...] = kv[:, MLA_HEADS * LANES:].astype(BF16)


def _mla_cache(cache_ckv, cache_kr_pad, wkv, layer):
    db, _, past, _ = cache_ckv.shape
    kw = MLA_HEADS * LANES
    vw = MLA_HEADS * MLA_V
    return pl.pallas_call(
        _mla_cache_kernel,
        out_shape=(jax.ShapeDtypeStruct((db * past, kw), BF16), jax.ShapeDtypeStruct((db * past, vw), BF16)),
        grid=(db,),
        in_specs=[pl.BlockSpec((None, None, past, MLA_KV_LORA), lambda b: (b, layer, 0, 0)),
                  pl.BlockSpec((None, None, past, LANES), lambda b: (b, layer, 0, 0)),
                  pl.BlockSpec((None, MLA_KV_LORA, kw + vw), lambda b: (layer, 0, 0))],
        out_specs=(pl.BlockSpec((past, kw), lambda b: (b, 0)), pl.BlockSpec((past, vw), lambda b: (b, 0))),
        compiler_params=_cp(("parallel",), 40),
        name="mla_cache",
    )(cache_ckv, cache_kr_pad, wkv)


def _phased_softmax_pv(ss, vs):
    ms = [jnp.max(s, axis=-1, keepdims=True) for s in ss]
    ps = [jnp.exp2(s - m) for s, m in zip(ss, ms)]
    ls = [jnp.sum(p, axis=-1, keepdims=True) for p in ps]
    return [jnp.dot(p.astype(BF16), v, preferred_element_type=F32) / l for p, v, l in zip(ps, vs, ls)]


def _mla_ctx_kernel(q_ref, k_ref, v_ref, o_ref, *, seq):
    lo = _lane_lo()
    rows = [slice(g * seq, (g + 1) * seq) for g in range(CTX_GROUP)]
    lanes = [slice(hh * LANES, (hh + 1) * LANES) for hh in range(2)]
    units = [(r, sl) for r in rows for sl in lanes]
    outs = _phased_softmax_pv([_scores(q_ref[r, sl], k_ref[r, sl]) for r, sl in units],
                              [v_ref[r, :] for r, _ in units])
    for g, r in enumerate(rows):
        o_ref[r, :] = jnp.where(lo, outs[2 * g], outs[2 * g + 1]).astype(o_ref.dtype)


def _mla_attn_ctx(q, k, v, out, n_seq, seq):
    pairs = MLA_HEADS // 2
    rows = CTX_GROUP * seq
    return pl.pallas_call(
        functools.partial(_mla_ctx_kernel, seq=seq),
        out_shape=out,
        grid=(n_seq // CTX_GROUP, pairs),
        in_specs=[pl.BlockSpec((rows, 2 * LANES), lambda s, p: (s, p)),
                  pl.BlockSpec((rows, 2 * LANES), lambda s, p: (s, p)),
                  pl.BlockSpec((rows, LANES), lambda s, p: (s, p))],
        out_specs=pl.BlockSpec((rows, LANES), lambda s, p: (s, p)),
        compiler_params=_cp(("parallel", "parallel"), 40),
        name="mla_attn_ctx",
    )(q, k, v)


def _mla_lat_kernel(q_ref, kl_ref, kc_ref, vl_ref, vc_ref, prev_ref, o_ref, s_ref, *, n_sub):
    del prev_ref
    tq = ATT_TQ
    k_refs = (kl_ref, kc_ref)
    v_refs = (vl_ref, vc_ref)
    chunks = _chunks((kl_ref.shape[0], kc_ref.shape[0]), ATT_CK)
    units = [(qs, hh) for qs in range(n_sub) for hh in range(2)]

    def stage_a(unit, slot, chunk, mrun):
        qs, hh = unit
        si, st, n, off = chunk
        sl = slice(hh * LANES, (hh + 1) * LANES)
        s = _scores(q_ref[qs * tq:(qs + 1) * tq, sl], k_refs[si][st:st + n, sl])
        s_ref[slot, :, off:off + n] = s
        return _tile_fold(jnp.maximum, mrun, s)

    def stage_b(slot, chunk, m, lrun, acc):
        si, st, n, off = chunk
        p = jnp.exp2(s_ref[slot, :, off:off + n] - m)
        lrun = _tile_fold(jnp.add, lrun, p)
        acc = acc + jnp.dot(p.astype(BF16), v_refs[si][st:st + n, :], preferred_element_type=F32)
        return lrun, acc

    neg = jnp.full((tq, LANES), -jnp.inf, F32)
    zero = jnp.zeros((tq, LANES), F32)
    mrun = neg
    for chunk in chunks:
        mrun = stage_a(units[0], 0, chunk, mrun)
    outs = {}
    for ui, unit in enumerate(units):
        slot = ui % 2
        m = jnp.max(mrun, axis=-1, keepdims=True)
        lrun, acc, mrun = zero, zero, neg
        for chunk in chunks:
            if ui + 1 < len(units):
                mrun = stage_a(units[ui + 1], 1 - slot, chunk, mrun)
            lrun, acc = stage_b(slot, chunk, m, lrun, acc)
        outs[unit] = acc / jnp.sum(lrun, axis=-1, keepdims=True)
    lo = _lane_lo()
    for qs in range(n_sub):
        o_ref[qs * tq:(qs + 1) * tq, :] = jnp.where(lo, outs[(qs, 0)], outs[(qs, 1)]).astype(o_ref.dtype)


def _mla_attn_lat(q, k, v, kc, vc, prev, t_ctx, db, ds, past):
    pairs = MLA_HEADS // 2
    n_sub = 2
    tqb = n_sub * ATT_TQ
    nq = ds // tqb
    q0 = t_ctx // tqb
    s0 = t_ctx // ds
    return pl.pallas_call(
        functools.partial(_mla_lat_kernel, n_sub=n_sub),
        out_shape=jax.ShapeDtypeStruct(prev.shape, prev.dtype),
        grid=(db, pairs, nq),
        in_specs=[pl.BlockSpec((tqb, 2 * LANES), lambda b, p, i: (q0 + b * nq + i, p)),
                  pl.BlockSpec((ds, 2 * LANES), lambda b, p, i: (s0 + b, p)),
                  pl.BlockSpec((past, 2 * LANES), lambda b, p, i: (b, p)),
                  pl.BlockSpec((ds, LANES), lambda b, p, i: (s0 + b, p)),
                  pl.BlockSpec((past, LANES), lambda b, p, i: (b, p)),
                  pl.BlockSpec(memory_space=pl.ANY)],
        out_specs=pl.BlockSpec((tqb, LANES), lambda b, p, i: (q0 + b * nq + i, p)),
        scratch_shapes=[pltpu.VMEM((2, ATT_TQ, ds + past), F32)],
        input_output_aliases={5: 0},
        compiler_params=_cp(("parallel", "parallel", "arbitrary"), 56),
        name="mla_attn_lat",
    )(q, k, kc, v, vc, prev)


def _attn64_ctx_kernel(q_ref, k_ref, v_ref, o_ref, *, seq):
    lo = _lane_lo()
    rows = [slice(g * seq, (g + 1) * seq) for g in range(CTX_GROUP)]
    qs = [q_ref[r, :] * HEAD64_QSCALE for r in rows]
    ks = [k_ref[r, :].astype(BF16) for r in rows]
    vs = [v_ref[r, :].astype(BF16) for r in rows]
    ss, vv = [], []
    for q, k, v in zip(qs, ks, vs):
        ss.append(_scores(jnp.where(lo, q, 0.0).astype(BF16), k))
        ss.append(_scores(jnp.where(lo, 0.0, q).astype(BF16), k))
        vv += [v, v]
    outs = _phased_softmax_pv(ss, vv)
    for g, r in enumerate(rows):
        o_ref[r, :] = jnp.where(lo, outs[2 * g], outs[2 * g + 1]).astype(o_ref.dtype)


def _na_attn_ctx(z, out, n_seq, seq):
    pairs = NA_HEADS // 2
    rows = CTX_GROUP * seq
    return pl.pallas_call(
        functools.partial(_attn64_ctx_kernel, seq=seq),
        out_shape=out,
        grid=(n_seq // CTX_GROUP, pairs),
        in_specs=[pl.BlockSpec((rows, LANES), lambda s, p: (s, C_NQ // LANES + p)),
                  pl.BlockSpec((rows, LANES), lambda s, p: (s, C_NK // LANES + p)),
                  pl.BlockSpec((rows, LANES), lambda s, p: (s, C_NV // LANES + p))],
        out_specs=pl.BlockSpec((rows, LANES), lambda s, p: (s, p)),
        compiler_params=_cp(("parallel", "parallel"), 40),
        name="na_attn_ctx",
    )(z, z, z)


def _na_plan(rows):
    krows = min(NA_KROWS, rows)
    wr = min(NA_WIN_R, rows)
    nblk = rows // NA_QROWS
    kbase = np.zeros((nblk,), np.int32)
    drmaps = np.zeros((nblk, NA_QROWS, krows), np.int32)
    invalid = 2 * NA_WIN_R - 1
    for blk in range(nblk):
        r0 = blk * NA_QROWS
        kb = int(np.clip(r0 - wr // 2, 0, rows - krows))
        kbase[blk] = kb
        for rr in range(NA_QROWS):
            r = r0 + rr
            w0 = int(np.clip(r - wr // 2, 0, rows - wr))
            assert kb <= w0 and w0 + wr <= kb + krows
            for kk in range(krows):
                krow = kb + kk
                drmaps[blk, rr, kk] = (krow - r + NA_WIN_R - 1) if (w0 <= krow < w0 + wr) else invalid
    pats, pat_of = np.unique(drmaps, axis=0, return_inverse=True)
    return kbase, pats, np.asarray(pat_of, np.int32).reshape(nblk), krows


def _na_bias_kernel(e_ref, o_ref, *, pats):
    lo = _lane_lo()
    n_pat, qr, kr = pats.shape
    for pi in range(n_pat):
        for rr in range(qr):
            for k2 in range(kr // 2):
                a, b = int(pats[pi, rr, 2 * k2]), int(pats[pi, rr, 2 * k2 + 1])
                blk = e_ref[a] if a == b else jnp.where(lo, e_ref[a], e_ref[b])
                o_ref[pi, rr * GRID_W:(rr + 1) * GRID_W, k2 * LANES:(k2 + 1) * LANES] = blk


def _na_bias_table(rpb, pats):
    depth, h = rpb.shape[:2]
    n_dr = 2 * NA_WIN_R
    c = np.arange(GRID_W)[:, None]
    kc = np.arange(GRID_W)[None, :]
    cs = np.clip(c - NA_WIN_C // 2, 0, GRID_W - NA_WIN_C)
    valid = (kc >= cs) & (kc < cs + NA_WIN_C)
    idx = np.clip(kc - c + NA_WIN_C - 1, 0, 2 * NA_WIN_C - 2)
    e = jnp.where(valid[None, None, None], rpb[:, :, :, idx].astype(F32) * LOG2E, NEG_INF)
    e = jnp.concatenate([e, jnp.full((depth, h, 1, GRID_W, GRID_W), NEG_INF, F32)], axis=2)
    e = jnp.concatenate([e, e], axis=-1)
    n_pat, qr, kr = pats.shape
    assert kr % 2 == 0
    return pl.pallas_call(
        functools.partial(_na_bias_kernel, pats=pats),
        out_shape=jax.ShapeDtypeStruct((depth, n_pat, h, qr * GRID_W, kr * GRID_W), F32),
        grid=(depth, h),
        in_specs=[pl.BlockSpec((None, None, n_dr, GRID_W, LANES), lambda l, hd: (l, hd, 0, 0, 0))],
        out_specs=pl.BlockSpec((None, n_pat, None, qr * GRID_W, kr * GRID_W), lambda l, hd: (l, 0, hd, 0, 0)),
        compiler_params=_cp(("parallel", "parallel"), 40),
        name="na_bias",
    )(e)


def _na_lat_kernel(pat_ref, kb_ref, q_ref, k_ref, v_ref, kc_ref, vc_ref, bias_ref, prev_ref, o_ref, s_ref, *, nk):
    del pat_ref, prev_ref
    blk = pl.program_id(2)
    kstart = pl.multiple_of(kb_ref[blk] * GRID_W, GRID_W)
    lo = _lane_lo()
    tq = q_ref.shape[0]
    n_units = bias_ref.shape[0]
    chunks = _chunks((nk, kc_ref.shape[0]), ATT_CK)

    def lanes_of(unit):
        return slice((unit // 2) * LANES, (unit // 2 + 1) * LANES)

    def q_of(unit):
        q = q_ref[:, lanes_of(unit)] * HEAD64_QSCALE
        return (jnp.where(lo, q, 0.0) if unit % 2 == 0 else jnp.where(lo, 0.0, q)).astype(BF16)

    def kv_chunk(refs, unit, chunk):
        si, st, n, _ = chunk
        if si == 0:
            return refs[0][pl.ds(kstart + st, n), lanes_of(unit)].astype(BF16)
        return refs[1][st:st + n, lanes_of(unit)].astype(BF16)

    def stage_a(unit, qm, chunk, mrun):
        si, st, n, off = chunk
        s = _scores(qm, kv_chunk((k_ref, kc_ref), unit, chunk))
        if si == 0:
            s = s + bias_ref[unit, :, st:st + n]
        s_ref[unit % 2, :, off:off + n] = s
        return _tile_fold(jnp.maximum, mrun, s)

    def stage_b(unit, chunk, m, lrun, acc):
        _, _, n, off = chunk
        p = jnp.exp2(s_ref[unit % 2, :, off:off + n] - m)
        lrun = _tile_fold(jnp.add, lrun, p)
        acc = acc + jnp.dot(p.astype(BF16), kv_chunk((v_ref, vc_ref), unit, chunk), preferred_element_type=F32)
        return lrun, acc

    neg = jnp.full((tq, LANES), -jnp.inf, F32)
    zero = jnp.zeros((tq, LANES), F32)
    mrun = neg
    qm = q_of(0)
    for chunk in chunks:
        mrun = stage_a(0, qm, chunk, mrun)
    outs = []
    for unit in range(n_units):
        m = jnp.max(mrun, axis=-1, keepdims=True)
        lrun, acc, mrun = zero, zero, neg
        if unit + 1 < n_units:
            qm = q_of(unit + 1)
        for chunk in chunks:
            if unit + 1 < n_units:
                mrun = stage_a(unit + 1, qm, chunk, mrun)
            lrun, acc = stage_b(unit, chunk, m, lrun, acc)
        outs.append(acc / jnp.sum(lrun, axis=-1, keepdims=True))
    for hq in range(n_units // 2):
        o_ref[:, hq * LANES:(hq + 1) * LANES] = jnp.where(lo, outs[2 * hq], outs[2 * hq + 1]).astype(o_ref.dtype)


def _na_attn_lat(z, cache_k, cache_v, bias, plan, prev, layer, t_ctx, db, ds, past):
    kbase, _, pat_of, krows = plan
    hs = NA_STEP_HEADS
    w = hs * NA_HD
    tq = NA_QROWS * GRID_W
    nk = krows * GRID_W
    nblk = ds // tq
    q0 = t_ctx // tq
    s0 = t_ctx // ds
    grid_spec = pltpu.PrefetchScalarGridSpec(
        num_scalar_prefetch=2,
        grid=(db, NA_HEADS // hs, nblk),
        in_specs=[pl.BlockSpec((tq, w), lambda b, p, i, pat, kb: (q0 + b * nblk + i, C_NQ // w + p)),
                  pl.BlockSpec((ds, w), lambda b, p, i, pat, kb: (s0 + b, C_NK // w + p)),
                  pl.BlockSpec((ds, w), lambda b, p, i, pat, kb: (s0 + b, C_NV // w + p)),
                  pl.BlockSpec((None, None, past, w), lambda b, p, i, pat, kb: (b, layer, 0, p)),
                  pl.BlockSpec((None, None, past, w), lambda b, p, i, pat, kb: (b, layer, 0, p)),
                  pl.BlockSpec((None, None, hs, tq, nk), lambda b, p, i, pat, kb: (layer, pat[i], p, 0, 0)),
                  pl.BlockSpec(memory_space=pl.ANY)],
        out_specs=pl.BlockSpec((tq, w), lambda b, p, i, pat, kb: (q0 + b * nblk + i, p)),
        scratch_shapes=[pltpu.VMEM((2, tq, nk + past), F32)],
    )
    return pl.pallas_call(
        functools.partial(_na_lat_kernel, nk=nk),
        out_shape=jax.ShapeDtypeStruct(prev.shape, prev.dtype),
        grid_spec=grid_spec,
        input_output_aliases={8: 0},
        compiler_params=_cp(("parallel", "parallel", "arbitrary"), 56),
        name="na_attn_lat",
    )(jnp.asarray(pat_of), jnp.asarray(kbase), z, z, z, cache_k, cache_v, bias, prev)


def _pool_kernel(*refs, t):
    u_ref, w_ref, sc_ref = refs[:3]
    o_ref, pad_ref = refs[-2:]
    halo = 8
    pos = lax.broadcasted_iota(jnp.int32, (t, POOL_GROUP), 0)
    zeros = jnp.zeros((halo, POOL_GROUP), F32)
    for gi, w in enumerate(POOL_WINDOWS):
        sl = slice(gi * POOL_GROUP, (gi + 1) * POOL_GROUP)
        u = u_ref[:, sl]
        pad_ref[0:halo, :] = zeros
        pad_ref[halo + t:2 * halo + t, :] = zeros
        pad_ref[halo:halo + t, :] = u
        tot = None
        for d in range(-(w // 2), w // 2):
            part = pad_ref[halo + d:halo + d + t, :]
            tot = part if tot is None else tot + part
        cnt = (jnp.minimum(pos + w // 2, t) - jnp.maximum(pos - w // 2, 0)).astype(F32)
        pooled = (tot / cnt - u).astype(BF16)
        mixed = jnp.dot(pooled, w_ref[gi], preferred_element_type=F32)
        o_ref[:, sl] = (mixed * sc_ref[:, sl]).astype(o_ref.dtype)


def _pool(z, pool_w, pool_scale, layer, out, prev, n_seq, seq, row0, name):
    width = len(POOL_WINDOWS) * POOL_GROUP
    aliased = prev is not None
    in_specs = [pl.BlockSpec((seq, width), lambda s: (row0 + s, C_POOL // width)),
                pl.BlockSpec((None, len(POOL_WINDOWS), POOL_GROUP, POOL_GROUP), lambda s: (layer, 0, 0, 0)),
                pl.BlockSpec((1, width), lambda s: (0, 0))]
    args = [z, pool_w, pool_scale]
    if aliased:
        in_specs.append(pl.BlockSpec(memory_space=pl.ANY))
        args.append(prev)
        out = jax.ShapeDtypeStruct(prev.shape, prev.dtype)
    return pl.pallas_call(
        functools.partial(_pool_kernel, t=seq),
        out_shape=out,
        grid=(n_seq,),
        in_specs=in_specs,
        out_specs=pl.BlockSpec((seq, width), lambda s: (row0 + s, 0)),
        scratch_shapes=[pltpu.VMEM((seq + 16, POOL_GROUP), F32)],
        input_output_aliases={3: 0} if aliased else {},
        compiler_params=_cp(("parallel",), 56),
        name=name,
    )(*args)


def _diff_prep_kernel(q_ref, k_ref, v_ref, cos_ref, sin_ref, qo_ref, ko_ref, vo_ref):
    cos = cos_ref[...]
    sin = sin_ref[...]
    lane = lax.broadcasted_iota(jnp.int32, (1, LANES), 1)
    first = (lane % DIFF_HD) < DIFF_HD // 2
    for h in range(DIFF_HEADS):
        sl = slice(h * LANES, (h + 1) * LANES)
        qo_ref[:, sl] = (_rope_block(q_ref[:, sl], cos, sin, first, DIFF_HD // 2) * HEAD64_QSCALE).astype(BF16)
        ko_ref[:, sl] = _rope_block(k_ref[:, sl], cos, sin, first, DIFF_HD // 2).astype(BF16)
    vo_ref[...] = v_ref[...].astype(BF16)


def _diff_prep(z, cos_d, sin_d, rope_row_of):
    t = z.shape[0]
    tm = PREP_TM
    w = DIFF_HEADS * LANES
    sds = jax.ShapeDtypeStruct((t, w), BF16)
    return pl.pallas_call(
        _diff_prep_kernel,
        out_shape=(sds, sds, sds),
        grid=(t // tm,),
        in_specs=[pl.BlockSpec((tm, w), lambda i: (i, C_DQ // w)),
                  pl.BlockSpec((tm, w), lambda i: (i, C_DK // w)),
                  pl.BlockSpec((tm, w), lambda i: (i, C_DV // w)),
                  pl.BlockSpec((tm, LANES), lambda i: (rope_row_of(i, tm), 0)),
                  pl.BlockSpec((tm, LANES), lambda i: (rope_row_of(i, tm), 0))],
        out_specs=(pl.BlockSpec((tm, w), lambda i: (i, 0)),) * 3,
        compiler_params=_cp(("parallel",), 40),
        name="diff_prep",
    )(z, z, z, cos_d, sin_d)


def _diff_lambda(linit_ref, lam_ref):
    lam_init = linit_ref[0]
    lp = lam_ref[...]
    lam = (jnp.exp(jnp.sum(lp[0:1] * lp[1:2], axis=-1, keepdims=True))
           - jnp.exp(jnp.sum(lp[2:3] * lp[3:4], axis=-1, keepdims=True)) + lam_init)
    return lam, lam_init


def _diff_ctx_kernel(linit_ref, lam_ref, g_ref, q_ref, k_ref, v_ref, o_ref, *, seq):
    lam, lam_init = _diff_lambda(linit_ref, lam_ref)
    lo = _lane_lo()
    rows = [slice(g * seq, (g + 1) * seq) for g in range(CTX_GROUP)]
    ss = []
    for r in rows:
        q = q_ref[r, :]
        zero = jnp.zeros_like(q)
        ss.append(_scores(jnp.where(lo, q, zero), k_ref[r, :]))
        ss.append(_scores(jnp.where(lo, zero, q), k_ref[r, :]))
    ms = [jnp.max(s, axis=-1, keepdims=True) for s in ss]
    es = [jnp.exp2(s - m) for s, m in zip(ss, ms)]
    ls = [jnp.sum(e, axis=-1, keepdims=True) for e in es]
    ps = [(es[2 * g] * (1.0 / ls[2 * g]) - es[2 * g + 1] * (lam / ls[2 * g + 1])).astype(BF16)
          for g in range(CTX_GROUP)]
    os = [jnp.dot(p, v_ref[r, :], preferred_element_type=F32) for p, r in zip(ps, rows)]
    for o, r in zip(os, rows):
        o_ref[r, :] = (_rms(o, g_ref[...]) * (1.0 - lam_init)).astype(o_ref.dtype)


def _diff_attn_ctx(linit, lam_p, g, q, k, v, out, layer, n_seq, seq):
    smem = pl.BlockSpec(memory_space=pltpu.SMEM)
    rows = CTX_GROUP * seq
    return pl.pallas_call(
        functools.partial(_diff_ctx_kernel, seq=seq),
        out_shape=out,
        grid=(n_seq // CTX_GROUP, DIFF_HEADS),
        in_specs=[smem,
                  pl.BlockSpec((None, 4, DIFF_HD), lambda s, h: (layer, 0, 0)),
                  pl.BlockSpec((1, LANES), lambda s, h: (0, 0)),
                  pl.BlockSpec((rows, LANES), lambda s, h: (s, h)),
                  pl.BlockSpec((rows, LANES), lambda s, h: (s, h)),
                  pl.BlockSpec((rows, LANES), lambda s, h: (s, h))],
        out_specs=pl.BlockSpec((rows, LANES), lambda s, h: (s, h)),
        compiler_params=_cp(("parallel", "parallel"), 40),
        name="diff_attn_ctx",
    )(linit, lam_p, g, q, k, v)


def _diff_lat_kernel(linit_ref, lam_ref, g_ref, q_ref, kl_ref, kc_ref, vl_ref, vc_ref, prev_ref, o_ref, s_ref,
                     *, n_sub):
    del prev_ref
    tq = DIFF_TQ
    lam, lam_init = _diff_lambda(linit_ref, lam_ref)
    lo = _lane_lo()
    k_refs = (kl_ref, kc_ref)
    v_refs = (vl_ref, vc_ref)
    chunks = _chunks((kl_ref.shape[0], kc_ref.shape[0]), ATT_CK)

    def q_maps(qs):
        q = q_ref[qs * tq:(qs + 1) * tq, :]
        zero = jnp.zeros_like(q)
        return jnp.where(lo, q, zero), jnp.where(lo, zero, q)

    def stage_a(qms, slot, chunk, mruns):
        si, st, n, off = chunk
        k = k_refs[si][st:st + n, :].astype(BF16)
        out = []
        for mi in range(2):
            s = _scores(qms[mi], k)
            s_ref[slot, mi, :, off:off + n] = s
            out.append(_tile_fold(jnp.maximum, mruns[mi], s))
        return out

    def stage_b1(slot, chunk, ms, lruns):
        _, _, n, off = chunk
        out = []
        for mi in range(2):
            e = jnp.exp2(s_ref[slot, mi, :, off:off + n] - ms[mi])
            s_ref[slot, mi, :, off:off + n] = e
            out.append(_tile_fold(jnp.add, lruns[mi], e))
        return out

    def stage_b2(slot, chunk, rho, acc):
        si, st, n, off = chunk
        p = s_ref[slot, 0, :, off:off + n] - s_ref[slot, 1, :, off:off + n] * rho
        return acc + jnp.dot(p.astype(BF16), v_refs[si][st:st + n, :].astype(BF16), preferred_element_type=F32)

    neg = jnp.full((tq, LANES), -jnp.inf, F32)
    zero = jnp.zeros((tq, LANES), F32)
    mruns = [neg, neg]
    qms = q_maps(0)
    for chunk in chunks:
        mruns = stage_a(qms, 0, chunk, mruns)
    for qs in range(n_sub):
        slot = qs % 2
        ms = [jnp.max(mr, axis=-1, keepdims=True) for mr in mruns]
        lruns, mruns = [zero, zero], [neg, neg]
        if qs + 1 < n_sub:
            qms = q_maps(qs + 1)
        for chunk in chunks:
            if qs + 1 < n_sub:
                mruns = stage_a(qms, 1 - slot, chunk, mruns)
            lruns = stage_b1(slot, chunk, ms, lruns)
        l1 = jnp.sum(lruns[0], axis=-1, keepdims=True)
        rho = lam * l1 / jnp.sum(lruns[1], axis=-1, keepdims=True)
        acc = zero
        for chunk in chunks:
            acc = stage_b2(slot, chunk, rho, acc)
        o_ref[qs * tq:(qs + 1) * tq, :] = (_rms(acc / l1, g_ref[...]) * (1.0 - lam_init)).astype(o_ref.dtype)


def _diff_attn_lat(linit, lam_p, g, q, k, v, cache_k, cache_v, prev, layer, t_ctx, db, ds, past):
    smem = pl.BlockSpec(memory_space=pltpu.SMEM)
    n_sub = DIFF_SUBTILES
    tqb = n_sub * DIFF_TQ
    nq = ds // tqb
    q0 = t_ctx // tqb
    s0 = t_ctx // ds
    return pl.pallas_call(
        functools.partial(_diff_lat_kernel, n_sub=n_sub),
        out_shape=jax.ShapeDtypeStruct(prev.shape, prev.dtype),
        grid=(db, DIFF_HEADS, nq),
        in_specs=[smem,
                  pl.BlockSpec((None, 4, DIFF_HD), lambda b, h, i: (layer, 0, 0)),
                  pl.BlockSpec((1, LANES), lambda b, h, i: (0, 0)),
                  pl.BlockSpec((tqb, LANES), lambda b, h, i: (q0 + b * nq + i, h)),
                  pl.BlockSpec((ds, LANES), lambda b, h, i: (s0 + b, h)),
                  pl.BlockSpec((None, None, past, LANES), lambda b, h, i: (b, layer, 0, h)),
                  pl.BlockSpec((ds, LANES), lambda b, h, i: (s0 + b, h)),
                  pl.BlockSpec((None, None, past, LANES), lambda b, h, i: (b, layer, 0, h)),
                  pl.BlockSpec(memory_space=pl.ANY)],
        out_specs=pl.BlockSpec((tqb, LANES), lambda b, h, i: (q0 + b * nq + i, h)),
        scratch_shapes=[pltpu.VMEM((2, 2, DIFF_TQ, ds + past), F32)],
        input_output_aliases={8: 0},
        compiler_params=_cp(("parallel", "parallel", "arbitrary"), 56),
        name="diff_attn_lat",
    )(linit, lam_p, g, q, k, cache_k, v, cache_v, prev)


def _merge_out_kernel(h_ref, wg0_ref, wg1_ref, wg2_ref, wg3_ref, oa_ref, ob_ref, oc_ref, od_ref, wbr_ref, wo_ref,
                      x_ref, mod_ref, gn_ref, x1_ref, h2_ref, *, d):
    n = pl.program_id(1)

    @pl.when(n == 0)
    def _():
        x1_ref[...] = jnp.zeros_like(x1_ref)

    h = h_ref[...]
    merged = None
    for bi, (wg_ref, o_ref) in enumerate(((wg0_ref, oa_ref), (wg1_ref, ob_ref), (wg2_ref, oc_ref), (wg3_ref, od_ref))):
        term = (jax.nn.sigmoid(jnp.dot(h, wg_ref[...], preferred_element_type=F32))
                * jnp.dot(o_ref[...], wbr_ref[bi], preferred_element_type=F32))
        merged = term if merged is None else merged + term
    x1_ref[...] += jnp.dot(merged.astype(BF16), wo_ref[...], preferred_element_type=F32)

    @pl.when(n == pl.num_programs(1) - 1)
    def _():
        mod = mod_ref[0]
        gain_mix = gn_ref[1:2, :] * mod[:, 2 * d:3 * d]
        gain_mlp = gn_ref[2:3, :] * (1.0 + mod[:, 4 * d:5 * d])
        sh2 = mod[:, 3 * d:4 * d]
        for r0 in range(0, x_ref.shape[0], EPI_ROWS):
            rows = slice(r0, r0 + EPI_ROWS)
            x1 = x_ref[rows, :] + _rms(x1_ref[rows, :], gain_mix)
            x1_ref[rows, :] = x1
            h2_ref[rows, :] = (_rms(x1, gain_mlp) + sh2).astype(BF16)


def _merge_out(h, w_gates, outs, w_br, w_o, layer, x, mod_l, gn, row_of):
    t, d = x.shape
    bw = outs[0].shape[1]
    tm, tn = 512, MERGE_TN
    nt = d // tn
    row_spec = pl.BlockSpec((tm, d), lambda i, n: (i, 0))
    o_spec = pl.BlockSpec((tm, bw), lambda i, n: (i, 0))
    g0 = GATE_COL0 // tn
    g_specs = [pl.BlockSpec((None, d, tn), functools.partial(lambda i, n, bi: (layer, 0, g0 + bi * nt + n), bi=bi))
               for bi in range(4)]
    return pl.pallas_call(
        functools.partial(_merge_out_kernel, d=d),
        out_shape=(jax.ShapeDtypeStruct((t, d), F32), jax.ShapeDtypeStruct((t, d), BF16)),
        grid=(t // tm, nt),
        in_specs=[row_spec, *g_specs,
                  o_spec, o_spec, o_spec, o_spec,
                  pl.BlockSpec((None, 4, bw, tn), lambda i, n: (layer, 0, 0, n)),
                  pl.BlockSpec((None, tn, d), lambda i, n: (layer, n, 0)),
                  row_spec,
                  pl.BlockSpec((1, 1, 6 * d), lambda i, n: (row_of(i, tm), 0, 0)),
                  pl.BlockSpec((4, d), lambda i, n: (0, 0))],
        out_specs=(row_spec, row_spec),
        compiler_params=_cp(("parallel", "arbitrary"), 58),
        name="merge_out",
    )(h, w_gates, w_gates, w_gates, w_gates, *outs, w_br, w_o, x, mod_l, gn)


def _ffn_kernel(*refs, d, emit_next):
    h_ref, wu_ref, wd_ref, x_ref, mod_ref, gn_ref = refs[:6]
    o_ref = refs[8] if emit_next else refs[6]
    f = pl.program_id(1)

    @pl.when(f == 0)
    def _():
        o_ref[...] = jnp.zeros_like(o_ref)

    u = jnp.dot(h_ref[...], wu_ref[...], preferred_element_type=F32)
    u = jnp.square(jnp.maximum(u, 0.0)).astype(BF16)
    o_ref[...] += jnp.dot(u, wd_ref[...], preferred_element_type=F32)

    @pl.when(f == pl.num_programs(1) - 1)
    def _():
        gain_out = gn_ref[3:4, :] * mod_ref[0][:, 5 * d:6 * d]
        if emit_next:
            modn_ref, gnn_ref, hn_ref = refs[6], refs[7], refs[9]
            gain_next = gnn_ref[0:1, :] * (1.0 + modn_ref[0][:, d:2 * d])
            shn = modn_ref[0][:, 0:d]
        for r0 in range(0, x_ref.shape[0], EPI_ROWS):
            rows = slice(r0, r0 + EPI_ROWS)
            x_new = x_ref[rows, :] + _rms(o_ref[rows, :], gain_out)
            o_ref[rows, :] = x_new
            if emit_next:
                hn_ref[rows, :] = (_rms(x_new, gain_next) + shn).astype(BF16)


def _ffn(h2, w_up, w_down, layer, x1, mod_l, gn, row_of, mod_next=None, gn_next=None, row0=0, n_rows=None):
    t, d = x1.shape
    n_rows = t if n_rows is None else n_rows
    dff = w_up.shape[2]
    tm, tf = 512, 1024
    i0 = row0 // tm
    emit_next = mod_next is not None
    mod_spec = pl.BlockSpec((1, 1, 6 * d), lambda i, f: (row_of(i0 + i, tm), 0, 0))
    gn_spec = pl.BlockSpec((4, d), lambda i, f: (0, 0))
    in_row_spec = pl.BlockSpec((tm, d), lambda i, f: (i0 + i, 0))
    out_row_spec = pl.BlockSpec((tm, d), lambda i, f: (i, 0))
    in_specs = [in_row_spec,
                pl.BlockSpec((None, d, tf), lambda i, f: (layer, 0, f)),
                pl.BlockSpec((None, tf, d), lambda i, f: (layer, f, 0)),
                in_row_spec, mod_spec, gn_spec]
    args = [h2, w_up, w_down, x1, mod_l, gn]
    out_shape = jax.ShapeDtypeStruct((n_rows, d), F32)
    out_specs = out_row_spec
    if emit_next:
        in_specs += [mod_spec, gn_spec]
        args += [mod_next, gn_next]
        out_shape = (out_shape, jax.ShapeDtypeStruct((n_rows, d), BF16))
        out_specs = (out_row_spec, out_row_spec)
    return pl.pallas_call(
        functools.partial(_ffn_kernel, d=d, emit_next=emit_next),
        out_shape=out_shape,
        grid=(n_rows // tm, dff // tf),
        in_specs=in_specs,
        out_specs=out_specs,
        compiler_params=_cp(("parallel", "arbitrary"), 58),
        name="ffn",
    )(*args)


def _prep_weights(w_in, w_uq, w_ukv):
    depth, d, _ = w_in.shape
    sizes = (MLA_Q_LORA, MLA_KV_LORA, MLA_ROPE, 512, 512, 512, 512, 512, 512, 512, 4 * d)
    offs = np.cumsum((0,) + sizes)
    part = lambda i: w_in[:, :, offs[i]:offs[i + 1]]
    q_c, kv_c, k_r, na_q, na_k, na_v, pool, dq, dk, dv, gates = (part(i) for i in range(11))
    kr_blk = jnp.pad(k_r, ((0, 0), (0, 0), (MLA_NOPE, LANES - MLA_NOPE - MLA_ROPE)))
    gap = jnp.zeros((depth, d, GATE_COL0 - Z_COLS), w_in.dtype)
    w_cat = jnp.concatenate([q_c, dq, dk, dv, na_q, na_k, na_v, pool, kv_c, kr_blk, gap, gates], axis=-1).astype(BF16)
    hd = MLA_NOPE + MLA_ROPE
    wuq = jnp.pad(w_uq.reshape(depth, MLA_Q_LORA, MLA_HEADS, hd), ((0, 0), (0, 0), (0, 0), (0, LANES - hd)))
    wuq = wuq.reshape(depth, MLA_Q_LORA, MLA_HEADS * LANES).astype(BF16)
    wkv4 = w_ukv.reshape(depth, MLA_KV_LORA, MLA_HEADS, MLA_NOPE + MLA_V)
    wk = jnp.pad(wkv4[..., :MLA_NOPE], ((0, 0), (0, 0), (0, 0), (0, LANES - MLA_NOPE)))
    wk = wk.reshape(depth, MLA_KV_LORA, MLA_HEADS * LANES)
    wv = wkv4[..., MLA_NOPE:].reshape(depth, MLA_KV_LORA, MLA_HEADS * MLA_V)
    wkv = jnp.concatenate([wk, wv], axis=-1).astype(BF16)
    return w_cat, wuq, wkv


def kernel(x_prompt, x_sample, cache_mla_ckv, cache_mla_krope, cache_na_k, cache_na_v, cache_diff_k, cache_diff_v, c, c_ctx, w_mod, b_mod, g_norm, w_in, g_q_lora, g_kv_lora, w_uq, w_ukv, na_rpb, pool_w, pool_scale, diff_lambda, diff_norm_g, w_br, w_o, w_up, w_down):
    nb, seq, d = x_prompt.shape
    db, ds, _ = x_sample.shape
    depth = w_in.shape[0]
    past = cache_mla_ckv.shape[2]
    t_ctx = nb * seq
    t_lat = db * ds
    t = t_ctx + t_lat
    assert t_ctx % ds == 0 and ds % 1024 == 0 and t_ctx % 1024 == 0 and seq % 8 == 0 and nb % CTX_GROUP == 0

    def row_of(i, tm):
        n_ctx = t_ctx // tm
        return jnp.where(i < n_ctx, 0, 1 + (i - n_ctx) // (ds // tm))

    def rope_row_of(i, tm):
        n_ctx = t_ctx // tm
        return jnp.where(i < n_ctx, 0, 1 + (i - n_ctx) % (ds // tm))

    w_cat, wuq, wkv = _prep_weights(w_in, w_uq, w_ukv)
    w_br_b = w_br.astype(BF16)
    w_o_b = w_o.astype(BF16)
    w_up_b = w_up.astype(BF16)
    w_down_b = w_down.astype(BF16)
    pool_w_b = pool_w.astype(BF16)
    cos_m, sin_m, cos_d, sin_d = _rope_tables(ds, PREP_TM)
    rows = ds // GRID_W
    na_plan = _na_plan(rows)
    bias = _na_bias_table(na_rpb, na_plan[1])
    cache_kr_pad = jnp.pad(cache_mla_krope, ((0, 0), (0, 0), (0, 0), (MLA_NOPE, LANES - MLA_NOPE - MLA_ROPE)))
    cache_nk = cache_na_k.reshape(db, depth, past, NA_HEADS * NA_HD)
    cache_nv = cache_na_v.reshape(db, depth, past, NA_HEADS * NA_HD)
    cache_dk = cache_diff_k.reshape(db, depth, past, DIFF_HEADS * 2 * DIFF_HD)
    cache_dv = cache_diff_v.reshape(db, depth, past, DIFF_HEADS * 2 * DIFF_HD)

    n_rows = 1 + db
    r_pad = -(-n_rows // 8) * 8
    cond = jnp.concatenate([c_ctx[None, :], c, jnp.zeros((r_pad - n_rows, d), F32)], axis=0)
    mod = _modulation(cond, w_mod, b_mod)
    mods = [mod[li].reshape(r_pad, 1, 6 * d) for li in range(depth)]

    branch_sds = jax.ShapeDtypeStruct((t, 512), BF16)
    x, h = _norm_mod(x_prompt.reshape(t_ctx, d), x_sample.reshape(t_lat, d), mods[0], g_norm[0][0:1], row_of)
    states = []
    for li in range(depth):
        mod_l = mods[li]
        gn = g_norm[li]
        z = _matmul(h, w_cat, li, GATE_COL0, 1024, 768, F32, "in_proj")

        q_a, ckv, k_a, v_a = _mla_prep(z, cos_m, sin_m, g_q_lora[li][None, :], g_kv_lora[li][None, :],
                                       wuq, wkv, li, rope_row_of)
        kc_a, vc_a = _mla_cache(cache_mla_ckv, cache_kr_pad, wkv, li)
        o_a = _mla_attn_ctx(q_a, k_a, v_a, branch_sds, nb, seq)
        o_a = _mla_attn_lat(q_a, k_a, v_a, kc_a, vc_a, o_a, t_ctx, db, ds, past)

        o_b = _na_attn_ctx(z, branch_sds, nb, seq)
        o_b = _na_attn_lat(z, cache_nk, cache_nv, bias, na_plan, o_b, li, t_ctx, db, ds, past)

        o_c = _pool(z, pool_w_b, pool_scale[li][None, :], li, branch_sds, None, nb, seq, 0, "pool_ctx")
        o_c = _pool(z, pool_w_b, pool_scale[li][None, :], li, None, o_c, db, ds, t_ctx // ds, "pool_lat")

        q_d, k_d, v_d = _diff_prep(z, cos_d, sin_d, rope_row_of)
        linit = jnp.full((1,), 0.8 - 0.6 * math.exp(-0.3 * li), F32)
        g_d = diff_norm_g[li][None, :]
        o_d = _diff_attn_ctx(linit, diff_lambda, g_d, q_d, k_d, v_d, branch_sds, li, nb, seq)
        o_d = _diff_attn_lat(linit, diff_lambda, g_d, q_d, k_d, v_d, cache_dk, cache_dv, o_d,
                             li, t_ctx, db, ds, past)

        x1, h2 = _merge_out(h, w_cat, (o_a, o_b, o_c, o_d), w_br_b, w_o_b, li, x, mod_l, gn, row_of)
        if li + 1 < depth:
            x, h = _ffn(h2, w_up_b, w_down_b, li, x1, mod_l, gn, row_of, mods[li + 1], g_norm[li + 1])
        else:
            y_prompt = _ffn(h2, w_up_b, w_down_b, li, x1, mod_l, gn, row_of, row0=0, n_rows=t_ctx)
            y_sample = _ffn(h2, w_up_b, w_down_b, li, x1, mod_l, gn, row_of, row0=t_ctx, n_rows=t_lat)

        zc = z[:t_ctx]
        states.append((ckv[:t_ctx], zc[:, C_KR + MLA_NOPE:C_KR + MLA_NOPE + MLA_ROPE], zc[:, C_NK:C_NK + 512],
                       zc[:, C_NV:C_NV + 512], zc[:, C_DK:C_DK + 512], zc[:, C_DV:C_DV + 512]))

    def stacked(k, tail):
        per_layer = [s[k].reshape(nb, seq, -1) for s in states]
        return jnp.stack(per_layer, axis=1).reshape(nb, depth, seq, *tail)

    return (y_prompt.reshape(nb, seq, d), y_sample.reshape(db, ds, d),
            stacked(0, (MLA_KV_LORA,)), stacked(1, (MLA_ROPE,)),
            stacked(2, (NA_HEADS, NA_HD)), stacked(3, (NA_HEADS, NA_HD)),
            stacked(4, (DIFF_HEADS, 2 * DIFF_HD)), stacked(5, (DIFF_HEADS, 2 * DIFF_HD)))
```

```python
import functools
import math

import numpy as np
import jax
import jax.numpy as jnp
from jax import lax
from jax.experimental import pallas as pl
from jax.experimental.pallas import tpu as pltpu

F32 = jnp.float32
BF16 = jnp.bfloat16

GRID_W = 64
MLA_HEADS = 8
MLA_NOPE = 64
MLA_ROPE = 32
MLA_V = 64
MLA_Q_LORA = 512
MLA_KV_LORA = 256
NA_HEADS = 8
NA_HD = 64
NA_WIN_R = 8
NA_WIN_C = 16
POOL_WINDOWS = (2, 4, 8, 16)
POOL_GROUP = 128
DIFF_HEADS = 4
DIFF_HD = 64
ROPE_THETA = 10000.0
RMS_EPS = 1e-6
NEG_INF = -1e30
LOG2E = math.log2(math.e)
MLA_QSCALE = (MLA_NOPE + MLA_ROPE) ** -0.5 * LOG2E
HEAD64_QSCALE = 64 ** -0.5 * LOG2E
LANES = 128
MIB = 1024 * 1024

C_QC = 0
C_DQ = 512
C_DK = 1024
C_DV = 1536
C_NQ = 2048
C_NK = 2560
C_NV = 3072
C_POOL = 3584
C_KVC = 4096
C_KR = 4352
Z_COLS = 4480
GATE_COL0 = 4608

MERGE_TN = 512
NA_STEP_HEADS = 4
NA_QROWS = 8
NA_KROWS = 16
ATT_TQ = 256
DIFF_TQ = 256
DIFF_SUBTILES = 4
MLA_SUBTILES = 4
ATT_CK = 512
CTX_GROUP = 4
EPI_ROWS = 16
PREP_TM = 1024


def _cp(sem, vmem_mib):
    return pltpu.CompilerParams(dimension_semantics=sem, vmem_limit_bytes=vmem_mib * MIB)


def _rms(x, g):
    return x * lax.rsqrt(jnp.mean(x * x, axis=-1, keepdims=True) + RMS_EPS) * g


def _scores(q, k):
    return lax.dot_general(q, k, (((1,), (1,)), ((), ())), preferred_element_type=F32)


def _lane_lo():
    return lax.broadcasted_iota(jnp.int32, (1, LANES), 1) < 64


def _tile_fold(op, run, x):
    for c in range(x.shape[1] // LANES):
        run = op(run, x[:, c * LANES:(c + 1) * LANES])
    return run


def _chunks(sizes, ck):
    out, off = [], 0
    for si, n in enumerate(sizes):
        step = min(ck, n)
        for st in range(0, n, step):
            out.append((si, st, step, off))
            off += step
    return out


def _mod_kernel(c_ref, w_ref, b_ref, o_ref):
    c = c_ref[...]
    s = c * jax.nn.sigmoid(c)
    o_ref[0] = jnp.dot(s.astype(BF16), w_ref[0].astype(BF16), preferred_element_type=F32) + b_ref[0]


def _modulation(cond, w_mod, b_mod):
    depth, d, n = w_mod.shape
    r = cond.shape[0]
    tn = 1024
    return pl.pallas_call(
        _mod_kernel,
        out_shape=jax.ShapeDtypeStruct((depth, r, n), F32),
        grid=(depth, n // tn),
        in_specs=[pl.BlockSpec((r, d), lambda l, j: (0, 0)),
                  pl.BlockSpec((1, d, tn), lambda l, j: (l, 0, j)),
                  pl.BlockSpec((1, 1, tn), lambda l, j: (l, 0, j))],
        out_specs=pl.BlockSpec((1, r, tn), lambda l, j: (l, 0, j)),
        compiler_params=_cp(("parallel", "parallel"), 40),
        name="modulation",
    )(cond, w_mod, b_mod.reshape(depth, 1, n))


def _norm_mod_kernel(xp_ref, xs_ref, mod_ref, g_ref, x_ref, h_ref, *, d, n_ctx):
    mod = mod_ref[0]
    sh = mod[:, 0:d]
    sc = mod[:, d:2 * d]

    def emit(src_ref):
        x = src_ref[...]
        x_ref[...] = x
        h_ref[...] = (_rms(x, g_ref[...]) * (1.0 + sc) + sh).astype(BF16)

    @pl.when(pl.program_id(0) < n_ctx)
    def _():
        emit(xp_ref)

    @pl.when(pl.program_id(0) >= n_ctx)
    def _():
        emit(xs_ref)


def _norm_mod(xp, xs, mod_l, g, row_of):
    (t_ctx, d), t_lat = xp.shape, xs.shape[0]
    tm = 512
    n_ctx = t_ctx // tm
    t = t_ctx + t_lat
    return pl.pallas_call(
        functools.partial(_norm_mod_kernel, d=d, n_ctx=n_ctx),
        out_shape=(jax.ShapeDtypeStruct((t, d), F32), jax.ShapeDtypeStruct((t, d), BF16)),
        grid=(t // tm,),
        in_specs=[pl.BlockSpec((tm, d), lambda i: (jnp.minimum(i, n_ctx - 1), 0)),
                  pl.BlockSpec((tm, d), lambda i: (jnp.maximum(i - n_ctx, 0), 0)),
                  pl.BlockSpec((1, 1, 6 * d), lambda i: (row_of(i, tm), 0, 0)),
                  pl.BlockSpec((1, d), lambda i: (0, 0))],
        out_specs=(pl.BlockSpec((tm, d), lambda i: (i, 0)), pl.BlockSpec((tm, d), lambda i: (i, 0))),
        compiler_params=_cp(("parallel",), 48),
        name="norm_mod",
    )(xp, xs, mod_l, g)


def _mm_kernel(a_ref, b_ref, o_ref):
    o_ref[...] = jnp.dot(a_ref[...], b_ref[...], preferred_element_type=F32).astype(o_ref.dtype)


def _matmul(a, b, layer, n, tm, tn, out_dtype, name):
    m, k = a.shape
    return pl.pallas_call(
        _mm_kernel,
        out_shape=jax.ShapeDtypeStruct((m, n), out_dtype),
        grid=(m // tm, n // tn),
        in_specs=[pl.BlockSpec((tm, k), lambda i, j: (i, 0)),
                  pl.BlockSpec((None, k, tn), lambda i, j: (layer, 0, j))],
        out_specs=pl.BlockSpec((tm, tn), lambda i, j: (i, j)),
        compiler_params=_cp(("parallel", "arbitrary"), 48),
        name=name,
    )(a, b)


def _rope_block(x, cos, sin, first_half, half):
    partner = jnp.where(first_half, -pltpu.roll(x, LANES - half, 1), pltpu.roll(x, half, 1))
    return x * cos + partner * sin


def _rope_tables(n_lat, ident_rows):
    t = jnp.arange(n_lat)
    row = (t // GRID_W).astype(F32)
    col = (t % GRID_W).astype(F32)

    def angles(rot_dim):
        n_freq = rot_dim // 4
        inv = ROPE_THETA ** (-jnp.arange(n_freq, dtype=F32) / n_freq)
        return jnp.concatenate([row[:, None] * inv, col[:, None] * inv], axis=-1)

    a_m = angles(MLA_ROPE)
    zeros64 = jnp.zeros((n_lat, 64), F32)
    zeros32 = jnp.zeros((n_lat, 32), F32)
    cos_m = jnp.concatenate([zeros64 + 1.0, jnp.cos(a_m), jnp.cos(a_m), zeros32 + 1.0], axis=-1)
    sin_m = jnp.concatenate([zeros64, jnp.sin(a_m), jnp.sin(a_m), zeros32], axis=-1)
    a_d = angles(DIFF_HD)
    cos_d = jnp.tile(jnp.cos(a_d), (1, 4))
    sin_d = jnp.tile(jnp.sin(a_d), (1, 4))
    one = jnp.ones((ident_rows, LANES), F32)
    zero = jnp.zeros((ident_rows, LANES), F32)
    cat = lambda a, b: jnp.concatenate([a, b], axis=0)
    return cat(one, cos_m), cat(zero, sin_m), cat(one, cos_d), cat(zero, sin_d)


def _mla_prep_kernel(qc_ref, kvc_ref, kr_ref, cos_ref, sin_ref, gq_ref, gkv_ref, wuq_ref, wkv_ref,
                     q_ref, ckv_ref, k_ref, v_ref):
    cos = cos_ref[...]
    sin = sin_ref[...]
    lane = lax.broadcasted_iota(jnp.int32, (1, LANES), 1)
    first = lane < MLA_NOPE + MLA_ROPE // 2
    rope = lambda x: _rope_block(x, cos, sin, first, MLA_ROPE // 2)
    qn = _rms(qc_ref[...], gq_ref[...]).astype(BF16)
    q = jnp.dot(qn, wuq_ref[...], preferred_element_type=F32)
    for h in range(MLA_HEADS):
        sl = slice(h * LANES, (h + 1) * LANES)
        q_ref[:, sl] = (rope(q[:, sl]) * MLA_QSCALE).astype(BF16)
    ckv = _rms(kvc_ref[...], gkv_ref[...])
    ckv_ref[...] = ckv
    kv = jnp.dot(ckv.astype(BF16), wkv_ref[...], preferred_element_type=F32)
    krr = rope(kr_ref[...])
    for h in range(MLA_HEADS):
        sl = slice(h * LANES, (h + 1) * LANES)
        k_ref[:, sl] = (kv[:, sl] + krr).astype(BF16)
    v_ref[...] = kv[:, MLA_HEADS * LANES:].astype(BF16)


def _mla_prep(z, cos_m, sin_m, gq, gkv, wuq, wkv, layer, rope_row_of):
    t = z.shape[0]
    tm = PREP_TM
    kw = MLA_HEADS * LANES
    vw = MLA_HEADS * MLA_V
    return pl.pallas_call(
        _mla_prep_kernel,
        out_shape=(jax.ShapeDtypeStruct((t, kw), BF16), jax.ShapeDtypeStruct((t, MLA_KV_LORA), F32),
                   jax.ShapeDtypeStruct((t, kw), BF16), jax.ShapeDtypeStruct((t, vw), BF16)),
        grid=(t // tm,),
        in_specs=[pl.BlockSpec((tm, MLA_Q_LORA), lambda i: (i, C_QC // MLA_Q_LORA)),
                  pl.BlockSpec((tm, MLA_KV_LORA), lambda i: (i, C_KVC // MLA_KV_LORA)),
                  pl.BlockSpec((tm, LANES), lambda i: (i, C_KR // LANES)),
                  pl.BlockSpec((tm, LANES), lambda i: (rope_row_of(i, tm), 0)),
                  pl.BlockSpec((tm, LANES), lambda i: (rope_row_of(i, tm), 0)),
                  pl.BlockSpec((1, MLA_Q_LORA), lambda i: (0, 0)),
                  pl.BlockSpec((1, MLA_KV_LORA), lambda i: (0, 0)),
                  pl.BlockSpec((None, MLA_Q_LORA, kw), lambda i: (layer, 0, 0)),
                  pl.BlockSpec((None, MLA_KV_LORA, kw + vw), lambda i: (layer, 0, 0))],
        out_specs=(pl.BlockSpec((tm, kw), lambda i: (i, 0)),
                   pl.BlockSpec((tm, MLA_KV_LORA), lambda i: (i, 0)),
                   pl.BlockSpec((tm, kw), lambda i: (i, 0)),
                   pl.BlockSpec((tm, vw), lambda i: (i, 0))),
        compiler_params=_cp(("parallel",), 56),
        name="mla_prep",
    )(z, z, z, cos_m, sin_m, gq, gkv, wuq, wkv)


def _mla_cache_kernel(ckv_ref, kr_ref, wkv_ref, k_ref, v_ref):
    kv = jnp.dot(ckv_ref[...].astype(BF16), wkv_ref[...], preferred_element_type=F32)
    kr = kr_ref[...]
    for h in range(MLA_HEADS):
        sl = slice(h * LANES, (h + 1) * LANES)
        k_ref[:, sl] = (kv[:, sl] + kr).astype(BF16)
    v_ref[...] = kv[:, MLA_HEADS * LANES:].astype(BF16)


def _mla_cache(cache_ckv, cache_kr_pad, wkv, layer):
    db, _, past, _ = cache_ckv.shape
    kw = MLA_HEADS * LANES
    vw = MLA_HEADS * MLA_V
    return pl.pallas_call(
        _mla_cache_kernel,
        out_shape=(jax.ShapeDtypeStruct((db * past, kw), BF16), jax.ShapeDtypeStruct((db * past, vw), BF16)),
        grid=(db,),
        in_specs=[pl.BlockSpec((None, None, past, MLA_KV_LORA), lambda b: (b, layer, 0, 0)),
                  pl.BlockSpec((None, None, past, LANES), lambda b: (b, layer, 0, 0)),
                  pl.BlockSpec((None, MLA_KV_LORA, kw + vw), lambda b: (layer, 0, 0))],
        out_specs=(pl.BlockSpec((past, kw), lambda b: (b, 0)), pl.BlockSpec((past, vw), lambda b: (b, 0))),
        compiler_params=_cp(("parallel",), 40),
        name="mla_cache",
    )(cache_ckv, cache_kr_pad, wkv)


def _phased_softmax_pv(ss, vs):
    ms = [jnp.max(s, axis=-1, keepdims=True) for s in ss]
    ps = [jnp.exp2(s - m) for s, m in zip(ss, ms)]
    ls = [jnp.sum(p, axis=-1, keepdims=True) for p in ps]
    return [jnp.dot(p.astype(BF16), v, preferred_element_type=F32) / l for p, v, l in zip(ps, vs, ls)]


def _mla_ctx_kernel(q_ref, k_ref, v_ref, o_ref, *, seq):
    lo = _lane_lo()
    rows = [slice(g * seq, (g + 1) * seq) for g in range(CTX_GROUP)]
    lanes = [slice(hh * LANES, (hh + 1) * LANES) for hh in range(2)]
    units = [(r, sl) for r in rows for sl in lanes]
    outs = _phased_softmax_pv([_scores(q_ref[r, sl], k_ref[r, sl]) for r, sl in units],
                              [v_ref[r, :] for r, _ in units])
    for g, r in enumerate(rows):
        o_ref[r, :] = jnp.where(lo, outs[2 * g], outs[2 * g + 1]).astype(o_ref.dtype)


def _mla_attn_ctx(q, k, v, out, n_seq, seq):
    pairs = MLA_HEADS // 2
    rows = CTX_GROUP * seq
    return pl.pallas_call(
        functools.partial(_mla_ctx_kernel, seq=seq),
        out_shape=out,
        grid=(n_seq // CTX_GROUP, pairs),
        in_specs=[pl.BlockSpec((rows, 2 * LANES), lambda s, p: (s, p)),
                  pl.BlockSpec((rows, 2 * LANES), lambda s, p: (s, p)),
                  pl.BlockSpec((rows, LANES), lambda s, p: (s, p))],
        out_specs=pl.BlockSpec((rows, LANES), lambda s, p: (s, p)),
        compiler_params=_cp(("parallel", "parallel"), 40),
        name="mla_attn_ctx",
    )(q, k, v)


def _mla_lat_kernel(q_ref, kl_ref, kc_ref, vl_ref, vc_ref, prev_ref, o_ref, s_ref, *, n_sub):
    del prev_ref
    tq = ATT_TQ
    k_refs = (kl_ref, kc_ref)
    v_refs = (vl_ref, vc_ref)
    chunks = _chunks((kl_ref.shape[0], kc_ref.shape[0]), ATT_CK)
    units = [(qs, hh) for qs in range(n_sub) for hh in range(2)]

    def stage_a(unit, slot, chunk, mrun):
        qs, hh = unit
        si, st, n, off = chunk
        sl = slice(hh * LANES, (hh + 1) * LANES)
        s = _scores(q_ref[qs * tq:(qs + 1) * tq, sl], k_refs[si][st:st + n, sl])
        s_ref[slot, :, off:off + n] = s
        return _tile_fold(jnp.maximum, mrun, s)

    def stage_b(slot, chunk, m, lrun, acc):
        si, st, n, off = chunk
        p = jnp.exp2(s_ref[slot, :, off:off + n] - m)
        lrun = _tile_fold(jnp.add, lrun, p)
        acc = acc + jnp.dot(p.astype(BF16), v_refs[si][st:st + n, :], preferred_element_type=F32)
        return lrun, acc

    neg = jnp.full((tq, LANES), -jnp.inf, F32)
    zero = jnp.zeros((tq, LANES), F32)
    mrun = neg
    for chunk in chunks:
        mrun = stage_a(units[0], 0, chunk, mrun)
    outs = {}
    for ui, unit in enumerate(units):
        slot = ui % 2
        m = jnp.max(mrun, axis=-1, keepdims=True)
        lrun, acc, mrun = zero, zero, neg
        for chunk in chunks:
            if ui + 1 < len(units):
                mrun = stage_a(units[ui + 1], 1 - slot, chunk, mrun)
            lrun, acc = stage_b(slot, chunk, m, lrun, acc)
        outs[unit] = acc / jnp.sum(lrun, axis=-1, keepdims=True)
    lo = _lane_lo()
    for qs in range(n_sub):
        o_ref[qs * tq:(qs + 1) * tq, :] = jnp.where(lo, outs[(qs, 0)], outs[(qs, 1)]).astype(o_ref.dtype)


def _mla_attn_lat(q, k, v, kc, vc, prev, t_ctx, db, ds, past):
    pairs = MLA_HEADS // 2
    n_sub = min(MLA_SUBTILES, ds // ATT_TQ)
    tqb = n_sub * ATT_TQ
    nq = ds // tqb
    q0 = t_ctx // tqb
    s0 = t_ctx // ds
    return pl.pallas_call(
        functools.partial(_mla_lat_kernel, n_sub=n_sub),
        out_shape=jax.ShapeDtypeStruct(prev.shape, prev.dtype),
        grid=(db, pairs, nq),
        in_specs=[pl.BlockSpec((tqb, 2 * LANES), lambda b, p, i: (q0 + b * nq + i, p)),
                  pl.BlockSpec((ds, 2 * LANES), lambda b, p, i: (s0 + b, p)),
                  pl.BlockSpec((past, 2 * LANES), lambda b, p, i: (b, p)),
                  pl.BlockSpec((ds, LANES), lambda b, p, i: (s0 + b, p)),
                  pl.BlockSpec((past, LANES), lambda b, p, i: (b, p)),
                  pl.BlockSpec(memory_space=pl.ANY)],
        out_specs=pl.BlockSpec((tqb, LANES), lambda b, p, i: (q0 + b * nq + i, p)),
        scratch_shapes=[pltpu.VMEM((2, ATT_TQ, ds + past), F32)],
        input_output_aliases={5: 0},
        compiler_params=_cp(("parallel", "parallel", "arbitrary"), 56),
        name="mla_attn_lat",
    )(q, k, kc, v, vc, prev)


def _attn64_ctx_kernel(q_ref, k_ref, v_ref, o_ref, *, seq):
    lo = _lane_lo()
    rows = [slice(g * seq, (g + 1) * seq) for g in range(CTX_GROUP)]
    qs = [q_ref[r, :] * HEAD64_QSCALE for r in rows]
    ks = [k_ref[r, :].astype(BF16) for r in rows]
    vs = [v_ref[r, :].astype(BF16) for r in rows]
    ss, vv = [], []
    for q, k, v in zip(qs, ks, vs):
        ss.append(_scores(jnp.where(lo, q, 0.0).astype(BF16), k))
        ss.append(_scores(jnp.where(lo, 0.0, q).astype(BF16), k))
        vv += [v, v]
    outs = _phased_softmax_pv(ss, vv)
    for g, r in enumerate(rows):
        o_ref[r, :] = jnp.where(lo, outs[2 * g], outs[2 * g + 1]).astype(o_ref.dtype)


def _na_attn_ctx(z, out, n_seq, seq):
    pairs = NA_HEADS // 2
    rows = CTX_GROUP * seq
    return pl.pallas_call(
        functools.partial(_attn64_ctx_kernel, seq=seq),
        out_shape=out,
        grid=(n_seq // CTX_GROUP, pairs),
        in_specs=[pl.BlockSpec((rows, LANES), lambda s, p: (s, C_NQ // LANES + p)),
                  pl.BlockSpec((rows, LANES), lambda s, p: (s, C_NK // LANES + p)),
                  pl.BlockSpec((rows, LANES), lambda s, p: (s, C_NV // LANES + p))],
        out_specs=pl.BlockSpec((rows, LANES), lambda s, p: (s, p)),
        compiler_params=_cp(("parallel", "parallel"), 40),
        name="na_attn_ctx",
    )(z, z, z)


def _na_plan(rows):
    krows = min(NA_KROWS, rows)
    wr = min(NA_WIN_R, rows)
    nblk = rows // NA_QROWS
    kbase = np.zeros((nblk,), np.int32)
    drmaps = np.zeros((nblk, NA_QROWS, krows), np.int32)
    invalid = 2 * NA_WIN_R - 1
    for blk in range(nblk):
        r0 = blk * NA_QROWS
        kb = int(np.clip(r0 - wr // 2, 0, rows - krows))
        kbase[blk] = kb
        for rr in range(NA_QROWS):
            r = r0 + rr
            w0 = int(np.clip(r - wr // 2, 0, rows - wr))
            assert kb <= w0 and w0 + wr <= kb + krows
            for kk in range(krows):
                krow = kb + kk
                drmaps[blk, rr, kk] = (krow - r + NA_WIN_R - 1) if (w0 <= krow < w0 + wr) else invalid
    pats, pat_of = np.unique(drmaps, axis=0, return_inverse=True)
    return kbase, pats, np.asarray(pat_of, np.int32).reshape(nblk), krows


def _na_bias_kernel(e_ref, o_ref, *, pats):
    lo = _lane_lo()
    n_pat, qr, kr = pats.shape
    for pi in range(n_pat):
        for rr in range(qr):
            for k2 in range(kr // 2):
                a, b = int(pats[pi, rr, 2 * k2]), int(pats[pi, rr, 2 * k2 + 1])
                blk = e_ref[a] if a == b else jnp.where(lo, e_ref[a], e_ref[b])
                o_ref[pi, rr * GRID_W:(rr + 1) * GRID_W, k2 * LANES:(k2 + 1) * LANES] = blk


def _na_bias_table(rpb, pats):
    depth, h = rpb.shape[:2]
    n_dr = 2 * NA_WIN_R
    c = np.arange(GRID_W)[:, None]
    kc = np.arange(GRID_W)[None, :]
    cs = np.clip(c - NA_WIN_C // 2, 0, GRID_W - NA_WIN_C)
    valid = (kc >= cs) & (kc < cs + NA_WIN_C)
    idx = np.clip(kc - c + NA_WIN_C - 1, 0, 2 * NA_WIN_C - 2)
    e = jnp.where(valid[None, None, None], rpb[:, :, :, idx].astype(F32) * LOG2E, NEG_INF)
    e = jnp.concatenate([e, jnp.full((depth, h, 1, GRID_W, GRID_W), NEG_INF, F32)], axis=2)
    e = jnp.concatenate([e, e], axis=-1)
    n_pat, qr, kr = pats.shape
    assert kr % 2 == 0
    return pl.pallas_call(
        functools.partial(_na_bias_kernel, pats=pats),
        out_shape=jax.ShapeDtypeStruct((depth, n_pat, h, qr * GRID_W, kr * GRID_W), F32),
        grid=(depth, h),
        in_specs=[pl.BlockSpec((None, None, n_dr, GRID_W, LANES), lambda l, hd: (l, hd, 0, 0, 0))],
        out_specs=pl.BlockSpec((None, n_pat, None, qr * GRID_W, kr * GRID_W), lambda l, hd: (l, 0, hd, 0, 0)),
        compiler_params=_cp(("parallel", "parallel"), 40),
        name="na_bias",
    )(e)


def _na_lat_kernel(pat_ref, kb_ref, q_ref, k_ref, v_ref, kc_ref, vc_ref, bias_ref, prev_ref, o_ref, s_ref, *, nk):
    del pat_ref, prev_ref
    blk = pl.program_id(2)
    kstart = pl.multiple_of(kb_ref[blk] * GRID_W, GRID_W)
    lo = _lane_lo()
    tq = q_ref.shape[0]
    n_units = bias_ref.shape[0]
    chunks = _chunks((nk, kc_ref.shape[0]), ATT_CK)

    def lanes_of(unit):
        return slice((unit // 2) * LANES, (unit // 2 + 1) * LANES)

    def q_of(unit):
        q = q_ref[:, lanes_of(unit)] * HEAD64_QSCALE
        return (jnp.where(lo, q, 0.0) if unit % 2 == 0 else jnp.where(lo, 0.0, q)).astype(BF16)

    def kv_chunk(refs, unit, chunk):
        si, st, n, _ = chunk
        if si == 0:
            return refs[0][pl.ds(kstart + st, n), lanes_of(unit)].astype(BF16)
        return refs[1][st:st + n, lanes_of(unit)].astype(BF16)

    def stage_a(unit, qm, chunk, mrun):
        si, st, n, off = chunk
        s = _scores(qm, kv_chunk((k_ref, kc_ref), unit, chunk))
        if si == 0:
            s = s + bias_ref[unit, :, st:st + n]
        s_ref[unit % 2, :, off:off + n] = s
        return _tile_fold(jnp.maximum, mrun, s)

    def stage_b(unit, chunk, m, lrun, acc):
        _, _, n, off = chunk
        p = jnp.exp2(s_ref[unit % 2, :, off:off + n] - m)
        lrun = _tile_fold(jnp.add, lrun, p)
        acc = acc + jnp.dot(p.astype(BF16), kv_chunk((v_ref, vc_ref), unit, chunk), preferred_element_type=F32)
        return lrun, acc

    neg = jnp.full((tq, LANES), -jnp.inf, F32)
    zero = jnp.zeros((tq, LANES), F32)
    mrun = neg
    qm = q_of(0)
    for chunk in chunks:
        mrun = stage_a(0, qm, chunk, mrun)
    outs = []
    for unit in range(n_units):
        m = jnp.max(mrun, axis=-1, keepdims=True)
        lrun, acc, mrun = zero, zero, neg
        if unit + 1 < n_units:
            qm = q_of(unit + 1)
        for chunk in chunks:
            if unit + 1 < n_units:
                mrun = stage_a(unit + 1, qm, chunk, mrun)
            lrun, acc = stage_b(unit, chunk, m, lrun, acc)
        outs.append(acc / jnp.sum(lrun, axis=-1, keepdims=True))
    for hq in range(n_units // 2):
        o_ref[:, hq * LANES:(hq + 1) * LANES] = jnp.where(lo, outs[2 * hq], outs[2 * hq + 1]).astype(o_ref.dtype)


def _na_attn_lat(z, cache_k, cache_v, bias, plan, prev, layer, t_ctx, db, ds, past):
    kbase, _, pat_of, krows = plan
    hs = NA_STEP_HEADS
    w = hs * NA_HD
    tq = NA_QROWS * GRID_W
    nk = krows * GRID_W
    nblk = ds // tq
    q0 = t_ctx // tq
    s0 = t_ctx // ds
    grid_spec = pltpu.PrefetchScalarGridSpec(
        num_scalar_prefetch=2,
        grid=(db, NA_HEADS // hs, nblk),
        in_specs=[pl.BlockSpec((tq, w), lambda b, p, i, pat, kb: (q0 + b * nblk + i, C_NQ // w + p)),
                  pl.BlockSpec((ds, w), lambda b, p, i, pat, kb: (s0 + b, C_NK // w + p)),
                  pl.BlockSpec((ds, w), lambda b, p, i, pat, kb: (s0 + b, C_NV // w + p)),
                  pl.BlockSpec((None, None, past, w), lambda b, p, i, pat, kb: (b, layer, 0, p)),
                  pl.BlockSpec((None, None, past, w), lambda b, p, i, pat, kb: (b, layer, 0, p)),
                  pl.BlockSpec((None, None, hs, tq, nk), lambda b, p, i, pat, kb: (layer, pat[i], p, 0, 0)),
                  pl.BlockSpec(memory_space=pl.ANY)],
        out_specs=pl.BlockSpec((tq, w), lambda b, p, i, pat, kb: (q0 + b * nblk + i, p)),
        scratch_shapes=[pltpu.VMEM((2, tq, nk + past), F32)],
    )
    return pl.pallas_call(
        functools.partial(_na_lat_kernel, nk=nk),
        out_shape=jax.ShapeDtypeStruct(prev.shape, prev.dtype),
        grid_spec=grid_spec,
        input_output_aliases={8: 0},
        compiler_params=_cp(("parallel", "parallel", "arbitrary"), 56),
        name="na_attn_lat",
    )(jnp.asarray(pat_of), jnp.asarray(kbase), z, z, z, cache_k, cache_v, bias, prev)


def _pool_kernel(*refs, t):
    u_ref, w_ref, sc_ref = refs[:3]
    o_ref, pad_ref = refs[-2:]
    halo = 8
    pos = lax.broadcasted_iota(jnp.int32, (t, POOL_GROUP), 0)
    zeros = jnp.zeros((halo, POOL_GROUP), F32)
    for gi, w in enumerate(POOL_WINDOWS):
        sl = slice(gi * POOL_GROUP, (gi + 1) * POOL_GROUP)
        u = u_ref[:, sl]
        pad_ref[0:halo, :] = zeros
        pad_ref[halo + t:2 * halo + t, :] = zeros
        pad_ref[halo:halo + t, :] = u
        tot = None
        for d in range(-(w // 2), w // 2):
            part = pad_ref[halo + d:halo + d + t, :]
            tot = part if tot is None else tot + part
        cnt = (jnp.minimum(pos + w // 2, t) - jnp.maximum(pos - w // 2, 0)).astype(F32)
        pooled = (tot / cnt - u).astype(BF16)
        mixed = jnp.dot(pooled, w_ref[gi], preferred_element_type=F32)
        o_ref[:, sl] = (mixed * sc_ref[:, sl]).astype(o_ref.dtype)


def _pool(z, pool_w, pool_scale, layer, out, prev, n_seq, seq, row0, name):
    width = len(POOL_WINDOWS) * POOL_GROUP
    aliased = prev is not None
    in_specs = [pl.BlockSpec((seq, width), lambda s: (row0 + s, C_POOL // width)),
                pl.BlockSpec((None, len(POOL_WINDOWS), POOL_GROUP, POOL_GROUP), lambda s: (layer, 0, 0, 0)),
                pl.BlockSpec((1, width), lambda s: (0, 0))]
    args = [z, pool_w, pool_scale]
    if aliased:
        in_specs.append(pl.BlockSpec(memory_space=pl.ANY))
        args.append(prev)
        out = jax.ShapeDtypeStruct(prev.shape, prev.dtype)
    return pl.pallas_call(
        functools.partial(_pool_kernel, t=seq),
        out_shape=out,
        grid=(n_seq,),
        in_specs=in_specs,
        out_specs=pl.BlockSpec((seq, width), lambda s: (row0 + s, 0)),
        scratch_shapes=[pltpu.VMEM((seq + 16, POOL_GROUP), F32)],
        input_output_aliases={3: 0} if aliased else {},
        compiler_params=_cp(("parallel",), 56),
        name=name,
    )(*args)


def _diff_prep_kernel(q_ref, k_ref, v_ref, cos_ref, sin_ref, qo_ref, ko_ref, vo_ref):
    cos = cos_ref[...]
    sin = sin_ref[...]
    lane = lax.broadcasted_iota(jnp.int32, (1, LANES), 1)
    first = (lane % DIFF_HD) < DIFF_HD // 2
    for h in range(DIFF_HEADS):
        sl = slice(h * LANES, (h + 1) * LANES)
        qo_ref[:, sl] = (_rope_block(q_ref[:, sl], cos, sin, first, DIFF_HD // 2) * HEAD64_QSCALE).astype(BF16)
        ko_ref[:, sl] = _rope_block(k_ref[:, sl], cos, sin, first, DIFF_HD // 2).astype(BF16)
    vo_ref[...] = v_ref[...].astype(BF16)


def _diff_prep(z, cos_d, sin_d, rope_row_of):
    t = z.shape[0]
    tm = PREP_TM
    w = DIFF_HEADS * LANES
    sds = jax.ShapeDtypeStruct((t, w), BF16)
    return pl.pallas_call(
        _diff_prep_kernel,
        out_shape=(sds, sds, sds),
        grid=(t // tm,),
        in_specs=[pl.BlockSpec((tm, w), lambda i: (i, C_DQ // w)),
                  pl.BlockSpec((tm, w), lambda i: (i, C_DK // w)),
                  pl.BlockSpec((tm, w), lambda i: (i, C_DV // w)),
                  pl.BlockSpec((tm, LANES), lambda i: (rope_row_of(i, tm), 0)),
                  pl.BlockSpec((tm, LANES), lambda i: (rope_row_of(i, tm), 0))],
        out_specs=(pl.BlockSpec((tm, w), lambda i: (i, 0)),) * 3,
        compiler_params=_cp(("parallel",), 40),
        name="diff_prep",
    )(z, z, z, cos_d, sin_d)


def _diff_lambda(linit_ref, lam_ref):
    lam_init = linit_ref[0]
    lp = lam_ref[...]
    lam = (jnp.exp(jnp.sum(lp[0:1] * lp[1:2], axis=-1, keepdims=True))
           - jnp.exp(jnp.sum(lp[2:3] * lp[3:4], axis=-1, keepdims=True)) + lam_init)
    return lam, lam_init


def _diff_ctx_kernel(linit_ref, lam_ref, g_ref, q_ref, k_ref, v_ref, o_ref, *, seq):
    lam, lam_init = _diff_lambda(linit_ref, lam_ref)
    lo = _lane_lo()
    rows = [slice(g * seq, (g + 1) * seq) for g in range(CTX_GROUP)]
    ss = []
    for r in rows:
        q = q_ref[r, :]
        zero = jnp.zeros_like(q)
        ss.append(_scores(jnp.where(lo, q, zero), k_ref[r, :]))
        ss.append(_scores(jnp.where(lo, zero, q), k_ref[r, :]))
    ms = [jnp.max(s, axis=-1, keepdims=True) for s in ss]
    es = [jnp.exp2(s - m) for s, m in zip(ss, ms)]
    ls = [jnp.sum(e, axis=-1, keepdims=True) for e in es]
    ps = [(es[2 * g] * (1.0 / ls[2 * g]) - es[2 * g + 1] * (lam / ls[2 * g + 1])).astype(BF16)
          for g in range(CTX_GROUP)]
    os = [jnp.dot(p, v_ref[r, :], preferred_element_type=F32) for p, r in zip(ps, rows)]
    for o, r in zip(os, rows):
        o_ref[r, :] = (_rms(o, g_ref[...]) * (1.0 - lam_init)).astype(o_ref.dtype)


def _diff_attn_ctx(linit, lam_p, g, q, k, v, out, layer, n_seq, seq):
    smem = pl.BlockSpec(memory_space=pltpu.SMEM)
    rows = CTX_GROUP * seq
    return pl.pallas_call(
        functools.partial(_diff_ctx_kernel, seq=seq),
        out_shape=out,
        grid=(n_seq // CTX_GROUP, DIFF_HEADS),
        in_specs=[smem,
                  pl.BlockSpec((None, 4, DIFF_HD), lambda s, h: (layer, 0, 0)),
                  pl.BlockSpec((1, LANES), lambda s, h: (0, 0)),
                  pl.BlockSpec((rows, LANES), lambda s, h: (s, h)),
                  pl.BlockSpec((rows, LANES), lambda s, h: (s, h)),
                  pl.BlockSpec((rows, LANES), lambda s, h: (s, h))],
        out_specs=pl.BlockSpec((rows, LANES), lambda s, h: (s, h)),
        compiler_params=_cp(("parallel", "parallel"), 40),
        name="diff_attn_ctx",
    )(linit, lam_p, g, q, k, v)


def _diff_lat_kernel(linit_ref, lam_ref, g_ref, q_ref, kl_ref, kc_ref, vl_ref, vc_ref, prev_ref, o_ref, s_ref,
                     *, n_sub):
    del prev_ref
    tq = DIFF_TQ
    lam, lam_init = _diff_lambda(linit_ref, lam_ref)
    lo = _lane_lo()
    k_refs = (kl_ref, kc_ref)
    v_refs = (vl_ref, vc_ref)
    chunks = _chunks((kl_ref.shape[0], kc_ref.shape[0]), ATT_CK)

    def q_maps(qs):
        q = q_ref[qs * tq:(qs + 1) * tq, :]
        zero = jnp.zeros_like(q)
        return jnp.where(lo, q, zero), jnp.where(lo, zero, q)

    def stage_a(qms, slot, chunk, mruns):
        si, st, n, off = chunk
        k = k_refs[si][st:st + n, :].astype(BF16)
        out = []
        for mi in range(2):
            s = _scores(qms[mi], k)
            s_ref[slot, mi, :, off:off + n] = s
            out.append(_tile_fold(jnp.maximum, mruns[mi], s))
        return out

    def stage_b1(slot, chunk, ms, lruns):
        _, _, n, off = chunk
        out = []
        for mi in range(2):
            e = jnp.exp2(s_ref[slot, mi, :, off:off + n] - ms[mi])
            s_ref[slot, mi, :, off:off + n] = e
            out.append(_tile_fold(jnp.add, lruns[mi], e))
        return out

    def stage_b2(slot, chunk, rho, acc):
        si, st, n, off = chunk
        p = s_ref[slot, 0, :, off:off + n] - s_ref[slot, 1, :, off:off + n] * rho
        return acc + jnp.dot(p.astype(BF16), v_refs[si][st:st + n, :].astype(BF16), preferred_element_type=F32)

    neg = jnp.full((tq, LANES), -jnp.inf, F32)
    zero = jnp.zeros((tq, LANES), F32)
    mruns = [neg, neg]
    qms = q_maps(0)
    for chunk in chunks:
        mruns = stage_a(qms, 0, chunk, mruns)
    for qs in range(n_sub):
        slot = qs % 2
        ms = [jnp.max(mr, axis=-1, keepdims=True) for mr in mruns]
        lruns, mruns = [zero, zero], [neg, neg]
        if qs + 1 < n_sub:
            qms = q_maps(qs + 1)
        for chunk in chunks:
            if qs + 1 < n_sub:
                mruns = stage_a(qms, 1 - slot, chunk, mruns)
            lruns = stage_b1(slot, chunk, ms, lruns)
        l1 = jnp.sum(lruns[0], axis=-1, keepdims=True)
        rho = lam * l1 / jnp.sum(lruns[1], axis=-1, keepdims=True)
        acc = zero
        for chunk in chunks:
            acc = stage_b2(slot, chunk, rho, acc)
        o_ref[qs * tq:(qs + 1) * tq, :] = (_rms(acc / l1, g_ref[...]) * (1.0 - lam_init)).astype(o_ref.dtype)


def _diff_attn_lat(linit, lam_p, g, q, k, v, cache_k, cache_v, prev, layer, t_ctx, db, ds, past):
    smem = pl.BlockSpec(memory_space=pltpu.SMEM)
    n_sub = min(DIFF_SUBTILES, ds // DIFF_TQ)
    tqb = n_sub * DIFF_TQ
    nq = ds // tqb
    q0 = t_ctx // tqb
    s0 = t_ctx // ds
    return pl.pallas_call(
        functools.partial(_diff_lat_kernel, n_sub=n_sub),
        out_shape=jax.ShapeDtypeStruct(prev.shape, prev.dtype),
        grid=(db, DIFF_HEADS, nq),
        in_specs=[smem,
                  pl.BlockSpec((None, 4, DIFF_HD), lambda b, h, i: (layer, 0, 0)),
                  pl.BlockSpec((1, LANES), lambda b, h, i: (0, 0)),
                  pl.BlockSpec((tqb, LANES), lambda b, h, i: (q0 + b * nq + i, h)),
                  pl.BlockSpec((ds, LANES), lambda b, h, i: (s0 + b, h)),
                  pl.BlockSpec((None, None, past, LANES), lambda b, h, i: (b, layer, 0, h)),
                  pl.BlockSpec((ds, LANES), lambda b, h, i: (s0 + b, h)),
                  pl.BlockSpec((None, None, past, LANES), lambda b, h, i: (b, layer, 0, h)),
                  pl.BlockSpec(memory_space=pl.ANY)],
        out_specs=pl.BlockSpec((tqb, LANES), lambda b, h, i: (q0 + b * nq + i, h)),
        scratch_shapes=[pltpu.VMEM((2, 2, DIFF_TQ, ds + past), F32)],
        input_output_aliases={8: 0},
        compiler_params=_cp(("parallel", "parallel", "arbitrary"), 56),
        name="diff_attn_lat",
    )(linit, lam_p, g, q, k, cache_k, v, cache_v, prev)


def _merge_out_kernel(h_ref, wg0_ref, wg1_ref, wg2_ref, wg3_ref, oa_ref, ob_ref, oc_ref, od_ref, wbr_ref, wo_ref,
                      x_ref, mod_ref, gn_ref, x1_ref, h2_ref, *, d):
    n = pl.program_id(1)

    @pl.when(n == 0)
    def _():
        x1_ref[...] = jnp.zeros_like(x1_ref)

    h = h_ref[...]
    merged = None
    for bi, (wg_ref, o_ref) in enumerate(((wg0_ref, oa_ref), (wg1_ref, ob_ref), (wg2_ref, oc_ref), (wg3_ref, od_ref))):
        term = (jax.nn.sigmoid(jnp.dot(h, wg_ref[...], preferred_element_type=F32))
                * jnp.dot(o_ref[...], wbr_ref[bi], preferred_element_type=F32))
        merged = term if merged is None else merged + term
    x1_ref[...] += jnp.dot(merged.astype(BF16), wo_ref[...], preferred_element_type=F32)

    @pl.when(n == pl.num_programs(1) - 1)
    def _():
        mod = mod_ref[0]
        gain_mix = gn_ref[1:2, :] * mod[:, 2 * d:3 * d]
        gain_mlp = gn_ref[2:3, :] * (1.0 + mod[:, 4 * d:5 * d])
        sh2 = mod[:, 3 * d:4 * d]
        for r0 in range(0, x_ref.shape[0], EPI_ROWS):
            rows = slice(r0, r0 + EPI_ROWS)
            x1 = x_ref[rows, :] + _rms(x1_ref[rows, :], gain_mix)
            x1_ref[rows, :] = x1
            h2_ref[rows, :] = (_rms(x1, gain_mlp) + sh2).astype(BF16)


def _merge_out(h, w_gates, outs, w_br, w_o, layer, x, mod_l, gn, row_of):
    t, d = x.shape
    bw = outs[0].shape[1]
    tm, tn = 512, MERGE_TN
    nt = d // tn
    row_spec = pl.BlockSpec((tm, d), lambda i, n: (i, 0))
    o_spec = pl.BlockSpec((tm, bw), lambda i, n: (i, 0))
    g0 = GATE_COL0 // tn
    g_specs = [pl.BlockSpec((None, d, tn), functools.partial(lambda i, n, bi: (layer, 0, g0 + bi * nt + n), bi=bi))
               for bi in range(4)]
    return pl.pallas_call(
        functools.partial(_merge_out_kernel, d=d),
        out_shape=(jax.ShapeDtypeStruct((t, d), F32), jax.ShapeDtypeStruct((t, d), BF16)),
        grid=(t // tm, nt),
        in_specs=[row_spec, *g_specs,
                  o_spec, o_spec, o_spec, o_spec,
                  pl.BlockSpec((None, 4, bw, tn), lambda i, n: (layer, 0, 0, n)),
                  pl.BlockSpec((None, tn, d), lambda i, n: (layer, n, 0)),
                  row_spec,
                  pl.BlockSpec((1, 1, 6 * d), lambda i, n: (row_of(i, tm), 0, 0)),
                  pl.BlockSpec((4, d), lambda i, n: (0, 0))],
        out_specs=(row_spec, row_spec),
        compiler_params=_cp(("parallel", "arbitrary"), 58),
        name="merge_out",
    )(h, w_gates, w_gates, w_gates, w_gates, *outs, w_br, w_o, x, mod_l, gn)


def _ffn_kernel(*refs, d, emit_next):
    h_ref, wu_ref, wd_ref, x_ref, mod_ref, gn_ref = refs[:6]
    o_ref = refs[8] if emit_next else refs[6]
    f = pl.program_id(1)

    @pl.when(f == 0)
    def _():
        o_ref[...] = jnp.zeros_like(o_ref)

    u = jnp.dot(h_ref[...], wu_ref[...], preferred_element_type=F32)
    u = jnp.square(jnp.maximum(u, 0.0)).astype(BF16)
    o_ref[...] += jnp.dot(u, wd_ref[...], preferred_element_type=F32)

    @pl.when(f == pl.num_programs(1) - 1)
    def _():
        gain_out = gn_ref[3:4, :] * mod_ref[0][:, 5 * d:6 * d]
        if emit_next:
            modn_ref, gnn_ref, hn_ref = refs[6], refs[7], refs[9]
            gain_next = gnn_ref[0:1, :] * (1.0 + modn_ref[0][:, d:2 * d])
            shn = modn_ref[0][:, 0:d]
        for r0 in range(0, x_ref.shape[0], EPI_ROWS):
            rows = slice(r0, r0 + EPI_ROWS)
            x_new = x_ref[rows, :] + _rms(o_ref[rows, :], gain_out)
            o_ref[rows, :] = x_new
            if emit_next:
                hn_ref[rows, :] = (_rms(x_new, gain_next) + shn).astype(BF16)


def _ffn(h2, w_up, w_down, layer, x1, mod_l, gn, row_of, mod_next=None, gn_next=None, row0=0, n_rows=None):
    t, d = x1.shape
    n_rows = t if n_rows is None else n_rows
    dff = w_up.shape[2]
    tm, tf = 512, 1024
    i0 = row0 // tm
    emit_next = mod_next is not None
    mod_spec = pl.BlockSpec((1, 1, 6 * d), lambda i, f: (row_of(i0 + i, tm), 0, 0))
    gn_spec = pl.BlockSpec((4, d), lambda i, f: (0, 0))
    in_row_spec = pl.BlockSpec((tm, d), lambda i, f: (i0 + i, 0))
    out_row_spec = pl.BlockSpec((tm, d), lambda i, f: (i, 0))
    in_specs = [in_row_spec,
                pl.BlockSpec((None, d, tf), lambda i, f: (layer, 0, f)),
                pl.BlockSpec((None, tf, d), lambda i, f: (layer, f, 0)),
                in_row_spec, mod_spec, gn_spec]
    args = [h2, w_up, w_down, x1, mod_l, gn]
    out_shape = jax.ShapeDtypeStruct((n_rows, d), F32)
    out_specs = out_row_spec
    if emit_next:
        in_specs += [mod_spec, gn_spec]
        args += [mod_next, gn_next]
        out_shape = (out_shape, jax.ShapeDtypeStruct((n_rows, d), BF16))
        out_specs = (out_row_spec, out_row_spec)
    return pl.pallas_call(
        functools.partial(_ffn_kernel, d=d, emit_next=emit_next),
        out_shape=out_shape,
        grid=(n_rows // tm, dff // tf),
        in_specs=in_specs,
        out_specs=out_specs,
        compiler_params=_cp(("parallel", "arbitrary"), 58),
        name="ffn",
    )(*args)


def _prep_weights(w_in, w_uq, w_ukv):
    depth, d, _ = w_in.shape
    sizes = (MLA_Q_LORA, MLA_KV_LORA, MLA_ROPE, 512, 512, 512, 512, 512, 512, 512, 4 * d)
    offs = np.cumsum((0,) + sizes)
    part = lambda i: w_in[:, :, offs[i]:offs[i + 1]]
    q_c, kv_c, k_r, na_q, na_k, na_v, pool, dq, dk, dv, gates = (part(i) for i in range(11))
    kr_blk = jnp.pad(k_r, ((0, 0), (0, 0), (MLA_NOPE, LANES - MLA_NOPE - MLA_ROPE)))
    gap = jnp.zeros((depth, d, GATE_COL0 - Z_COLS), w_in.dtype)
    w_cat = jnp.concatenate([q_c, dq, dk, dv, na_q, na_k, na_v, pool, kv_c, kr_blk, gap, gates], axis=-1).astype(BF16)
    hd = MLA_NOPE + MLA_ROPE
    wuq = jnp.pad(w_uq.reshape(depth, MLA_Q_LORA, MLA_HEADS, hd), ((0, 0), (0, 0), (0, 0), (0, LANES - hd)))
    wuq = wuq.reshape(depth, MLA_Q_LORA, MLA_HEADS * LANES).astype(BF16)
    wkv4 = w_ukv.reshape(depth, MLA_KV_LORA, MLA_HEADS, MLA_NOPE + MLA_V)
    wk = jnp.pad(wkv4[..., :MLA_NOPE], ((0, 0), (0, 0), (0, 0), (0, LANES - MLA_NOPE)))
    wk = wk.reshape(depth, MLA_KV_LORA, MLA_HEADS * LANES)
    wv = wkv4[..., MLA_NOPE:].reshape(depth, MLA_KV_LORA, MLA_HEADS * MLA_V)
    wkv = jnp.concatenate([wk, wv], axis=-1).astype(BF16)
    return w_cat, wuq, wkv


def kernel(x_prompt, x_sample, cache_mla_ckv, cache_mla_krope, cache_na_k, cache_na_v, cache_diff_k, cache_diff_v, c, c_ctx, w_mod, b_mod, g_norm, w_in, g_q_lora, g_kv_lora, w_uq, w_ukv, na_rpb, pool_w, pool_scale, diff_lambda, diff_norm_g, w_br, w_o, w_up, w_down):
    nb, seq, d = x_prompt.shape
    db, ds, _ = x_sample.shape
    depth = w_in.shape[0]
    past = cache_mla_ckv.shape[2]
    t_ctx = nb * seq
    t_lat = db * ds
    t = t_ctx + t_lat
    assert t_ctx % ds == 0 and ds % 1024 == 0 and t_ctx % 1024 == 0 and seq % 8 == 0 and nb % CTX_GROUP == 0

    def row_of(i, tm):
        n_ctx = t_ctx // tm
        return jnp.where(i < n_ctx, 0, 1 + (i - n_ctx) // (ds // tm))

    def rope_row_of(i, tm):
        n_ctx = t_ctx // tm
        return jnp.where(i < n_ctx, 0, 1 + (i - n_ctx) % (ds // tm))

    w_cat, wuq, wkv = _prep_weights(w_in, w_uq, w_ukv)
    w_br_b = w_br.astype(BF16)
    w_o_b = w_o.astype(BF16)
    w_up_b = w_up.astype(BF16)
    w_down_b = w_down.astype(BF16)
    pool_w_b = pool_w.astype(BF16)
    cos_m, sin_m, cos_d, sin_d = _rope_tables(ds, PREP_TM)
    rows = ds // GRID_W
    na_plan = _na_plan(rows)
    bias = _na_bias_table(na_rpb, na_plan[1])
    cache_kr_pad = jnp.pad(cache_mla_krope, ((0, 0), (0, 0), (0, 0), (MLA_NOPE, LANES - MLA_NOPE - MLA_ROPE)))
    cache_nk = cache_na_k.reshape(db, depth, past, NA_HEADS * NA_HD)
    cache_nv = cache_na_v.reshape(db, depth, past, NA_HEADS * NA_HD)
    cache_dk = cache_diff_k.reshape(db, depth, past, DIFF_HEADS * 2 * DIFF_HD)
    cache_dv = cache_diff_v.reshape(db, depth, past, DIFF_HEADS * 2 * DIFF_HD)

    n_rows = 1 + db
    r_pad = -(-n_rows // 8) * 8
    cond = jnp.concatenate([c_ctx[None, :], c, jnp.zeros((r_pad - n_rows, d), F32)], axis=0)
    mod = _modulation(cond, w_mod, b_mod)
    mods = [mod[li].reshape(r_pad, 1, 6 * d) for li in range(depth)]

    branch_sds = jax.ShapeDtypeStruct((t, 512), BF16)
    x, h = _norm_mod(x_prompt.reshape(t_ctx, d), x_sample.reshape(t_lat, d), mods[0], g_norm[0][0:1], row_of)
    states = []
    for li in range(depth):
        mod_l = mods[li]
        gn = g_norm[li]
        z = _matmul(h, w_cat, li, GATE_COL0, 1024, 768, F32, "in_proj")

        q_a, ckv, k_a, v_a = _mla_prep(z, cos_m, sin_m, g_q_lora[li][None, :], g_kv_lora[li][None, :],
                                       wuq, wkv, li, rope_row_of)
        kc_a, vc_a = _mla_cache(cache_mla_ckv, cache_kr_pad, wkv, li)
        o_a = _mla_attn_ctx(q_a, k_a, v_a, branch_sds, nb, seq)
        o_a = _mla_attn_lat(q_a, k_a, v_a, kc_a, vc_a, o_a, t_ctx, db, ds, past)

        o_b = _na_attn_ctx(z, branch_sds, nb, seq)
        o_b = _na_attn_lat(z, cache_nk, cache_nv, bias, na_plan, o_b, li, t_ctx, db, ds, past)

        o_c = _pool(z, pool_w_b, pool_scale[li][None, :], li, branch_sds, None, nb, seq, 0, "pool_ctx")
        o_c = _pool(z, pool_w_b, pool_scale[li][None, :], li, None, o_c, db, ds, t_ctx // ds, "pool_lat")

        q_d, k_d, v_d = _diff_prep(z, cos_d, sin_d, rope_row_of)
        linit = jnp.full((1,), 0.8 - 0.6 * math.exp(-0.3 * li), F32)
        g_d = diff_norm_g[li][None, :]
        o_d = _diff_attn_ctx(linit, diff_lambda, g_d, q_d, k_d, v_d, branch_sds, li, nb, seq)
        o_d = _diff_attn_lat(linit, diff_lambda, g_d, q_d, k_d, v_d, cache_dk, cache_dv, o_d,
                             li, t_ctx, db, ds, past)

        x1, h2 = _merge_out(h, w_cat, (o_a, o_b, o_c, o_d), w_br_b, w_o_b, li, x, mod_l, gn, row_of)
        if li + 1 < depth:
            x, h = _ffn(h2, w_up_b, w_down_b, li, x1, mod_l, gn, row_of, mods[li + 1], g_norm[li + 1])
        else:
            y_prompt = _ffn(h2, w_up_b, w_down_b, li, x1, mod_l, gn, row_of, row0=0, n_rows=t_ctx)
            y_sample = _ffn(h2, w_up_b, w_down_b, li, x1, mod_l, gn, row_of, row0=t_ctx, n_rows=t_lat)

        zc = z[:t_ctx]
        states.append((ckv[:t_ctx], zc[:, C_KR + MLA_NOPE:C_KR + MLA_NOPE + MLA_ROPE], zc[:, C_NK:C_NK + 512],
                       zc[:, C_NV:C_NV + 512], zc[:, C_DK:C_DK + 512], zc[:, C_DV:C_DV + 512]))

    def stacked(k, tail):
        per_layer = [s[k].reshape(nb, seq, -1) for s in states]
        return jnp.stack(per_layer, axis=1).reshape(nb, depth, seq, *tail)

    return (y_prompt.reshape(nb, seq, d), y_sample.reshape(db, ds, d),
            stacked(0, (MLA_KV_LORA,)), stacked(1, (MLA_ROPE,)),
            stacked(2, (NA_HEADS, NA_HD)), stacked(3, (NA_HEADS, NA_HD)),
            stacked(4, (DIFF_HEADS, 2 * DIFF_HD)), stacked(5, (DIFF_HEADS, 2 * DIFF_HD)))
```
